```python
import math
import jax, jax.numpy as jnp
from jax import lax
import numpy as np

D_MODEL = 2048
BATCH = 4
SEQ = 2048
DEPTH = 4

GRID_W = 64
CTX_LEN = 256
N_MIXERS = 2
S5_GROUP_WIDTH = 16
S5_GROUPS = D_MODEL // S5_GROUP_WIDTH
S5_STATE = 64
S5_DIRS = 2
DT_MIN = 1e-3
DT_MAX = 1e-1
CONV_WIDTH = 31
N_EXPERTS = 16
EC_CAPACITY_FACTOR = 2
EXPERT_FF = D_MODEL // 2
N_S5_LAYERS = (DEPTH + 1) // 2
N_CONV_LAYERS = DEPTH // 2
DEEPNORM_ALPHA = (2.0 * DEPTH) ** 0.25
DEEPNORM_BETA = (8.0 * DEPTH) ** -0.25
LN_EPS = 1e-5

kernel_name = 'hybrid_s5_conformer_ec_moe_diffusion'


def _layer_norm(x, g, b):
    xf = x.astype(jnp.float32)
    mu = jnp.mean(xf, axis=-1, keepdims=True)
    var = jnp.mean(jnp.square(xf - mu), axis=-1, keepdims=True)
    y = (xf - mu) * lax.rsqrt(var + LN_EPS)
    return (y * g.astype(jnp.float32) + b.astype(jnp.float32)).astype(x.dtype)


def _post_norm(x, y, g, b):
    return _layer_norm(DEEPNORM_ALPHA * x + y, g, b)


def _modulate(x, shift, scale):
    return x * (1.0 + scale) + shift


def _scan_combine(left, right):
    a_l, b_l = left
    a_r, b_r = right
    return a_l * a_r, a_r * b_l + b_r


def _s5_states(u, a_bar, b_bar, reverse, h0=None):
    bsz, length, _ = u.shape
    ug = u.astype(jnp.float32).reshape(bsz, length, S5_GROUPS, S5_GROUP_WIDTH).astype(jnp.complex64)
    bu = jnp.einsum('blgh,gph->blgp', ug, b_bar)
    if h0 is not None:
        edge = length - 1 if reverse else 0
        bu = bu.at[:, edge].add(a_bar * h0)
    a = jnp.broadcast_to(a_bar, bu.shape)
    _, h = lax.associative_scan(_scan_combine, (a, bu), axis=1, reverse=reverse)
    return h


def _s5_readout(h, c_mat):
    bsz, length = h.shape[:2]
    y = jnp.einsum('blgp,ghp->blgh', h, c_mat)
    return jnp.real(y).reshape(bsz, length, D_MODEL)


def _s5_output(u, y_fwd, y_bwd, d_skip, w_glu, b_glu):
    y = y_fwd + y_bwd + d_skip.astype(jnp.float32) * u.astype(jnp.float32)
    z = jax.nn.gelu(y).astype(u.dtype)
    val, gate = jnp.split(z @ w_glu + b_glu, 2, axis=-1)
    return val * jax.nn.sigmoid(gate)


def _s5_mixer(u_lat, u_ctx, a_re, a_im, log_dt, b_re, b_im, c_re, c_im, d_skip, w_glu, b_glu,
              ctx_out_needed):
    f32 = jnp.float32
    lam = lax.complex(a_re.astype(f32), a_im.astype(f32))
    dt = jnp.exp(log_dt.astype(f32))[..., None]
    a_bar = jnp.exp(lam * dt)
    b_bar = ((a_bar - 1.0) / lam)[..., None] * lax.complex(b_re.astype(f32), b_im.astype(f32))
    c_mat = lax.complex(c_re.astype(f32), c_im.astype(f32))
    h_ctx_f = _s5_states(u_ctx, a_bar[0], b_bar[0], reverse=False)
    h_ctx_b = _s5_states(u_ctx, a_bar[1], b_bar[1], reverse=True)
    y_lat_f = _s5_readout(_s5_states(u_lat, a_bar[0], b_bar[0], False, h_ctx_f[:, -1]), c_mat[0])
    y_lat_b = _s5_readout(_s5_states(u_lat, a_bar[1], b_bar[1], True, h_ctx_b[:, 0]), c_mat[1])
    y_lat = _s5_output(u_lat, y_lat_f, y_lat_b, d_skip, w_glu, b_glu)
    y_ctx = None
    if ctx_out_needed:
        y_ctx = _s5_output(u_ctx, _s5_readout(h_ctx_f, c_mat[0]), _s5_readout(h_ctx_b, c_mat[1]),
                           d_skip, w_glu, b_glu)
    return y_lat, y_ctx


def _depthwise_conv(h, w_dw, b_dw, rows):
    width = w_dw.shape[0]
    pad = width // 2
    chans = h.shape[-1]
    w = w_dw.astype(h.dtype)
    if rows is None:
        out = lax.conv_general_dilated(h, w[:, None, :], window_strides=(1,), padding=[(pad, pad)],
                                       dimension_numbers=('NWC', 'WIO', 'NWC'),
                                       feature_group_count=chans)
    else:
        bsz, length, _ = h.shape
        grid = h.reshape(bsz, rows, GRID_W, chans)
        out = lax.conv_general_dilated(grid, w[:, None, None, :], window_strides=(1, 1),
                                       padding=[(pad, pad), (0, 0)],
                                       dimension_numbers=('NHWC', 'HWIO', 'NHWC'),
                                       feature_group_count=chans).reshape(bsz, length, chans)
    return out + b_dw


def _conformer_conv(u, w_pw1, b_pw1, w_dw, b_dw, ln_g, ln_b, w_pw2, b_pw2, rows):
    a, g = jnp.split(u @ w_pw1 + b_pw1, 2, axis=-1)
    h = _depthwise_conv(a * jax.nn.sigmoid(g), w_dw, b_dw, rows)
    h = jax.nn.silu(_layer_norm(h, ln_g, ln_b))
    return h @ w_pw2 + b_pw2


def _expert_choice_ffn(u, w_router, w_in, w_out):
    bsz, n_tok, d = u.shape
    cap = EC_CAPACITY_FACTOR * n_tok // N_EXPERTS
    aff = jax.nn.softmax(jnp.einsum('btd,de->bte', u, w_router).astype(jnp.float32), axis=-1)
    gates, idx = lax.top_k(jnp.swapaxes(aff, 1, 2), cap)
    x_sel = jax.vmap(lambda ub, ib: ub[ib])(u, idx)
    g, up = jnp.split(jnp.einsum('becd,edf->becf', x_sel, w_in), 2, axis=-1)
    y = jnp.einsum('becf,efd->becd', jax.nn.silu(g) * up, w_out) * gates[..., None].astype(u.dtype)

    def combine(ib, yb):
        return jnp.zeros((n_tok, d), yb.dtype).at[ib.reshape(-1)].add(yb.reshape(-1, d))

    return jax.vmap(combine)(idx, y)


def setup_inputs(seed: int = 0) -> dict:
    key = jax.random.key(seed)
    ks = jax.random.split(key, 32)
    f32 = jnp.float32
    D, E, F = D_MODEL, N_EXPERTS, EXPERT_FF
    G, P, H, K = S5_GROUPS, S5_STATE, S5_GROUP_WIDTH, CONV_WIDTH
    NS, NC = N_S5_LAYERS, N_CONV_LAYERS

    def nrm(k, shape, s):
        return jax.random.normal(k, shape, f32) * s

    glu_col_scale = jnp.concatenate([jnp.full((D,), DEEPNORM_BETA, f32), jnp.ones((D,), f32)])
    n_idx = jnp.arange(P, dtype=f32)
    return {
        'x': nrm(ks[0], (BATCH, SEQ, D), 1.0),
        'c': nrm(ks[1], (BATCH, D), 1.0),
        'ctx': nrm(ks[2], (BATCH, CTX_LEN, D), 1.0),
        'c_ctx': nrm(ks[3], (D,), 1.0),
        'ada_w': nrm(ks[4], (DEPTH, D, 6 * D), 0.5 * D ** -0.5),
        'ada_b': nrm(ks[5], (DEPTH, 6 * D), 0.01),
        'ln_g': 1.0 + nrm(ks[6], (DEPTH, 2, D), 0.02),
        'ln_b': nrm(ks[7], (DEPTH, 2, D), 0.02),
        's5_a_re': -0.5 + nrm(ks[8], (NS, S5_DIRS, G, P), 0.01),
        's5_a_im': math.pi * n_idx + nrm(ks[9], (NS, S5_DIRS, G, P), 0.01),
        's5_log_dt': jax.random.uniform(ks[10], (NS, S5_DIRS, G), f32, math.log(DT_MIN), math.log(DT_MAX)),
        's5_b_re': nrm(ks[11], (NS, S5_DIRS, G, P, H), (2.0 * H) ** -0.5),
        's5_b_im': nrm(ks[12], (NS, S5_DIRS, G, P, H), (2.0 * H) ** -0.5),
        's5_c_re': nrm(ks[13], (NS, S5_DIRS, G, H, P), P ** -0.5),
        's5_c_im': nrm(ks[14], (NS, S5_DIRS, G, H, P), P ** -0.5),
        's5_d': nrm(ks[15], (NS, D), 1.0),
        's5_w_glu': nrm(ks[16], (NS, D, 2 * D), D ** -0.5) * glu_col_scale,
        's5_b_glu': nrm(ks[17], (NS, 2 * D), 0.01),
        'cv_w_pw1': nrm(ks[18], (NC, D, 2 * D), D ** -0.5),
        'cv_b_pw1': nrm(ks[19], (NC, 2 * D), 0.01),
        'cv_w_dw': nrm(ks[20], (NC, K, D), K ** -0.5),
        'cv_b_dw': nrm(ks[21], (NC, D), 0.01),
        'cv_ln_g': 1.0 + nrm(ks[22], (NC, D), 0.02),
        'cv_ln_b': nrm(ks[23], (NC, D), 0.02),
        'cv_w_pw2': nrm(ks[24], (NC, D, D), DEEPNORM_BETA * D ** -0.5),
        'cv_b_pw2': nrm(ks[25], (NC, D), 0.01),
        'moe_w_router': nrm(ks[26], (DEPTH, D, E), D ** -0.5),
        'moe_w_in': nrm(ks[27], (DEPTH, E, D, 2 * F), D ** -0.5),
        'moe_w_out': nrm(ks[28], (DEPTH, E, F, D), DEEPNORM_BETA * F ** -0.5),
    }


def reference(x, c, ctx, c_ctx, ada_w, ada_b, ln_g, ln_b, s5_a_re, s5_a_im, s5_log_dt, s5_b_re,
              s5_b_im, s5_c_re, s5_c_im, s5_d, s5_w_glu, s5_b_glu, cv_w_pw1, cv_b_pw1, cv_w_dw,
              cv_b_dw, cv_ln_g, cv_ln_b, cv_w_pw2, cv_b_pw2, moe_w_router, moe_w_in, moe_w_out):
    rows = x.shape[1] // GRID_W
    cond_lat = jax.nn.silu(c)
    cond_ctx = jax.nn.silu(c_ctx)
    x_lat, x_ctx = x, ctx
    for i in range(DEPTH):
        is_s5 = (i % N_MIXERS) == 0
        j = i // N_MIXERS
        ctx_out_needed = any((k % N_MIXERS) == 0 for k in range(i + 1, DEPTH))
        ctx_read = is_s5 or ctx_out_needed

        m_lat = (cond_lat @ ada_w[i] + ada_b[i])[:, None, :]
        sh1, sc1, g1, sh2, sc2, g2 = jnp.split(m_lat, 6, axis=-1)
        u_lat = _modulate(x_lat, sh1, sc1)
        if ctx_read:
            cm = jnp.split(cond_ctx @ ada_w[i] + ada_b[i], 6)
            u_ctx = _modulate(x_ctx, cm[0], cm[1])

        if is_s5:
            y_lat, y_ctx = _s5_mixer(u_lat, u_ctx, s5_a_re[j], s5_a_im[j], s5_log_dt[j], s5_b_re[j],
                                     s5_b_im[j], s5_c_re[j], s5_c_im[j], s5_d[j], s5_w_glu[j],
                                     s5_b_glu[j], ctx_out_needed)
        else:
            conv_args = (cv_w_pw1[j], cv_b_pw1[j], cv_w_dw[j], cv_b_dw[j], cv_ln_g[j], cv_ln_b[j],
                         cv_w_pw2[j], cv_b_pw2[j])
            y_lat = _conformer_conv(u_lat, *conv_args, rows)
            y_ctx = _conformer_conv(u_ctx, *conv_args, None) if ctx_out_needed else None

        x_lat = _post_norm(x_lat, g1 * y_lat, ln_g[i, 0], ln_b[i, 0])
        f_lat = _expert_choice_ffn(_modulate(x_lat, sh2, sc2), moe_w_router[i], moe_w_in[i], moe_w_out[i])
        x_lat = _post_norm(x_lat, g2 * f_lat, ln_g[i, 1], ln_b[i, 1])

        if ctx_out_needed:
            x_ctx = _post_norm(x_ctx, cm[2] * y_ctx, ln_g[i, 0], ln_b[i, 0])
            f_ctx = _expert_choice_ffn(_modulate(x_ctx, cm[3], cm[4]), moe_w_router[i], moe_w_in[i],
                                       moe_w_out[i])
            x_ctx = _post_norm(x_ctx, cm[5] * f_ctx, ln_g[i, 1], ln_b[i, 1])
    return x_lat
```

```python
import functools

import jax
import jax.numpy as jnp
from jax import lax
from jax.experimental import pallas as pl
from jax.experimental.pallas import tpu as pltpu

F32 = jnp.float32
BF16 = jnp.bfloat16

DEPTH = 4
N_MIXERS = 2
D_MODEL = 2048
GRID_W = 64
S5_H = 16
S5_P = 64
N_EXPERTS = 16
EC_CAPACITY_FACTOR = 2
DEEPNORM_ALPHA = (2.0 * DEPTH) ** 0.25
LN_EPS = 1e-5

LANES = 128
SUBLANES = 8
S5_CHUNK = 8
S5_GPT = LANES // S5_H
S5_STATE_COLS = S5_GPT * S5_P
MOD_ROWS = 8
VMEM_LIMIT = 56 * 1024 * 1024


def _params(sem, vmem=VMEM_LIMIT):
    return pltpu.CompilerParams(dimension_semantics=sem, vmem_limit_bytes=vmem)


def _layer_norm(v, g, b):
    mu = jnp.mean(v, axis=-1, keepdims=True)
    c = v - mu
    var = jnp.mean(c * c, axis=-1, keepdims=True)
    return c * lax.rsqrt(var + LN_EPS) * g + b


def _split_bf16(v):
    hi = v.astype(BF16)
    lo = (v - hi.astype(F32)).astype(BF16)
    return hi, lo


def _ada_kernel(c_ref, w_ref, b_ref, o_ref):
    c = c_ref[...]
    cond = c * jax.nn.sigmoid(c)
    hi, lo = _split_bf16(cond)
    lhs = jnp.concatenate([hi, lo], axis=0)
    r = jnp.dot(lhs, w_ref[0].astype(BF16), preferred_element_type=F32)
    o_ref[0] = r[:MOD_ROWS] + r[MOD_ROWS:] + b_ref[0]


def _ada_all(c8, ada_w, ada_b):
    depth, d, n = ada_w.shape
    tn = 1024
    return pl.pallas_call(
        _ada_kernel,
        grid=(depth, n // tn),
        in_specs=[
            pl.BlockSpec((MOD_ROWS, d), lambda i, k: (0, 0)),
            pl.BlockSpec((1, d, tn), lambda i, k: (i, 0, k)),
            pl.BlockSpec((1, 1, tn), lambda i, k: (i, 0, k)),
        ],
        out_specs=pl.BlockSpec((1, MOD_ROWS, tn), lambda i, k: (i, 0, k)),
        out_shape=jax.ShapeDtypeStruct((depth, MOD_ROWS, n), F32),
        compiler_params=_params(("arbitrary", "arbitrary")),
        name="adaln",
    )(c8, ada_w, ada_b.reshape(depth, 1, n))


def _s5_operators(a_re, a_im, log_dt, b_re, b_im, c_re, c_im):
    hp = lax.Precision.HIGHEST
    t = S5_CHUNK
    g = a_re.shape[1]
    nj = g // S5_GPT
    lam = lax.complex(a_re.astype(F32), a_im.astype(F32))
    dt = jnp.exp(log_dt.astype(F32))[..., None]
    a_bar = jnp.exp(lam * dt)
    b_bar = ((a_bar - 1.0) / lam)[..., None] * lax.complex(b_re.astype(F32), b_im.astype(F32))
    c_mat = lax.complex(c_re.astype(F32), c_im.astype(F32))
    pows = [jnp.ones_like(a_bar)]
    for _ in range(t):
        pows.append(pows[-1] * a_bar)
    apow = jnp.stack(pows)
    eye = jnp.eye(S5_GPT, dtype=F32)

    kf = jnp.real(jnp.einsum('ghp,kgp,gpj->kghj', c_mat[0], apow[:t, 0], b_bar[0], precision=hp))
    kb = jnp.real(jnp.einsum('ghp,kgp,gpj->kghj', c_mat[1], apow[:t, 1], b_bar[1], precision=hp))
    kall = jnp.concatenate([kb[:0:-1], (kf[0] + kb[0])[None], kf[1:]], axis=0)
    s_idx = jnp.arange(t)[:, None]
    t_idx = jnp.arange(t)[None, :]
    mg = kall[t_idx - s_idx + (t - 1)]
    mg = mg.reshape(t, t, nj, S5_GPT, S5_H, S5_H).transpose(2, 0, 3, 5, 1, 4)
    wm = mg[:, :, :, :, :, None, :] * eye[None, None, :, None, None, :, None]
    wm = wm.reshape(nj, t * LANES, t * LANES)

    exps_b = jnp.stack([jnp.arange(t - 1, -1, -1), jnp.arange(t)])
    ab = jnp.stack([apow[exps_b[d], d] for d in range(2)])
    bfull = ab[..., None] * b_bar[:, None]
    bparts = jnp.stack([jnp.real(bfull), jnp.imag(bfull)], axis=1)
    bparts = bparts.reshape(2, 2, t, nj, S5_GPT, S5_P, S5_H).transpose(3, 2, 4, 6, 0, 1, 5)
    wb = bparts[:, :, :, :, :, :, None, :] * eye[None, None, :, None, None, None, :, None]
    wb = wb.reshape(nj, t * LANES, 2 * 2 * S5_STATE_COLS)

    exps_c = jnp.stack([jnp.arange(1, t + 1), jnp.arange(t, 0, -1)])
    ac = jnp.stack([apow[exps_c[d], d] for d in range(2)])
    q = c_mat[:, None] * ac[:, :, :, None, :]
    cparts = jnp.stack([jnp.real(q), -jnp.imag(q)], axis=1)
    cparts = cparts.reshape(2, 2, t, nj, S5_GPT, S5_H, S5_P).transpose(3, 0, 1, 4, 6, 2, 5)
    wc = cparts[:, :, :, :, :, :, None, :] * eye[None, None, None, :, None, None, :, None]
    wc = wc.reshape(nj, 2 * 2 * S5_STATE_COLS, t * LANES)

    a_t = apow[t]
    rows = jnp.stack([jnp.real(a_t[0]), jnp.imag(a_t[0]), jnp.real(a_t[1]), jnp.imag(a_t[1])])
    rows = rows.reshape(4, nj, S5_STATE_COLS).transpose(1, 0, 2)
    at = jnp.concatenate([rows, jnp.zeros_like(rows)], axis=1)
    return wm.astype(BF16), wb.astype(BF16), wc.astype(BF16), at


def _s5_kernel(x_ref, xc_ref, sh_ref, sc_ref, d_ref, wm_ref, wb_ref, wc_ref, at_ref, *rest,
               nb_half, n_lat, n_ctx, ctx_out):
    if ctx_out:
        z_ref, zc_ref, xf_s, st_s, y_s, zb_s, zcb_s = rest
    else:
        z_ref, xf_s, st_s, y_s, zb_s = rest
        zc_ref = zcb_s = None
    t = S5_CHUNK
    ncl = n_lat // t
    ncc = n_ctx // t
    lat_rows = nb_half * ncl
    half = pl.program_id(1)
    sc_cols = S5_STATE_COLS
    nst = sc_cols // LANES

    for k in range(nb_half):
        b = half * nb_half + k
        scale = 1.0 + sc_ref[pl.ds(b, 1), :]
        shift = sh_ref[pl.ds(b, 1), :]
        for s in range(t):
            xf_s[k * ncl:(k + 1) * ncl, s * LANES:(s + 1) * LANES] = (
                x_ref[pl.ds(k * n_lat + s, ncl, stride=t), :] * scale + shift)
    scale_c = 1.0 + sc_ref[nb_half * 2:nb_half * 2 + 1, :]
    shift_c = sh_ref[nb_half * 2:nb_half * 2 + 1, :]
    for k in range(nb_half):
        for s in range(t):
            r0 = lat_rows + k * ncc
            xf_s[r0:r0 + ncc, s * LANES:(s + 1) * LANES] = (
                xc_ref[pl.ds(k * n_ctx + s, ncc, stride=t), :] * scale_c + shift_c)

    xb = xf_s[...].astype(BF16)
    dvec = jnp.concatenate([d_ref[...]] * t, axis=1)
    y_s[...] = jnp.dot(xb, wm_ref[0], preferred_element_type=F32) + xf_s[...] * dvec

    for d in range(2):
        local = jnp.dot(xb, wb_ref[0, :, d * 2 * sc_cols:(d + 1) * 2 * sc_cols],
                        preferred_element_type=F32)
        for q in range(2 * nst):
            st_s[q] = local[:, q * LANES:(q + 1) * LANES]
        ar = [at_ref[0, 2 * d:2 * d + 1, q * LANES:(q + 1) * LANES] for q in range(nst)]
        ai = [at_ref[0, 2 * d + 1:2 * d + 2, q * LANES:(q + 1) * LANES] for q in range(nst)]

        def step(rows, h, ar=ar, ai=ai):
            new = [None] * (2 * nst)
            for q in range(nst):
                hr, hi = h[q], h[nst + q]
                sr = st_s[q, rows, :]
                si = st_s[nst + q, rows, :]
                st_s[q, rows, :] = hr
                st_s[nst + q, rows, :] = hi
                new[q] = ar[q] * hr - ai[q] * hi + sr
                new[nst + q] = ar[q] * hi + ai[q] * hr + si
            return tuple(new)

        def ctx_rows(c):
            return pl.ds(lat_rows + c, nb_half, stride=ncc)

        def lat_rows_at(c):
            return pl.ds(c, nb_half, stride=ncl)

        zero = tuple(jnp.zeros((nb_half, LANES), F32) for _ in range(2 * nst))
        if d == 0:
            h = lax.fori_loop(0, ncc, lambda c, h: step(ctx_rows(c), h), zero)
            lax.fori_loop(0, ncl, lambda c, h: step(lat_rows_at(c), h), h)
        else:
            h = lax.fori_loop(0, ncc, lambda i, h: step(ctx_rows(ncc - 1 - i), h), zero)
            lax.fori_loop(0, ncl, lambda i, h: step(lat_rows_at(ncl - 1 - i), h), h)

        entering = jnp.concatenate([st_s[q] for q in range(2 * nst)], axis=1)
        y_s[...] += jnp.dot(entering.astype(BF16),
                            wc_ref[0, d * 2 * sc_cols:(d + 1) * 2 * sc_cols, :],
                            preferred_element_type=F32)

    z = jax.nn.gelu(y_s[...])
    for k in range(nb_half):
        for s in range(t):
            zb_s[pl.ds(k * n_lat + s, ncl, stride=t), :] = z[k * ncl:(k + 1) * ncl, s * LANES:(s + 1) * LANES]
    z_ref[...] = zb_s[...].astype(BF16)
    if ctx_out:
        for k in range(nb_half):
            for s in range(t):
                r0 = lat_rows + k * ncc
                zcb_s[pl.ds(k * n_ctx + s, ncc, stride=t), :] = z[r0:r0 + ncc, s * LANES:(s + 1) * LANES]
        zc_ref[...] = zcb_s[...].astype(BF16)


def _s5_core(x, xc, mods, d_skip, ops, ctx_out):
    nb, n_lat, d = x.shape
    n_ctx = xc.shape[1]
    wm, wb, wc, at = ops
    nj = d // LANES
    nb_half = nb // 2
    t = S5_CHUNK
    rows = nb_half * (n_lat + n_ctx) // t
    kern = functools.partial(_s5_kernel, nb_half=nb_half, n_lat=n_lat, n_ctx=n_ctx, ctx_out=ctx_out)
    out_shape = [jax.ShapeDtypeStruct((nb * n_lat, d), BF16)]
    out_specs = [pl.BlockSpec((nb_half * n_lat, LANES), lambda j, h: (h, j))]
    scratch = [pltpu.VMEM((rows, t * LANES), F32), pltpu.VMEM((2 * S5_STATE_COLS // LANES, rows, LANES), F32),
               pltpu.VMEM((rows, t * LANES), F32), pltpu.VMEM((nb_half * n_lat, LANES), F32)]
    if ctx_out:
        out_shape.append(jax.ShapeDtypeStruct((nb * n_ctx, d), BF16))
        out_specs.append(pl.BlockSpec((nb_half * n_ctx, LANES), lambda j, h: (h, j)))
        scratch.append(pltpu.VMEM((nb_half * n_ctx, LANES), F32))
    res = pl.pallas_call(
        kern,
        grid=(nj, 2),
        in_specs=[
            pl.BlockSpec((nb_half * n_lat, LANES), lambda j, h: (h, j)),
            pl.BlockSpec((nb_half * n_ctx, LANES), lambda j, h: (h, j)),
            pl.BlockSpec((MOD_ROWS, LANES), lambda j, h: (0, j)),
            pl.BlockSpec((MOD_ROWS, LANES), lambda j, h, nj=nj: (0, nj + j)),
            pl.BlockSpec((1, LANES), lambda j, h: (0, j)),
            pl.BlockSpec((1,) + wm.shape[1:], lambda j, h: (j, 0, 0)),
            pl.BlockSpec((1,) + wb.shape[1:], lambda j, h: (j, 0, 0)),
            pl.BlockSpec((1,) + wc.shape[1:], lambda j, h: (j, 0, 0)),
            pl.BlockSpec((1,) + at.shape[1:], lambda j, h: (j, 0, 0)),
        ],
        out_specs=out_specs,
        out_shape=out_shape,
        scratch_shapes=scratch,
        compiler_params=_params(("arbitrary", "arbitrary")),
        name="s5_core",
    )(x.reshape(nb * n_lat, d), xc.reshape(nb * n_ctx, d), mods, mods, d_skip.reshape(1, d), wm, wb, wc, at)
    z = res[0].reshape(nb, n_lat, d)
    zc = res[1].reshape(nb, n_ctx, d) if ctx_out else None
    return z, zc


def _mod_row(is_ctx, nb):
    return nb if is_ctx else pl.program_id(0)


def _s5_out_kernel(z_ref, w_ref, b_ref, x_ref, g_ref, lg_ref, lb_ref, o_ref, *, is_ctx, nb):
    d = x_ref.shape[-1]
    acc = jnp.dot(z_ref[0], w_ref[...], preferred_element_type=F32) + b_ref[...]
    y = acc[:, :d] * jax.nn.sigmoid(acc[:, d:])
    gate = g_ref[pl.ds(_mod_row(is_ctx, nb), 1), :]
    o_ref[0] = _layer_norm(DEEPNORM_ALPHA * x_ref[0] + gate * y, lg_ref[...], lb_ref[...])


def _s5_out(z, w_bf, b_glu, x, mods, ln_g, ln_b, is_ctx, nb):
    n, t, d = x.shape
    tm = min(t, 256)
    kern = functools.partial(_s5_out_kernel, is_ctx=is_ctx, nb=nb)
    return pl.pallas_call(
        kern,
        grid=(n, t // tm),
        in_specs=[
            pl.BlockSpec((1, tm, d), lambda b, i: (b, i, 0)),
            pl.BlockSpec((d, 2 * d), lambda b, i: (0, 0), pipeline_mode=pl.Buffered(1)),
            pl.BlockSpec((1, 2 * d), lambda b, i: (0, 0)),
            pl.BlockSpec((1, tm, d), lambda b, i: (b, i, 0)),
            pl.BlockSpec((MOD_ROWS, d), lambda b, i: (0, 2)),
            pl.BlockSpec((1, d), lambda b, i: (0, 0)),
            pl.BlockSpec((1, d), lambda b, i: (0, 0)),
        ],
        out_specs=pl.BlockSpec((1, tm, d), lambda b, i: (b, i, 0)),
        out_shape=jax.ShapeDtypeStruct((n, t, d), F32),
        compiler_params=_params(("arbitrary", "arbitrary")),
        name="s5_out",
    )(z, w_bf, b_glu.reshape(1, 2 * d), x, mods, ln_g.reshape(1, d), ln_b.reshape(1, d))


def _pw1_kernel(x_ref, sh_ref, sc_ref, w_ref, b_ref, o_ref, *, is_ctx, nb):
    d = x_ref.shape[-1]
    row = _mod_row(is_ctx, nb)
    u = x_ref[0] * (1.0 + sc_ref[pl.ds(row, 1), :]) + sh_ref[pl.ds(row, 1), :]
    acc = jnp.dot(u.astype(BF16), w_ref[...], preferred_element_type=F32) + b_ref[...]
    o_ref[0] = acc[:, :d] * jax.nn.sigmoid(acc[:, d:])


def _pw1(x, mods, w_bf, b_pw1, is_ctx, nb):
    n, t, d = x.shape
    tm = min(t, 256)
    kern = functools.partial(_pw1_kernel, is_ctx=is_ctx, nb=nb)
    return pl.pallas_call(
        kern,
        grid=(n, t // tm),
        in_specs=[
            pl.BlockSpec((1, tm, d), lambda b, i: (b, i, 0)),
            pl.BlockSpec((MOD_ROWS, d), lambda b, i: (0, 0)),
            pl.BlockSpec((MOD_ROWS, d), lambda b, i: (0, 1)),
            pl.BlockSpec((d, 2 * d), lambda b, i: (0, 0), pipeline_mode=pl.Buffered(1)),
            pl.BlockSpec((1, 2 * d), lambda b, i: (0, 0)),
        ],
        out_specs=pl.BlockSpec((1, tm, d), lambda b, i: (b, i, 0)),
        out_shape=jax.ShapeDtypeStruct((n, t, d), F32),
        compiler_params=_params(("arbitrary", "arbitrary")),
        name="conv_pw1",
    )(x, mods, mods, w_bf, b_pw1.reshape(1, 2 * d))


def _conv_kernel(h_ref, wdw_ref, bdw_ref, cg_ref, cb_ref, w2_ref, b2_ref, x_ref, g_ref, lg_ref, lb_ref,
                 o_ref, hp_s, cv_s, y_s, *, is_ctx, nb, n_rows, wt):
    kw = wdw_ref.shape[0]
    pad = kw // 2
    d = h_ref.shape[-1]
    zeros = jnp.zeros((pad, wt, d), F32)
    hp_s[0:pad] = zeros
    hp_s[pad + n_rows:pad + n_rows + pad] = zeros
    hp_s[pad:pad + n_rows] = h_ref[0]
    bias = bdw_ref[...]

    def conv_row(r, carry):
        acc = jnp.zeros((wt, d), F32)
        for k in range(kw):
            acc = acc + wdw_ref[k:k + 1, :] * hp_s[r + k]
        cv_s[pl.ds(pl.multiple_of(r * wt, wt), wt), :] = acc + bias
        return carry

    lax.fori_loop(0, n_rows, conv_row, 0)
    hn = _layer_norm(cv_s[...], cg_ref[...], cb_ref[...])
    hn = hn * jax.nn.sigmoid(hn)
    y_s[...] = jnp.dot(hn.astype(BF16), w2_ref[...], preferred_element_type=F32) + b2_ref[...]
    gate = g_ref[pl.ds(_mod_row(is_ctx, nb), 1), :]
    lg = lg_ref[...]
    lb = lb_ref[...]

    def post_row(r, carry):
        y = y_s[pl.ds(pl.multiple_of(r * wt, wt), wt), :]
        o_ref[0, r] = _layer_norm(DEEPNORM_ALPHA * x_ref[0, r] + gate * y, lg, lb)
        return carry

    lax.fori_loop(0, n_rows, post_row, 0)


def _conv_post(h, w_dw, b_dw, cv_g, cv_b, w2_bf, b2, x, mods, ln_g, ln_b, n_rows, width, wt, is_ctx, nb):
    n, t, d = x.shape
    kw = w_dw.shape[0]
    kern = functools.partial(_conv_kernel, is_ctx=is_ctx, nb=nb, n_rows=n_rows, wt=wt)
    blk = pl.BlockSpec((1, n_rows, wt, d), lambda b, i: (b, 0, i, 0))
    vec = pl.BlockSpec((1, d), lambda b, i: (0, 0))
    out = pl.pallas_call(
        kern,
        grid=(n, width // wt),
        in_specs=[
            blk,
            pl.BlockSpec((kw, d), lambda b, i: (0, 0)),
            vec, vec, vec,
            pl.BlockSpec((d, d), lambda b, i: (0, 0), pipeline_mode=pl.Buffered(1)),
            vec,
            blk,
            pl.BlockSpec((MOD_ROWS, d), lambda b, i: (0, 2)),
            vec, vec,
        ],
        out_specs=blk,
        out_shape=jax.ShapeDtypeStruct((n, n_rows, width, d), F32),
        scratch_shapes=[pltpu.VMEM((n_rows + 2 * (kw // 2), wt, d), F32),
                        pltpu.VMEM((n_rows * wt, d), F32),
                        pltpu.VMEM((n_rows * wt, d), F32)],
        compiler_params=_params(("arbitrary", "arbitrary")),
        name="conv_post",
    )(h.reshape(n, n_rows, width, d), w_dw, b_dw.reshape(1, d), cv_g.reshape(1, d), cv_b.reshape(1, d),
      w2_bf, b2.reshape(1, d), x.reshape(n, n_rows, width, d), mods, ln_g.reshape(1, d), ln_b.reshape(1, d))
    return out.reshape(n, t, d)


def _router_kernel(x_ref, sh_ref, sc_ref, wr_ref, u_ref, a_ref, *, is_ctx, nb):
    row = _mod_row(is_ctx, nb)
    u = x_ref[0] * (1.0 + sc_ref[pl.ds(row, 1), :]) + sh_ref[pl.ds(row, 1), :]
    u_ref[0] = u.astype(BF16)
    uh, ul = _split_bf16(u)
    wh, wl = _split_bf16(wr_ref[...])
    nt = (((1,), (1,)), ((), ()))
    logits = (lax.dot_general(wh, uh, nt, preferred_element_type=F32)
              + lax.dot_general(wh, ul, nt, preferred_element_type=F32)
              + lax.dot_general(wl, uh, nt, preferred_element_type=F32))
    m = jnp.max(logits, axis=0, keepdims=True)
    ex = jnp.exp(logits - m)
    a_ref[0] = ex / jnp.sum(ex, axis=0, keepdims=True)


def _router(x, mods, w_router_t, is_ctx, nb):
    n, t, d = x.shape
    e = w_router_t.shape[0]
    tm = min(t, 512)
    kern = functools.partial(_router_kernel, is_ctx=is_ctx, nb=nb)
    return pl.pallas_call(
        kern,
        grid=(n, t // tm),
        in_specs=[
            pl.BlockSpec((1, tm, d), lambda b, i: (b, i, 0)),
            pl.BlockSpec((MOD_ROWS, d), lambda b, i: (0, 3)),
            pl.BlockSpec((MOD_ROWS, d), lambda b, i: (0, 4)),
            pl.BlockSpec((e, d), lambda b, i: (0, 0)),
        ],
        out_specs=[pl.BlockSpec((1, tm, d), lambda b, i: (b, i, 0)),
                   pl.BlockSpec((1, e, tm), lambda b, i: (b, 0, i))],
        out_shape=[jax.ShapeDtypeStruct((n, t, d), BF16), jax.ShapeDtypeStruct((n, e, t), F32)],
        compiler_params=_params(("arbitrary", "arbitrary")),
        name="router",
    )(x, mods, mods, w_router_t)


def _topk_kernel(a_ref, p_ref, pt_ref, *, cap):
    a = a_ref[0]
    e, t = a.shape
    bits = pltpu.bitcast(a, jnp.int32)
    capf = jnp.float32(cap)

    def count(mask):
        return jnp.sum(jnp.where(mask, 1.0, 0.0), axis=1, keepdims=True)

    thr = jnp.zeros((e, 1), jnp.int32)
    for bit in range(30, -1, -1):
        cand = thr | jnp.int32(1 << bit)
        thr = jnp.where(count(bits >= cand) >= capf, cand, thr)
    gt = bits > thr
    eq = bits == thr
    need = capf - count(gt)
    tri = jnp.where(lax.broadcasted_iota(jnp.int32, (t, t), 0) <= lax.broadcasted_iota(jnp.int32, (t, t), 1),
                    1.0, 0.0).astype(BF16)
    eq_f = jnp.where(eq, 1.0, 0.0)
    eq_rank = jnp.dot(eq_f.astype(BF16), tri, preferred_element_type=F32) - eq_f
    sel = jnp.where(gt, 1.0, jnp.where(eq & (eq_rank < need), 1.0, 0.0))
    slot = jnp.dot(sel.astype(BF16), tri, preferred_element_type=F32) - 1.0
    pos = jnp.where(sel > 0.0, slot, -1.0)
    p_ref[0] = pos.astype(jnp.int32)
    padded = jnp.concatenate([pos, jnp.full((LANES - e, t), -1.0, F32)], axis=0)
    pt_ref[0] = padded.T.astype(jnp.int32)


def _topk(aff_t, cap):
    n, e, t = aff_t.shape
    kern = functools.partial(_topk_kernel, cap=cap)
    return pl.pallas_call(
        kern,
        grid=(n,),
        in_specs=[pl.BlockSpec((1, e, t), lambda b: (b, 0, 0))],
        out_specs=[pl.BlockSpec((1, e, t), lambda b: (b, 0, 0)),
                   pl.BlockSpec((1, t, LANES), lambda b: (b, 0, 0))],
        out_shape=[jax.ShapeDtypeStruct((n, e, t), jnp.int32), jax.ShapeDtypeStruct((n, t, LANES), jnp.int32)],
        compiler_params=_params(("arbitrary",)),
        name="topk",
    )(aff_t)


def _expert_kernel(u_ref, p_ref, a_ref, wi_ref, wo_ref, y_ref, *, cap):
    pos = p_ref[0, 0]
    t = pos.shape[1]
    ff = wo_ref.shape[1]
    hit = lax.broadcasted_iota(jnp.int32, (cap, t), 0) == pos
    onehot = jnp.where(hit, 1.0, 0.0).astype(BF16)
    x_sel = jnp.dot(onehot, u_ref[0], preferred_element_type=F32).astype(BF16)
    gate = jnp.sum(jnp.where(hit, a_ref[0, 0], 0.0), axis=1, keepdims=True)
    h = jnp.dot(x_sel, wi_ref[0], preferred_element_type=F32)
    g = h[:, :ff]
    act = (g * jax.nn.sigmoid(g)) * h[:, ff:]
    y = jnp.dot(act.astype(BF16), wo_ref[0], preferred_element_type=F32)
    y_ref[0] = (y * gate).astype(BF16)


def _experts(u2, pos_t, aff_t, w_in_bf, w_out_bf, cap):
    n, t, d = u2.shape
    e = w_in_bf.shape[0]
    ff = w_out_bf.shape[1]
    kern = functools.partial(_expert_kernel, cap=cap)
    return pl.pallas_call(
        kern,
        grid=(e, n),
        in_specs=[
            pl.BlockSpec((1, t, d), lambda ei, b: (b, 0, 0)),
            pl.BlockSpec((1, 1, 1, t), lambda ei, b: (b, ei, 0, 0)),
            pl.BlockSpec((1, 1, 1, t), lambda ei, b: (b, ei, 0, 0)),
            pl.BlockSpec((1, d, 2 * ff), lambda ei, b: (ei, 0, 0)),
            pl.BlockSpec((1, ff, d), lambda ei, b: (ei, 0, 0)),
        ],
        out_specs=pl.BlockSpec((1, cap, d), lambda ei, b: (b, ei, 0)),
        out_shape=jax.ShapeDtypeStruct((n, e * cap, d), BF16),
        compiler_params=_params(("arbitrary", "arbitrary")),
        name="experts",
    )(u2, pos_t.reshape(n, e, 1, t), aff_t.reshape(n, e, 1, t), w_in_bf, w_out_bf)


def _combine_kernel(y_ref, pt_ref, x_ref, g_ref, lg_ref, lb_ref, o_ref, *, is_ctx, nb, cap, n_exp):
    ec = n_exp * cap
    src = lax.broadcasted_iota(jnp.int32, (LANES, ec), 0)
    dst = lax.broadcasted_iota(jnp.int32, (LANES, ec), 1)
    expand = jnp.where((dst >= src * cap) & (dst < (src + 1) * cap), 1.0, 0.0).astype(BF16)
    pe = jnp.dot(pt_ref[0].astype(F32).astype(BF16), expand, preferred_element_type=F32)
    slot = lax.broadcasted_iota(jnp.int32, (1, ec), 1) % cap
    onehot = jnp.where(pe == slot.astype(F32), 1.0, 0.0).astype(BF16)
    f = jnp.dot(onehot, y_ref[0], preferred_element_type=F32)
    gate = g_ref[pl.ds(_mod_row(is_ctx, nb), 1), :]
    o_ref[0] = _layer_norm(DEEPNORM_ALPHA * x_ref[0] + gate * f, lg_ref[...], lb_ref[...])


def _combine(y, pos_tm, x, mods, ln_g, ln_b, cap, is_ctx, nb):
    n, t, d = x.shape
    ec = y.shape[1]
    tm = min(t, 256)
    kern = functools.partial(_combine_kernel, is_ctx=is_ctx, nb=nb, cap=cap, n_exp=ec // cap)
    return pl.pallas_call(
        kern,
        grid=(n, t // tm),
        in_specs=[
            pl.BlockSpec((1, ec, d), lambda b, i: (b, 0, 0), pipeline_mode=pl.Buffered(1)),
            pl.BlockSpec((1, tm, LANES), lambda b, i: (b, i, 0)),
            pl.BlockSpec((1, tm, d), lambda b, i: (b, i, 0)),
            pl.BlockSpec((MOD_ROWS, d), lambda b, i: (0, 5)),
            pl.BlockSpec((1, d), lambda b, i: (0, 0)),
            pl.BlockSpec((1, d), lambda b, i: (0, 0)),
        ],
        out_specs=pl.BlockSpec((1, tm, d), lambda b, i: (b, i, 0)),
        out_shape=jax.ShapeDtypeStruct((n, t, d), F32),
        compiler_params=_params(("arbitrary", "arbitrary")),
        name="combine",
    )(y, pos_tm, x, mods, ln_g.reshape(1, d), ln_b.reshape(1, d))


def _moe_post(x1, mods, w_router_t, w_in_bf, w_out_bf, ln_g, ln_b, is_ctx, nb):
    n_tok = x1.shape[1]
    cap = EC_CAPACITY_FACTOR * n_tok // N_EXPERTS
    u2, aff_t = _router(x1, mods, w_router_t, is_ctx, nb)
    pos_t, pos_tm = _topk(aff_t, cap)
    y = _experts(u2, pos_t, aff_t, w_in_bf, w_out_bf, cap)
    return _combine(y, pos_tm, x1, mods, ln_g, ln_b, cap, is_ctx, nb)


def kernel(x, c, ctx, c_ctx, ada_w, ada_b, ln_g, ln_b, s5_a_re, s5_a_im, s5_log_dt, s5_b_re, s5_b_im,
           s5_c_re, s5_c_im, s5_d, s5_w_glu, s5_b_glu, cv_w_pw1, cv_b_pw1, cv_w_dw, cv_b_dw, cv_ln_g,
           cv_ln_b, cv_w_pw2, cv_b_pw2, moe_w_router, moe_w_in, moe_w_out):
    nb, seq, d = x.shape
    n_ctx = ctx.shape[1]
    assert d == D_MODEL and nb + 1 <= MOD_ROWS and nb % 2 == 0
    rows = seq // GRID_W

    c8 = jnp.concatenate([c.astype(F32), c_ctx.astype(F32)[None], jnp.zeros((MOD_ROWS - nb - 1, d), F32)], axis=0)
    mods_all = _ada_all(c8, ada_w, ada_b)

    x_lat, x_ctx = x, ctx
    for i in range(DEPTH):
        is_s5 = (i % N_MIXERS) == 0
        j = i // N_MIXERS
        ctx_out = any((k % N_MIXERS) == 0 for k in range(i + 1, DEPTH))
        mods = mods_all[i]

        if is_s5:
            ops = _s5_operators(s5_a_re[j], s5_a_im[j], s5_log_dt[j], s5_b_re[j], s5_b_im[j],
                                s5_c_re[j], s5_c_im[j])
            z_lat, z_ctx = _s5_core(x_lat, x_ctx, mods, s5_d[j], ops, ctx_out)
            w_glu = s5_w_glu[j].astype(BF16)
            x1_lat = _s5_out(z_lat, w_glu, s5_b_glu[j], x_lat, mods, ln_g[i, 0], ln_b[i, 0], False, nb)
            if ctx_out:
                x1_ctx = _s5_out(z_ctx, w_glu, s5_b_glu[j], x_ctx, mods, ln_g[i, 0], ln_b[i, 0], True, nb)
        else:
            w1 = cv_w_pw1[j].astype(BF16)
            w2 = cv_w_pw2[j].astype(BF16)
            conv_args = (cv_w_dw[j], cv_b_dw[j], cv_ln_g[j], cv_ln_b[j], w2, cv_b_pw2[j])
            h_lat = _pw1(x_lat, mods, w1, cv_b_pw1[j], False, nb)
            x1_lat = _conv_post(h_lat, *conv_args, x_lat, mods, ln_g[i, 0], ln_b[i, 0],
                                rows, GRID_W, 2 * SUBLANES, False, nb)
            if ctx_out:
                h_ctx = _pw1(x_ctx, mods, w1, cv_b_pw1[j], True, nb)
                x1_ctx = _conv_post(h_ctx, *conv_args, x_ctx, mods, ln_g[i, 0], ln_b[i, 0],
                                    n_ctx, 1, 1, True, nb)

        w_router_t = moe_w_router[i].T
        w_in = moe_w_in[i].astype(BF16)
        w_out = moe_w_out[i].astype(BF16)
        x_lat = _moe_post(x1_lat, mods, w_router_t, w_in, w_out, ln_g[i, 1], ln_b[i, 1], False, nb)
        if ctx_out:
            x_ctx = _moe_post(x1_ctx, mods, w_router_t, w_in, w_out, ln_g[i, 1], ln_b[i, 1], True, nb)
    return x_lat
```

```python
import functools

import jax
import jax.numpy as jnp
from jax import lax
from jax.experimental import pallas as pl
from jax.experimental.pallas import tpu as pltpu

F32 = jnp.float32
BF16 = jnp.bfloat16

DEPTH = 4
N_MIXERS = 2
GRID_W = 64
S5_H = 16
S5_P = 64
N_EXPERTS = 16
EC_CAPACITY_FACTOR = 2
DEEPNORM_ALPHA = (2.0 * DEPTH) ** 0.25
LN_EPS = 1e-5

LANES = 128
SUBLANES = 8
S5_CHUNK = 8
S5_GPT = LANES // S5_H
S5_STATE_COLS = S5_GPT * S5_P
MOD_ROWS = 8
VMEM_LIMIT = 56 * 1024 * 1024
ROW_TILE = 512
COMBINE_COLS = 1024
FF_TILE = 256
CONV_WT = 2 * SUBLANES


def _params(sem, vmem=VMEM_LIMIT):
    return pltpu.CompilerParams(dimension_semantics=sem, vmem_limit_bytes=vmem)


def _layer_norm(v, g, b):
    mu = jnp.mean(v, axis=-1, keepdims=True)
    c = v - mu
    var = jnp.mean(c * c, axis=-1, keepdims=True)
    return c * lax.rsqrt(var + LN_EPS) * g + b


def _split_bf16(v):
    hi = v.astype(BF16)
    lo = (v - hi.astype(F32)).astype(BF16)
    return hi, lo


def _mod_spec(layer, d, k):
    return pl.BlockSpec((1, MOD_ROWS, d), lambda *_: (layer, 0, k))


def _vec_spec(d):
    return pl.BlockSpec((1, d), lambda *_: (0, 0))


def _mod_row(is_ctx, nb):
    return nb if is_ctx else pl.program_id(0)


def _ada_kernel(c_ref, w_ref, b_ref, o_ref):
    c = c_ref[...]
    cond = c * jax.nn.sigmoid(c)
    hi, lo = _split_bf16(cond)
    lhs = jnp.concatenate([hi, lo], axis=0)
    r = jnp.dot(lhs, w_ref[0].astype(BF16), preferred_element_type=F32)
    o_ref[0] = r[:MOD_ROWS] + r[MOD_ROWS:] + b_ref[0]


def _ada_all(c8, ada_w, ada_b):
    depth, d, n = ada_w.shape
    tn = 1024
    return pl.pallas_call(
        _ada_kernel,
        grid=(depth, n // tn),
        in_specs=[
            pl.BlockSpec((MOD_ROWS, d), lambda i, k: (0, 0)),
            pl.BlockSpec((1, d, tn), lambda i, k: (i, 0, k)),
            pl.BlockSpec((1, 1, tn), lambda i, k: (i, 0, k)),
        ],
        out_specs=pl.BlockSpec((1, MOD_ROWS, tn), lambda i, k: (i, 0, k)),
        out_shape=jax.ShapeDtypeStruct((depth, MOD_ROWS, n), F32),
        compiler_params=_params(("arbitrary", "arbitrary")),
        name="adaln",
    )(c8, ada_w, ada_b.reshape(depth, 1, n))


def _cmul(ar, ai, br, bi):
    return ar * br - ai * bi, ar * bi + ai * br


def _s5_operators(a_re, a_im, log_dt, b_re, b_im, c_re, c_im):
    hp = lax.Precision.HIGHEST
    t = S5_CHUNK
    g = a_re.shape[1]
    nj = g // S5_GPT
    lam_r, lam_i = a_re.astype(F32), a_im.astype(F32)
    dt = jnp.exp(log_dt.astype(F32))[..., None]
    mag = jnp.exp(lam_r * dt)
    abar_r, abar_i = mag * jnp.cos(lam_i * dt), mag * jnp.sin(lam_i * dt)
    den = lam_r * lam_r + lam_i * lam_i
    xr, xi = abar_r - 1.0, abar_i
    coef_r, coef_i = (xr * lam_r + xi * lam_i) / den, (xi * lam_r - xr * lam_i) / den
    bb_r, bb_i = _cmul(coef_r[..., None], coef_i[..., None], b_re.astype(F32), b_im.astype(F32))
    cm_r, cm_i = c_re.astype(F32), c_im.astype(F32)
    pr, pi = [jnp.ones_like(abar_r)], [jnp.zeros_like(abar_r)]
    for _ in range(t):
        nr, ni = _cmul(pr[-1], pi[-1], abar_r, abar_i)
        pr.append(nr)
        pi.append(ni)
    pr, pi = jnp.stack(pr), jnp.stack(pi)

    def lag_kernels(d):
        qr, qi = _cmul(pr[:t, d, :, :, None], pi[:t, d, :, :, None], bb_r[d][None], bb_i[d][None])
        return (jnp.einsum('ghp,kgpj->kghj', cm_r[d], qr, precision=hp)
                - jnp.einsum('ghp,kgpj->kghj', cm_i[d], qi, precision=hp))

    kf, kb = lag_kernels(0), lag_kernels(1)
    kall = jnp.concatenate([kb[:0:-1], (kf[0] + kb[0])[None], kf[1:]], axis=0)
    s_idx = jnp.arange(t)[:, None]
    t_idx = jnp.arange(t)[None, :]
    mg = kall[t_idx - s_idx + (t - 1)]
    cm = mg.reshape(t, t, nj, S5_GPT, S5_H, S5_H).transpose(2, 0, 3, 5, 1, 4).reshape(nj, t * LANES, t * S5_H)

    exps_b = (jnp.arange(t - 1, -1, -1), jnp.arange(t))
    parts = []
    for d in range(2):
        fr, fi = _cmul(pr[exps_b[d], d][..., None], pi[exps_b[d], d][..., None], bb_r[d][None], bb_i[d][None])
        parts.append(jnp.stack([fr, fi]))
    bparts = jnp.stack(parts)
    cb = bparts.reshape(2, 2, t, nj, S5_GPT, S5_P, S5_H).transpose(3, 2, 4, 6, 0, 1, 5)
    cb = cb.reshape(nj, t * LANES, 2 * 2 * S5_P)

    exps_c = (jnp.arange(1, t + 1), jnp.arange(t, 0, -1))
    parts = []
    for d in range(2):
        qr, qi = _cmul(cm_r[d][None], cm_i[d][None], pr[exps_c[d], d][:, :, None, :], pi[exps_c[d], d][:, :, None, :])
        parts.append(jnp.stack([qr, -qi]))
    cparts = jnp.stack(parts)
    cc = cparts.reshape(2, 2, t, nj, S5_GPT, S5_H, S5_P).transpose(3, 0, 1, 4, 6, 2, 5)
    cc = cc.reshape(nj, 2 * 2 * S5_STATE_COLS, t * S5_H)

    rows = jnp.stack([pr[t, 0], pi[t, 0], pr[t, 1], pi[t, 1]])
    rows = rows.reshape(4, nj, S5_STATE_COLS).transpose(1, 0, 2)
    at = jnp.concatenate([rows, jnp.zeros_like(rows)], axis=1)
    return cm, cb, cc, at


def _iota(shape, axis):
    return lax.broadcasted_iota(jnp.int32, shape, axis)


def _expand_blockdiag(comp, src_of_col, row_grp, col_grp):
    k = comp.shape[1]
    n = src_of_col.shape[1]
    rep = jnp.where(_iota((k, n), 0) == src_of_col, 1.0, 0.0).astype(BF16)
    full = jnp.dot(comp.astype(BF16), rep, preferred_element_type=F32)
    return jnp.where(row_grp == col_grp, full, 0.0).astype(BF16)


def _s5_kernel(x_ref, xc_ref, sh_ref, sc_ref, d_ref, cm_ref, cb_ref, cc_ref, at_ref, *rest,
               nb_half, n_lat, n_ctx, ctx_out):
    if ctx_out:
        z_ref, zc_ref, wm_s, wb_s, wc_s, xf_s, st_s, y_s, zb_s, zcb_s = rest
    else:
        z_ref, wm_s, wb_s, wc_s, xf_s, st_s, y_s, zb_s = rest
        zc_ref = zcb_s = None
    t = S5_CHUNK
    ncl = n_lat // t
    ncc = n_ctx // t
    lat_rows = nb_half * ncl
    half = pl.program_id(1)
    sc_cols = S5_STATE_COLS
    nst = sc_cols // LANES
    tl = t * LANES

    @pl.when(half == 0)
    def _():
        io_col = _iota((1, tl), 1)
        io_grp = (io_col >> 4) & (S5_GPT - 1)
        io_src = (io_col >> 7) * S5_H + (io_col & (S5_H - 1))
        st_col = _iota((1, 4 * sc_cols), 1)
        st_grp = (st_col >> 6) & (S5_GPT - 1)
        st_src = (st_col >> 9) * S5_P + (st_col & (S5_P - 1))
        io_row_grp = (_iota((tl, 1), 0) >> 4) & (S5_GPT - 1)
        st_row_grp = (_iota((4 * sc_cols, 1), 0) >> 6) & (S5_GPT - 1)
        wm_s[...] = _expand_blockdiag(cm_ref[0], io_src, io_row_grp, io_grp)
        wb_s[...] = _expand_blockdiag(cb_ref[0], st_src, io_row_grp, st_grp)
        wc_s[...] = _expand_blockdiag(cc_ref[0], io_src, st_row_grp, io_grp)

    for k in range(nb_half):
        b = half * nb_half + k
        scale = 1.0 + sc_ref[0, pl.ds(b, 1), :]
        shift = sh_ref[0, pl.ds(b, 1), :]
        for s in range(t):
            xf_s[k * ncl:(k + 1) * ncl, s * LANES:(s + 1) * LANES] = (
                x_ref[pl.ds(k * n_lat + s, ncl, stride=t), :] * scale + shift)
    scale_c = 1.0 + sc_ref[0, nb_half * 2:nb_half * 2 + 1, :]
    shift_c = sh_ref[0, nb_half * 2:nb_half * 2 + 1, :]
    for k in range(nb_half):
        for s in range(t):
            r0 = lat_rows + k * ncc
            xf_s[r0:r0 + ncc, s * LANES:(s + 1) * LANES] = (
                xc_ref[pl.ds(k * n_ctx + s, ncc, stride=t), :] * scale_c + shift_c)

    xb = xf_s[...].astype(BF16)
    dvec = jnp.concatenate([d_ref[...]] * t, axis=1)
    y_s[...] = jnp.dot(xb, wm_s[...], preferred_element_type=F32) + xf_s[...] * dvec

    for d in range(2):
        local = jnp.dot(xb, wb_s[:, d * 2 * sc_cols:(d + 1) * 2 * sc_cols], preferred_element_type=F32)
        for q in range(2 * nst):
            st_s[d * 2 * nst + q] = local[:, q * LANES:(q + 1) * LANES]
    ar = [[at_ref[0, 2 * d:2 * d + 1, q * LANES:(q + 1) * LANES] for q in range(nst)] for d in range(2)]
    ai = [[at_ref[0, 2 * d + 1:2 * d + 2, q * LANES:(q + 1) * LANES] for q in range(nst)] for d in range(2)]

    def step(rows, h):
        new = [None] * (4 * nst)
        for d in range(2):
            base = d * 2 * nst
            for q in range(nst):
                hr, hi = h[base + q], h[base + nst + q]
                sr = st_s[base + q, rows[d], :]
                si = st_s[base + nst + q, rows[d], :]
                st_s[base + q, rows[d], :] = hr
                st_s[base + nst + q, rows[d], :] = hi
                new[base + q] = ar[d][q] * hr - ai[d][q] * hi + sr
                new[base + nst + q] = ar[d][q] * hi + ai[d][q] * hr + si
        return tuple(new)

    def ctx_rows(i):
        return (pl.ds(lat_rows + i, nb_half, stride=ncc), pl.ds(lat_rows + ncc - 1 - i, nb_half, stride=ncc))

    def lat_rows_at(i):
        return (pl.ds(i, nb_half, stride=ncl), pl.ds(ncl - 1 - i, nb_half, stride=ncl))

    zero = tuple(jnp.zeros((nb_half, LANES), F32) for _ in range(4 * nst))
    h = lax.fori_loop(0, ncc, lambda i, h: step(ctx_rows(i), h), zero, unroll=2)
    lax.fori_loop(0, ncl, lambda i, h: step(lat_rows_at(i), h), h, unroll=2)

    entering = jnp.concatenate([st_s[q] for q in range(4 * nst)], axis=1)
    y_s[...] += jnp.dot(entering.astype(BF16), wc_s[...], preferred_element_type=F32)

    z = jax.nn.gelu(y_s[...])
    for k in range(nb_half):
        for s in range(t):
            zb_s[pl.ds(k * n_lat + s, ncl, stride=t), :] = z[k * ncl:(k + 1) * ncl, s * LANES:(s + 1) * LANES]
    z_ref[...] = zb_s[...].astype(BF16)
    if ctx_out:
        for k in range(nb_half):
            for s in range(t):
                r0 = lat_rows + k * ncc
                zcb_s[pl.ds(k * n_ctx + s, ncc, stride=t), :] = z[r0:r0 + ncc, s * LANES:(s + 1) * LANES]
        zc_ref[...] = zcb_s[...].astype(BF16)


def _s5_core(x, xc, mods_all, layer, d_skip, ops, ctx_out):
    nb, n_lat, d = x.shape
    n_ctx = xc.shape[1]
    cm, cb, cc, at = ops
    nj = d // LANES
    nb_half = nb // 2
    t = S5_CHUNK
    tl = t * LANES
    rows = nb_half * (n_lat + n_ctx) // t
    kern = functools.partial(_s5_kernel, nb_half=nb_half, n_lat=n_lat, n_ctx=n_ctx, ctx_out=ctx_out)
    out_shape = [jax.ShapeDtypeStruct((nb * n_lat, d), BF16)]
    out_specs = [pl.BlockSpec((nb_half * n_lat, LANES), lambda j, h: (h, j))]
    scratch = [pltpu.VMEM((tl, tl), BF16), pltpu.VMEM((tl, 4 * S5_STATE_COLS), BF16),
               pltpu.VMEM((4 * S5_STATE_COLS, tl), BF16),
               pltpu.VMEM((rows, tl), F32), pltpu.VMEM((4 * S5_STATE_COLS // LANES, rows, LANES), F32),
               pltpu.VMEM((rows, tl), F32), pltpu.VMEM((nb_half * n_lat, LANES), F32)]
    if ctx_out:
        out_shape.append(jax.ShapeDtypeStruct((nb * n_ctx, d), BF16))
        out_specs.append(pl.BlockSpec((nb_half * n_ctx, LANES), lambda j, h: (h, j)))
        scratch.append(pltpu.VMEM((nb_half * n_ctx, LANES), F32))
    res = pl.pallas_call(
        kern,
        grid=(nj, 2),
        in_specs=[
            pl.BlockSpec((nb_half * n_lat, LANES), lambda j, h: (h, j)),
            pl.BlockSpec((nb_half * n_ctx, LANES), lambda j, h: (h, j)),
            pl.BlockSpec((1, MOD_ROWS, LANES), lambda j, h: (layer, 0, j)),
            pl.BlockSpec((1, MOD_ROWS, LANES), lambda j, h: (layer, 0, nj + j)),
            pl.BlockSpec((1, LANES), lambda j, h: (0, j)),
            pl.BlockSpec((1,) + cm.shape[1:], lambda j, h: (j, 0, 0)),
            pl.BlockSpec((1,) + cb.shape[1:], lambda j, h: (j, 0, 0)),
            pl.BlockSpec((1,) + cc.shape[1:], lambda j, h: (j, 0, 0)),
            pl.BlockSpec((1,) + at.shape[1:], lambda j, h: (j, 0, 0)),
        ],
        out_specs=out_specs,
        out_shape=out_shape,
        scratch_shapes=scratch,
        compiler_params=_params(("arbitrary", "arbitrary")),
        name="s5_core",
    )(x.reshape(nb * n_lat, d), xc.reshape(nb * n_ctx, d), mods_all, mods_all, d_skip.reshape(1, d), cm, cb, cc, at)
    z = res[0].reshape(nb, n_lat, d)
    zc = res[1].reshape(nb, n_ctx, d) if ctx_out else None
    return z, zc


def _s5_out_kernel(z_ref, w_ref, b_ref, x_ref, g_ref, lg_ref, lb_ref, o_ref, *, is_ctx, nb):
    d = x_ref.shape[-1]
    acc = jnp.dot(z_ref[0], w_ref[0], preferred_element_type=F32) + b_ref[...]
    y = acc[:, :d] * jax.nn.sigmoid(acc[:, d:])
    gate = g_ref[0, pl.ds(_mod_row(is_ctx, nb), 1), :]
    o_ref[0] = _layer_norm(DEEPNORM_ALPHA * x_ref[0] + gate * y, lg_ref[...], lb_ref[...])


def _s5_out(z, w_bf, j, b_glu, x, mods_all, layer, ln_g, ln_b, is_ctx, nb):
    n, t, d = x.shape
    tm = min(t, ROW_TILE)
    kern = functools.partial(_s5_out_kernel, is_ctx=is_ctx, nb=nb)
    tok = pl.BlockSpec((1, tm, d), lambda b, i: (b, i, 0))
    return pl.pallas_call(
        kern,
        grid=(n, t // tm),
        in_specs=[
            tok,
            pl.BlockSpec((1, d, 2 * d), lambda b, i: (j, 0, 0), pipeline_mode=pl.Buffered(1)),
            _vec_spec(2 * d),
            tok,
            _mod_spec(layer, d, 2),
            _vec_spec(d), _vec_spec(d),
        ],
        out_specs=tok,
        out_shape=jax.ShapeDtypeStruct((n, t, d), F32),
        compiler_params=_params(("arbitrary", "arbitrary")),
        name="s5_out",
    )(z, w_bf, b_glu.reshape(1, 2 * d), x, mods_all, ln_g.reshape(1, d), ln_b.reshape(1, d))


def _pw1_kernel(x_ref, sh_ref, sc_ref, w_ref, b_ref, o_ref, *, is_ctx, nb):
    d = x_ref.shape[-1]
    row = _mod_row(is_ctx, nb)
    u = x_ref[0] * (1.0 + sc_ref[0, pl.ds(row, 1), :]) + sh_ref[0, pl.ds(row, 1), :]
    acc = jnp.dot(u.astype(BF16), w_ref[0], preferred_element_type=F32) + b_ref[...]
    o_ref[0] = acc[:, :d] * jax.nn.sigmoid(acc[:, d:])


def _pw1(x, mods_all, layer, w_bf, j, b_pw1, is_ctx, nb):
    n, t, d = x.shape
    tm = min(t, ROW_TILE)
    kern = functools.partial(_pw1_kernel, is_ctx=is_ctx, nb=nb)
    tok = pl.BlockSpec((1, tm, d), lambda b, i: (b, i, 0))
    return pl.pallas_call(
        kern,
        grid=(n, t // tm),
        in_specs=[
            tok,
            _mod_spec(layer, d, 0), _mod_spec(layer, d, 1),
            pl.BlockSpec((1, d, 2 * d), lambda b, i: (j, 0, 0), pipeline_mode=pl.Buffered(1)),
            _vec_spec(2 * d),
        ],
        out_specs=tok,
        out_shape=jax.ShapeDtypeStruct((n, t, d), F32),
        compiler_params=_params(("arbitrary", "arbitrary")),
        name="conv_pw1",
    )(x, mods_all, mods_all, w_bf, b_pw1.reshape(1, 2 * d))


def _conv_tail(cv, cg_ref, cb_ref, w2_ref, b2_ref, x, gate, lg_ref, lb_ref):
    hn = _layer_norm(cv, cg_ref[...], cb_ref[...])
    hn = hn * jax.nn.sigmoid(hn)
    y = jnp.dot(hn.astype(BF16), w2_ref[0], preferred_element_type=F32) + b2_ref[...]
    return _layer_norm(DEEPNORM_ALPHA * x + gate * y, lg_ref[...], lb_ref[...])


def _conv_lat_kernel(h_ref, wdw_ref, bdw_ref, cg_ref, cb_ref, w2_ref, b2_ref, x_ref, g_ref, lg_ref, lb_ref,
                     o_ref, hp_s, wb_s, cv_s, *, n_rows, wt):
    kw = wdw_ref.shape[0]
    pad = kw // 2
    d = h_ref.shape[-1]
    nsub = wt // SUBLANES
    zeros = jnp.zeros((pad, wt, d), F32)
    hp_s[0:pad] = zeros
    hp_s[pad + n_rows:pad + n_rows + pad] = zeros
    hp_s[pad:pad + n_rows] = h_ref[0]
    for k in range(kw):
        wb_s[k] = jnp.broadcast_to(wdw_ref[k:k + 1, :], (SUBLANES, d))
    bias = bdw_ref[...]

    def conv_row(r, carry):
        accs = [jnp.zeros((SUBLANES, d), F32) for _ in range(nsub)]
        for k in range(kw):
            w8 = wb_s[k]
            for q in range(nsub):
                accs[q] = accs[q] + w8 * hp_s[r + k, q * SUBLANES:(q + 1) * SUBLANES, :]
        for q in range(nsub):
            cv_s[pl.ds(pl.multiple_of(r * wt + q * SUBLANES, SUBLANES), SUBLANES), :] = accs[q] + bias
        return carry

    lax.fori_loop(0, n_rows, conv_row, 0)
    gate = g_ref[0, pl.ds(pl.program_id(0), 1), :]
    x = x_ref[0].reshape(n_rows * wt, d)
    out = _conv_tail(cv_s[...], cg_ref, cb_ref, w2_ref, b2_ref, x, gate, lg_ref, lb_ref)
    o_ref[0] = out.reshape(n_rows, wt, d)


def _conv_ctx_kernel(h_ref, wdw_ref, bdw_ref, cg_ref, cb_ref, w2_ref, b2_ref, x_ref, g_ref, lg_ref, lb_ref,
                     o_ref, hp_s, cv_s, *, n_tok, nb):
    kw = wdw_ref.shape[0]
    pad = kw // 2
    d = h_ref.shape[-1]
    lead = 2 * SUBLANES
    rblk = 128
    cblk = 2 * LANES
    hp_s[0:lead] = jnp.zeros((lead, d), F32)
    hp_s[lead + n_tok:lead + n_tok + lead] = jnp.zeros((lead, d), F32)
    hp_s[lead:lead + n_tok] = h_ref[0]
    span = ((kw - 1 + lead - pad) // SUBLANES) * SUBLANES

    def col_block(lc, carry):
        cols = pl.ds(pl.multiple_of(lc * cblk, cblk), cblk)
        for rc in range(n_tok // rblk):
            acc = jnp.zeros((rblk, cblk), F32)
            for q in range(SUBLANES):
                taps = [k for k in range(kw) if (k + lead - pad) % SUBLANES == q]
                if not taps:
                    continue
                shifted = hp_s[pl.ds(rc * rblk + q, rblk + span), cols]
                for k in taps:
                    o = k + lead - pad - q
                    acc = acc + wdw_ref[k:k + 1, cols] * shifted[o:o + rblk]
            cv_s[rc * rblk:(rc + 1) * rblk, cols] = acc + bdw_ref[:, cols]
        return carry

    lax.fori_loop(0, d // cblk, col_block, 0)
    gate = g_ref[0, nb:nb + 1, :]
    o_ref[0] = _conv_tail(cv_s[...], cg_ref, cb_ref, w2_ref, b2_ref, x_ref[0], gate, lg_ref, lb_ref)


def _conv_post(h, w_dw, b_dw, cv_g, cv_b, w2_bf, j, b2, x, mods_all, layer, ln_g, ln_b, n_rows, is_ctx, nb):
    n, t, d = x.shape
    kw = w_dw.shape[0]
    weights = [pl.BlockSpec((kw, d), lambda b, i: (0, 0)), _vec_spec(d), _vec_spec(d), _vec_spec(d),
               pl.BlockSpec((1, d, d), lambda b, i: (j, 0, 0), pipeline_mode=pl.Buffered(1)), _vec_spec(d)]
    tail = [_mod_spec(layer, d, 2), _vec_spec(d), _vec_spec(d)]
    args_w = (w_dw, b_dw.reshape(1, d), cv_g.reshape(1, d), cv_b.reshape(1, d), w2_bf, b2.reshape(1, d))
    args_t = (mods_all, ln_g.reshape(1, d), ln_b.reshape(1, d))
    if is_ctx:
        blk = pl.BlockSpec((1, t, d), lambda b, i: (b, 0, 0))
        return pl.pallas_call(
            functools.partial(_conv_ctx_kernel, n_tok=t, nb=nb),
            grid=(n, 1),
            in_specs=[blk] + weights + [blk] + tail,
            out_specs=blk,
            out_shape=jax.ShapeDtypeStruct((n, t, d), F32),
            scratch_shapes=[pltpu.VMEM((t + 4 * SUBLANES, d), F32), pltpu.VMEM((t, d), F32)],
            compiler_params=_params(("arbitrary", "arbitrary")),
            name="conv_post_ctx",
        )(h, *args_w, x, *args_t)
    wt = CONV_WT
    width = t // n_rows
    blk = pl.BlockSpec((1, n_rows, wt, d), lambda b, i: (b, 0, i, 0))
    out = pl.pallas_call(
        functools.partial(_conv_lat_kernel, n_rows=n_rows, wt=wt),
        grid=(n, width // wt),
        in_specs=[blk] + weights + [blk] + tail,
        out_specs=blk,
        out_shape=jax.ShapeDtypeStruct((n, n_rows, width, d), F32),
        scratch_shapes=[pltpu.VMEM((n_rows + 2 * (kw // 2), wt, d), F32),
                        pltpu.VMEM((kw, SUBLANES, d), F32),
                        pltpu.VMEM((n_rows * wt, d), F32)],
        compiler_params=_params(("arbitrary", "arbitrary")),
        name="conv_post",
    )(h.reshape(n, n_rows, width, d), *args_w, x.reshape(n, n_rows, width, d), *args_t)
    return out.reshape(n, t, d)


def _router_kernel(x_ref, sh_ref, sc_ref, wr_ref, u_ref, a_ref, *, is_ctx, nb):
    row = _mod_row(is_ctx, nb)
    u = x_ref[0] * (1.0 + sc_ref[0, pl.ds(row, 1), :]) + sh_ref[0, pl.ds(row, 1), :]
    u_ref[0] = u.astype(BF16)
    uh, ul = _split_bf16(u)
    wh, wl = _split_bf16(wr_ref[0])
    nt = (((1,), (1,)), ((), ()))
    logits = (lax.dot_general(wh, uh, nt, preferred_element_type=F32)
              + lax.dot_general(wh, ul, nt, preferred_element_type=F32)
              + lax.dot_general(wl, uh, nt, preferred_element_type=F32))
    m = jnp.max(logits, axis=0, keepdims=True)
    ex = jnp.exp(logits - m)
    a_ref[0] = ex / jnp.sum(ex, axis=0, keepdims=True)


def _router(x, mods_all, layer, w_router_t, is_ctx, nb):
    n, t, d = x.shape
    e = w_router_t.shape[1]
    tm = min(t, ROW_TILE)
    kern = functools.partial(_router_kernel, is_ctx=is_ctx, nb=nb)
    tok = pl.BlockSpec((1, tm, d), lambda b, i: (b, i, 0))
    return pl.pallas_call(
        kern,
        grid=(n, t // tm),
        in_specs=[tok, _mod_spec(layer, d, 3), _mod_spec(layer, d, 4),
                  pl.BlockSpec((1, e, d), lambda b, i: (layer, 0, 0))],
        out_specs=[tok, pl.BlockSpec((1, e, tm), lambda b, i: (b, 0, i))],
        out_shape=[jax.ShapeDtypeStruct((n, t, d), BF16), jax.ShapeDtypeStruct((n, e, t), F32)],
        compiler_params=_params(("arbitrary", "arbitrary")),
        name="router",
    )(x, mods_all, mods_all, w_router_t)


def _topk_kernel(a_ref, p_ref, pt_ref, *, cap):
    a = a_ref[0]
    e, t = a.shape
    capf = jnp.float32(cap)

    def count(mask):
        return jnp.sum(jnp.where(mask, 1.0, 0.0), axis=1, keepdims=True)

    def as_row_values(bits):
        return jnp.concatenate([pltpu.bitcast(bits, F32)] * (t // LANES), axis=1)

    thr_bits = jnp.zeros((e, LANES), jnp.int32)
    for bit in range(30, -1, -1):
        cand = thr_bits | jnp.int32(1 << bit)
        keep = count(a >= as_row_values(cand)) >= capf
        thr_bits = jnp.where(keep, cand, thr_bits)
    thr = as_row_values(thr_bits)
    gt = a > thr
    eq = a == thr
    need = capf - count(gt)
    tri = jnp.where(_iota((t, t), 0) <= _iota((t, t), 1), 1.0, 0.0).astype(BF16)
    eq_f = jnp.where(eq, 1.0, 0.0)
    eq_rank = jnp.dot(eq_f.astype(BF16), tri, preferred_element_type=F32) - eq_f
    sel = jnp.where(gt, 1.0, jnp.where(eq & (eq_rank < need), 1.0, 0.0))
    slot = jnp.dot(sel.astype(BF16), tri, preferred_element_type=F32) - 1.0
    pos = jnp.where(sel > 0.0, slot, -1.0)
    p_ref[0] = pos.astype(jnp.int32)
    padded = jnp.concatenate([pos, jnp.full((LANES - e, t), -1.0, F32)], axis=0)
    pt_ref[0] = padded.T.astype(jnp.int32)


def _topk(aff_t, cap):
    n, e, t = aff_t.shape
    kern = functools.partial(_topk_kernel, cap=cap)
    return pl.pallas_call(
        kern,
        grid=(n,),
        in_specs=[pl.BlockSpec((1, e, t), lambda b: (b, 0, 0))],
        out_specs=[pl.BlockSpec((1, e, t), lambda b: (b, 0, 0)),
                   pl.BlockSpec((1, t, LANES), lambda b: (b, 0, 0))],
        out_shape=[jax.ShapeDtypeStruct((n, e, t), jnp.int32), jax.ShapeDtypeStruct((n, t, LANES), jnp.int32)],
        compiler_params=_params(("arbitrary",)),
        name="topk",
    )(aff_t)


def _gather_kernel(u_ref, p_ref, a_ref, *rest, cap):
    xs_ref, gs_ref = rest[-2:]
    e = pl.program_id(1)
    pos = p_ref[0, pl.ds(e, 1), :]
    t = pos.shape[1]
    hit = _iota((cap, t), 0) == pos
    onehot = jnp.where(hit, 1.0, 0.0).astype(BF16)
    xs_ref[0] = jnp.dot(onehot, u_ref[0], preferred_element_type=F32).astype(BF16)
    gate = jnp.sum(jnp.where(hit, a_ref[0, pl.ds(e, 1), :], 0.0), axis=1, keepdims=True)
    gs_ref[0] = jnp.broadcast_to(gate, (cap, LANES))


def _gather(u2, pos_t, aff_t, cap, total_rows, row0, carry):
    n, t, d = u2.shape
    e = pos_t.shape[1]
    blk0 = row0 // cap
    in_specs = [pl.BlockSpec((1, t, d), lambda b, ei: (b, 0, 0)),
                pl.BlockSpec((1, e, t), lambda b, ei: (b, 0, 0)),
                pl.BlockSpec((1, e, t), lambda b, ei: (b, 0, 0))]
    args = [u2, pos_t, aff_t]
    aliases = {}
    if carry is not None:
        in_specs += [pl.BlockSpec(memory_space=pl.ANY), pl.BlockSpec(memory_space=pl.ANY)]
        args += list(carry)
        aliases = {3: 0, 4: 1}
    return pl.pallas_call(
        functools.partial(_gather_kernel, cap=cap),
        grid=(n, e),
        in_specs=in_specs,
        out_specs=[pl.BlockSpec((1, cap, d), lambda b, ei: (ei, blk0 + b, 0)),
                   pl.BlockSpec((1, cap, LANES), lambda b, ei: (ei, blk0 + b, 0))],
        out_shape=[jax.ShapeDtypeStruct((e, total_rows, d), BF16),
                   jax.ShapeDtypeStruct((e, total_rows, LANES), F32)],
        input_output_aliases=aliases,
        compiler_params=_params(("arbitrary", "arbitrary")),
        name="gather",
    )(*args)


def _ffn_kernel(xs_ref, gs_ref, wg_ref, wu_ref, wo_ref, y_ref, act_s, wo_s):
    fc = pl.program_id(1)
    nfc = act_s.shape[0]
    x = xs_ref[0]
    g = jnp.dot(x, wg_ref[0, 0].astype(BF16), preferred_element_type=F32)
    up = jnp.dot(x, wu_ref[0, 0].astype(BF16), preferred_element_type=F32)
    act_s[fc] = ((g * jax.nn.sigmoid(g)) * up).astype(BF16)
    wo_s[fc] = wo_ref[0, 0].astype(BF16)

    @pl.when(fc == nfc - 1)
    def _():
        act = jnp.concatenate([act_s[k] for k in range(nfc)], axis=1)
        w_out = wo_s[...].reshape(nfc * wo_s.shape[1], wo_s.shape[2])
        y = jnp.dot(act, w_out, preferred_element_type=F32)
        y_ref[0] = (y * gs_ref[0][:, 0:1]).astype(BF16)


def _ffn(xs, gs, w_in, w_out, layer):
    e, r, d = xs.shape
    ff = w_out.shape[2]
    nfc = ff // FF_TILE
    return pl.pallas_call(
        _ffn_kernel,
        grid=(e, nfc),
        in_specs=[
            pl.BlockSpec((1, r, d), lambda ei, fc: (ei, 0, 0), pipeline_mode=pl.Buffered(1)),
            pl.BlockSpec((1, r, LANES), lambda ei, fc: (ei, 0, 0)),
            pl.BlockSpec((1, 1, d, FF_TILE), lambda ei, fc: (layer, ei, 0, fc)),
            pl.BlockSpec((1, 1, d, FF_TILE), lambda ei, fc: (layer, ei, 0, nfc + fc)),
            pl.BlockSpec((1, 1, FF_TILE, d), lambda ei, fc: (layer, ei, fc, 0)),
        ],
        out_specs=pl.BlockSpec((1, r, d), lambda ei, fc: (ei, 0, 0)),
        out_shape=jax.ShapeDtypeStruct((e, r, d), BF16),
        scratch_shapes=[pltpu.VMEM((nfc, r, FF_TILE), BF16), pltpu.VMEM((nfc, FF_TILE, d), BF16)],
        compiler_params=_params(("arbitrary", "arbitrary")),
        name="expert_ffn",
    )(xs, gs, w_in, w_in, w_out)


def _combine_kernel(y_ref, pt_ref, x_ref, g_ref, lg_ref, lb_ref, o_ref, oh_s, *, is_ctx, nb, cap):
    n_exp, _, d = y_ref.shape
    ec = n_exp * cap
    cw = min(ec, COMBINE_COLS)
    pos = pt_ref[0].astype(F32).astype(BF16)
    for c0 in range(0, ec, cw):
        src = _iota((LANES, cw), 0)
        dst = _iota((LANES, cw), 1) + c0
        expand = jnp.where((dst >= src * cap) & (dst < (src + 1) * cap), 1.0, 0.0).astype(BF16)
        pe = jnp.dot(pos, expand, preferred_element_type=F32)
        slot = (_iota((1, cw), 1) + c0) % cap
        oh_s[:, c0:c0 + cw] = jnp.where(pe == slot.astype(F32), 1.0, 0.0).astype(BF16)
    f = jnp.dot(oh_s[...], y_ref[...].reshape(ec, d), preferred_element_type=F32)
    gate = g_ref[0, pl.ds(_mod_row(is_ctx, nb), 1), :]
    o_ref[0] = _layer_norm(DEEPNORM_ALPHA * x_ref[0] + gate * f, lg_ref[...], lb_ref[...])


def _combine(y, pos_tm, x, mods_all, layer, ln_g, ln_b, cap, row0, is_ctx, nb):
    n, t, d = x.shape
    e = y.shape[0]
    tm = min(t, ROW_TILE)
    blk0 = row0 // cap
    kern = functools.partial(_combine_kernel, is_ctx=is_ctx, nb=nb, cap=cap)
    tok = pl.BlockSpec((1, tm, d), lambda b, i: (b, i, 0))
    return pl.pallas_call(
        kern,
        grid=(n, t // tm),
        in_specs=[
            pl.BlockSpec((e, cap, d), lambda b, i: (0, blk0 + b, 0), pipeline_mode=pl.Buffered(1)),
            pl.BlockSpec((1, tm, LANES), lambda b, i: (b, i, 0)),
            tok,
            _mod_spec(layer, d, 5),
            _vec_spec(d), _vec_spec(d),
        ],
        out_specs=tok,
        out_shape=jax.ShapeDtypeStruct((n, t, d), F32),
        scratch_shapes=[pltpu.VMEM((tm, e * cap), BF16)],
        compiler_params=_params(("arbitrary", "arbitrary")),
        name="combine",
    )(y, pos_tm, x, mods_all, ln_g.reshape(1, d), ln_b.reshape(1, d))


def _moe_post(x1_lat, x1_ctx, mods_all, layer, w_router_t, w_in, w_out, ln_g, ln_b, nb):
    streams = [(x1_lat, False)] + ([(x1_ctx, True)] if x1_ctx is not None else [])
    caps = [EC_CAPACITY_FACTOR * s.shape[1] // N_EXPERTS for s, _ in streams]
    total_rows = sum(nb * c for c in caps)
    routed, carry, row0 = [], None, 0
    for (s, is_ctx), cap in zip(streams, caps):
        u2, aff_t = _router(s, mods_all, layer, w_router_t, is_ctx, nb)
        pos_t, pos_tm = _topk(aff_t, cap)
        carry = _gather(u2, pos_t, aff_t, cap, total_rows, row0, carry)
        routed.append((pos_tm, cap, row0))
        row0 += nb * cap
    y = _ffn(carry[0], carry[1], w_in, w_out, layer)
    outs = [_combine(y, pos_tm, s, mods_all, layer, ln_g, ln_b, cap, r0, is_ctx, nb)
            for (s, is_ctx), (pos_tm, cap, r0) in zip(streams, routed)]
    return outs[0], (outs[1] if len(outs) > 1 else None)


def kernel(x, c, ctx, c_ctx, ada_w, ada_b, ln_g, ln_b, s5_a_re, s5_a_im, s5_log_dt, s5_b_re, s5_b_im,
           s5_c_re, s5_c_im, s5_d, s5_w_glu, s5_b_glu, cv_w_pw1, cv_b_pw1, cv_w_dw, cv_b_dw, cv_ln_g,
           cv_ln_b, cv_w_pw2, cv_b_pw2, moe_w_router, moe_w_in, moe_w_out):
    nb, seq, d = x.shape
    assert nb + 1 <= MOD_ROWS and nb % 2 == 0 and d % LANES == 0
    rows = seq // GRID_W

    c8 = jnp.concatenate([c.astype(F32), c_ctx.astype(F32)[None], jnp.zeros((MOD_ROWS - nb - 1, d), F32)], axis=0)
    mods_all = _ada_all(c8, ada_w, ada_b)
    w_glu_bf = s5_w_glu.astype(BF16)
    w_pw1_bf = cv_w_pw1.astype(BF16)
    w_pw2_bf = cv_w_pw2.astype(BF16)
    w_router_t = jnp.swapaxes(moe_w_router, 1, 2)

    x_lat, x_ctx = x, ctx
    for i in range(DEPTH):
        is_s5 = (i % N_MIXERS) == 0
        j = i // N_MIXERS
        ctx_out = any((k % N_MIXERS) == 0 for k in range(i + 1, DEPTH))
        x1_ctx = None

        if is_s5:
            ops = _s5_operators(s5_a_re[j], s5_a_im[j], s5_log_dt[j], s5_b_re[j], s5_b_im[j],
                                s5_c_re[j], s5_c_im[j])
            z_lat, z_ctx = _s5_core(x_lat, x_ctx, mods_all, i, s5_d[j], ops, ctx_out)
            x1_lat = _s5_out(z_lat, w_glu_bf, j, s5_b_glu[j], x_lat, mods_all, i, ln_g[i, 0], ln_b[i, 0], False, nb)
            if ctx_out:
                x1_ctx = _s5_out(z_ctx, w_glu_bf, j, s5_b_glu[j], x_ctx, mods_all, i, ln_g[i, 0], ln_b[i, 0],
                                 True, nb)
        else:
            conv_args = (cv_w_dw[j], cv_b_dw[j], cv_ln_g[j], cv_ln_b[j], w_pw2_bf, j, cv_b_pw2[j])
            h_lat = _pw1(x_lat, mods_all, i, w_pw1_bf, j, cv_b_pw1[j], False, nb)
            x1_lat = _conv_post(h_lat, *conv_args, x_lat, mods_all, i, ln_g[i, 0], ln_b[i, 0], rows, False, nb)
            if ctx_out:
                h_ctx = _pw1(x_ctx, mods_all, i, w_pw1_bf, j, cv_b_pw1[j], True, nb)
                x1_ctx = _conv_post(h_ctx, *conv_args, x_ctx, mods_all, i, ln_g[i, 0], ln_b[i, 0], None, True, nb)

        x_lat, x_ctx_new = _moe_post(x1_lat, x1_ctx, mods_all, i, w_router_t, moe_w_in, moe_w_out,
                                     ln_g[i, 1], ln_b[i, 1], nb)
        if ctx_out:
            x_ctx = x_ctx_new
    return x_lat
```

```python
import functools

import jax
import jax.numpy as jnp
from jax import lax
from jax.experimental import pallas as pl
from jax.experimental.pallas import tpu as pltpu

F32 = jnp.float32
BF16 = jnp.bfloat16

DEPTH = 4
N_MIXERS = 2
GRID_W = 64
S5_H = 16
S5_P = 64
N_EXPERTS = 16
EC_CAPACITY_FACTOR = 2
DEEPNORM_ALPHA = (2.0 * DEPTH) ** 0.25
LN_EPS = 1e-5

LANES = 128
SUBLANES = 8
S5_CHUNK = 8
S5_GPT = LANES // S5_H
S5_STATE_COLS = S5_GPT * S5_P
MOD_ROWS = 8
VMEM_LIMIT = 56 * 1024 * 1024
ROW_TILE = 512
COMBINE_COLS = 1024
FF_TILE = 256
CONV_WT = 2 * SUBLANES


def _params(sem, vmem=VMEM_LIMIT):
    return pltpu.CompilerParams(dimension_semantics=sem, vmem_limit_bytes=vmem)


def _layer_norm(v, g, b):
    mu = jnp.mean(v, axis=-1, keepdims=True)
    c = v - mu
    var = jnp.mean(c * c, axis=-1, keepdims=True)
    return c * lax.rsqrt(var + LN_EPS) * g + b


def _split_bf16(v):
    hi = v.astype(BF16)
    lo = (v - hi.astype(F32)).astype(BF16)
    return hi, lo


def _mod_spec(layer, d, k):
    return pl.BlockSpec((1, MOD_ROWS, d), lambda *_: (layer, 0, k))


def _vec_spec(d):
    return pl.BlockSpec((1, d), lambda *_: (0, 0))


def _mod_row(is_ctx, nb):
    return nb if is_ctx else pl.program_id(0)


def _ada_kernel(c_ref, w_ref, b_ref, o_ref):
    c = c_ref[...]
    cond = c * jax.nn.sigmoid(c)
    hi, lo = _split_bf16(cond)
    lhs = jnp.concatenate([hi, lo], axis=0)
    r = jnp.dot(lhs, w_ref[0].astype(BF16), preferred_element_type=F32)
    o_ref[0] = r[:MOD_ROWS] + r[MOD_ROWS:] + b_ref[0]


def _ada_all(c8, ada_w, ada_b):
    depth, d, n = ada_w.shape
    tn = 1024
    return pl.pallas_call(
        _ada_kernel,
        grid=(depth, n // tn),
        in_specs=[
            pl.BlockSpec((MOD_ROWS, d), lambda i, k: (0, 0)),
            pl.BlockSpec((1, d, tn), lambda i, k: (i, 0, k)),
            pl.BlockSpec((1, 1, tn), lambda i, k: (i, 0, k)),
        ],
        out_specs=pl.BlockSpec((1, MOD_ROWS, tn), lambda i, k: (i, 0, k)),
        out_shape=jax.ShapeDtypeStruct((depth, MOD_ROWS, n), F32),
        compiler_params=_params(("arbitrary", "arbitrary")),
        name="adaln",
    )(c8, ada_w, ada_b.reshape(depth, 1, n))


def _cmul(ar, ai, br, bi):
    return ar * br - ai * bi, ar * bi + ai * br


def _s5_operators(a_re, a_im, log_dt, b_re, b_im, c_re, c_im):
    hp = lax.Precision.HIGHEST
    t = S5_CHUNK
    g = a_re.shape[1]
    nj = g // S5_GPT
    lam_r, lam_i = a_re.astype(F32), a_im.astype(F32)
    dt = jnp.exp(log_dt.astype(F32))[..., None]
    mag = jnp.exp(lam_r * dt)
    abar_r, abar_i = mag * jnp.cos(lam_i * dt), mag * jnp.sin(lam_i * dt)
    den = lam_r * lam_r + lam_i * lam_i
    xr, xi = abar_r - 1.0, abar_i
    coef_r, coef_i = (xr * lam_r + xi * lam_i) / den, (xi * lam_r - xr * lam_i) / den
    bb_r, bb_i = _cmul(coef_r[..., None], coef_i[..., None], b_re.astype(F32), b_im.astype(F32))
    cm_r, cm_i = c_re.astype(F32), c_im.astype(F32)
    pr, pi = [jnp.ones_like(abar_r)], [jnp.zeros_like(abar_r)]
    for _ in range(t):
        nr, ni = _cmul(pr[-1], pi[-1], abar_r, abar_i)
        pr.append(nr)
        pi.append(ni)
    pr, pi = jnp.stack(pr), jnp.stack(pi)

    def lag_kernels(d):
        qr, qi = _cmul(pr[:t, d, :, :, None], pi[:t, d, :, :, None], bb_r[d][None], bb_i[d][None])
        return (jnp.einsum('ghp,kgpj->kghj', cm_r[d], qr, precision=hp)
                - jnp.einsum('ghp,kgpj->kghj', cm_i[d], qi, precision=hp))

    kf, kb = lag_kernels(0), lag_kernels(1)
    kall = jnp.concatenate([kb[:0:-1], (kf[0] + kb[0])[None], kf[1:]], axis=0)
    nlag = 2 * t - 1
    kl = kall.reshape(nlag, nj, S5_GPT, S5_H, S5_H).transpose(1, 0, 2, 4, 3).reshape(nj, nlag * LANES, S5_H)

    def per_tile(re, im, perm, rows, cols):
        both = jnp.stack([re, im], axis=1)
        both = both.reshape(2, 2, nj, S5_GPT, both.shape[-2], both.shape[-1])
        return both.transpose(perm).reshape(nj, rows, cols)

    bt = per_tile(bb_r, bb_i, (2, 0, 1, 3, 5, 4), 4 * LANES, S5_P)
    ct = per_tile(cm_r, cm_i, (2, 0, 1, 3, 5, 4), 4 * S5_STATE_COLS, S5_H)

    exps_b = (jnp.arange(t - 1, -1, -1), jnp.arange(t))
    exps_c = (jnp.arange(1, t + 1), jnp.arange(t, 0, -1))

    def powers(exps):
        both = jnp.stack([jnp.stack([pr[exps[d], d], pi[exps[d], d]]) for d in range(2)])
        return both.reshape(2, 2, t, nj, S5_STATE_COLS)

    pw = powers(exps_b).transpose(3, 0, 1, 2, 4).reshape(nj, 4 * t, S5_STATE_COLS)
    pc = powers(exps_c).transpose(3, 0, 1, 4, 2).reshape(nj, 4 * S5_STATE_COLS, t)

    rows = jnp.stack([pr[t, 0], pi[t, 0], pr[t, 1], pi[t, 1]])
    rows = rows.reshape(4, nj, S5_STATE_COLS).transpose(1, 0, 2)
    at = jnp.concatenate([rows, jnp.zeros_like(rows)], axis=1)
    return kl, bt, ct, pw, pc, at


def _iota(shape, axis):
    return lax.broadcasted_iota(jnp.int32, shape, axis)


def _group_blockdiag(comp, width, rows_per_group):
    k = comp.shape[1]
    n = S5_GPT * k
    rep = jnp.where((_iota((k, n), 1) & (k - 1)) == _iota((k, n), 0), 1.0, 0.0).astype(BF16)
    hi = comp.astype(BF16)
    rest = comp - hi.astype(F32)
    mid = rest.astype(BF16)
    lo = (rest - mid.astype(F32)).astype(BF16)
    tiled = (jnp.dot(hi, rep, preferred_element_type=F32) + jnp.dot(mid, rep, preferred_element_type=F32)
             + jnp.dot(lo, rep, preferred_element_type=F32))
    row_grp = (_iota((comp.shape[0], 1), 0) >> (rows_per_group.bit_length() - 1)) & (S5_GPT - 1)
    col_grp = _iota((1, n), 1) >> (width.bit_length() - 1)
    return jnp.where(row_grp == col_grp, tiled, 0.0)


def _build_s5_operators(kl_ref, bt_ref, ct_ref, pw_ref, pc_ref, wm_s, wb_s, wc_s):
    t = S5_CHUNK
    sc = S5_STATE_COLS
    lagk = _group_blockdiag(kl_ref[0], S5_H, S5_H).astype(BF16)
    for s in range(t):
        for u in range(t):
            lag = u - s + t - 1
            wm_s[s * LANES:(s + 1) * LANES, u * LANES:(u + 1) * LANES] = lagk[lag * LANES:(lag + 1) * LANES, :]
    bexp = _group_blockdiag(bt_ref[0], S5_P, S5_H)
    for d in range(2):
        b_re = bexp[(2 * d) * LANES:(2 * d + 1) * LANES]
        b_im = bexp[(2 * d + 1) * LANES:(2 * d + 2) * LANES]
        for s in range(t):
            p_re = pw_ref[0, (2 * d) * t + s:(2 * d) * t + s + 1, :]
            p_im = pw_ref[0, (2 * d + 1) * t + s:(2 * d + 1) * t + s + 1, :]
            rows = slice(s * LANES, (s + 1) * LANES)
            wb_s[rows, d * 2 * sc:d * 2 * sc + sc] = (b_re * p_re - b_im * p_im).astype(BF16)
            wb_s[rows, d * 2 * sc + sc:(d + 1) * 2 * sc] = (b_re * p_im + b_im * p_re).astype(BF16)
    cexp = _group_blockdiag(ct_ref[0], S5_H, S5_P)
    for d in range(2):
        c_re = cexp[(2 * d) * sc:(2 * d + 1) * sc]
        c_im = cexp[(2 * d + 1) * sc:(2 * d + 2) * sc]
        for u in range(t):
            p_re = pc_ref[0, (2 * d) * sc:(2 * d + 1) * sc, u:u + 1]
            p_im = pc_ref[0, (2 * d + 1) * sc:(2 * d + 2) * sc, u:u + 1]
            cols = slice(u * LANES, (u + 1) * LANES)
            wc_s[d * 2 * sc:d * 2 * sc + sc, cols] = (c_re * p_re - c_im * p_im).astype(BF16)
            wc_s[d * 2 * sc + sc:(d + 1) * 2 * sc, cols] = (-(c_re * p_im + c_im * p_re)).astype(BF16)


def _s5_kernel(x_ref, xc_ref, sh_ref, sc_ref, d_ref, kl_ref, bt_ref, ct_ref, pw_ref, pc_ref, at_ref, *rest,
               nb_half, n_lat, n_ctx, ctx_out):
    if ctx_out:
        z_ref, zc_ref, wm_s, wb_s, wc_s, xf_s, st_s, y_s, zb_s, zcb_s = rest
    else:
        z_ref, wm_s, wb_s, wc_s, xf_s, st_s, y_s, zb_s = rest
        zc_ref = zcb_s = None
    t = S5_CHUNK
    ncl = n_lat // t
    ncc = n_ctx // t
    lat_rows = nb_half * ncl
    half = pl.program_id(1)
    sc_cols = S5_STATE_COLS
    nst = sc_cols // LANES

    @pl.when(half == 0)
    def _():
        _build_s5_operators(kl_ref, bt_ref, ct_ref, pw_ref, pc_ref, wm_s, wb_s, wc_s)

    for k in range(nb_half):
        b = half * nb_half + k
        scale = 1.0 + sc_ref[0, pl.ds(b, 1), :]
        shift = sh_ref[0, pl.ds(b, 1), :]
        for s in range(t):
            xf_s[k * ncl:(k + 1) * ncl, s * LANES:(s + 1) * LANES] = (
                x_ref[pl.ds(k * n_lat + s, ncl, stride=t), :] * scale + shift)
    scale_c = 1.0 + sc_ref[0, nb_half * 2:nb_half * 2 + 1, :]
    shift_c = sh_ref[0, nb_half * 2:nb_half * 2 + 1, :]
    for k in range(nb_half):
        for s in range(t):
            r0 = lat_rows + k * ncc
            xf_s[r0:r0 + ncc, s * LANES:(s + 1) * LANES] = (
                xc_ref[pl.ds(k * n_ctx + s, ncc, stride=t), :] * scale_c + shift_c)

    xb = xf_s[...].astype(BF16)
    dvec = jnp.concatenate([d_ref[...]] * t, axis=1)
    y_s[...] = jnp.dot(xb, wm_s[...], preferred_element_type=F32) + xf_s[...] * dvec

    for d in range(2):
        local = jnp.dot(xb, wb_s[:, d * 2 * sc_cols:(d + 1) * 2 * sc_cols], preferred_element_type=F32)
        for q in range(2 * nst):
            st_s[d * 2 * nst + q] = local[:, q * LANES:(q + 1) * LANES]
    ar = [[at_ref[0, 2 * d:2 * d + 1, q * LANES:(q + 1) * LANES] for q in range(nst)] for d in range(2)]
    ai = [[at_ref[0, 2 * d + 1:2 * d + 2, q * LANES:(q + 1) * LANES] for q in range(nst)] for d in range(2)]

    def step(rows, h):
        new = [None] * (4 * nst)
        for d in range(2):
            base = d * 2 * nst
            for q in range(nst):
                hr, hi = h[base + q], h[base + nst + q]
                sr = st_s[base + q, rows[d], :]
                si = st_s[base + nst + q, rows[d], :]
                st_s[base + q, rows[d], :] = hr
                st_s[base + nst + q, rows[d], :] = hi
                new[base + q] = ar[d][q] * hr - ai[d][q] * hi + sr
                new[base + nst + q] = ar[d][q] * hi + ai[d][q] * hr + si
        return tuple(new)

    def ctx_rows(i):
        return (pl.ds(lat_rows + i, nb_half, stride=ncc), pl.ds(lat_rows + ncc - 1 - i, nb_half, stride=ncc))

    def lat_rows_at(i):
        return (pl.ds(i, nb_half, stride=ncl), pl.ds(ncl - 1 - i, nb_half, stride=ncl))

    zero = tuple(jnp.zeros((nb_half, LANES), F32) for _ in range(4 * nst))
    h = lax.fori_loop(0, ncc, lambda i, h: step(ctx_rows(i), h), zero, unroll=2)
    lax.fori_loop(0, ncl, lambda i, h: step(lat_rows_at(i), h), h, unroll=2)

    entering = jnp.concatenate([st_s[q] for q in range(4 * nst)], axis=1)
    y_s[...] += jnp.dot(entering.astype(BF16), wc_s[...], preferred_element_type=F32)

    z = jax.nn.gelu(y_s[...])
    for k in range(nb_half):
        for s in range(t):
            zb_s[pl.ds(k * n_lat + s, ncl, stride=t), :] = z[k * ncl:(k + 1) * ncl, s * LANES:(s + 1) * LANES]
    z_ref[...] = zb_s[...].astype(BF16)
    if ctx_out:
        for k in range(nb_half):
            for s in range(t):
                r0 = lat_rows + k * ncc
                zcb_s[pl.ds(k * n_ctx + s, ncc, stride=t), :] = z[r0:r0 + ncc, s * LANES:(s + 1) * LANES]
        zc_ref[...] = zcb_s[...].astype(BF16)


def _s5_core(x, xc, mods_all, layer, d_skip, ops, ctx_out):
    nb, n_lat, d = x.shape
    n_ctx = xc.shape[1]
    nj = d // LANES
    nb_half = nb // 2
    t = S5_CHUNK
    tl = t * LANES
    rows = nb_half * (n_lat + n_ctx) // t
    kern = functools.partial(_s5_kernel, nb_half=nb_half, n_lat=n_lat, n_ctx=n_ctx, ctx_out=ctx_out)
    out_shape = [jax.ShapeDtypeStruct((nb * n_lat, d), BF16)]
    out_specs = [pl.BlockSpec((nb_half * n_lat, LANES), lambda j, h: (h, j))]
    scratch = [pltpu.VMEM((tl, tl), BF16), pltpu.VMEM((tl, 4 * S5_STATE_COLS), BF16),
               pltpu.VMEM((4 * S5_STATE_COLS, tl), BF16),
               pltpu.VMEM((rows, tl), F32), pltpu.VMEM((4 * S5_STATE_COLS // LANES, rows, LANES), F32),
               pltpu.VMEM((rows, tl), F32), pltpu.VMEM((nb_half * n_lat, LANES), F32)]
    if ctx_out:
        out_shape.append(jax.ShapeDtypeStruct((nb * n_ctx, d), BF16))
        out_specs.append(pl.BlockSpec((nb_half * n_ctx, LANES), lambda j, h: (h, j)))
        scratch.append(pltpu.VMEM((nb_half * n_ctx, LANES), F32))
    res = pl.pallas_call(
        kern,
        grid=(nj, 2),
        in_specs=[
            pl.BlockSpec((nb_half * n_lat, LANES), lambda j, h: (h, j)),
            pl.BlockSpec((nb_half * n_ctx, LANES), lambda j, h: (h, j)),
            pl.BlockSpec((1, MOD_ROWS, LANES), lambda j, h: (layer, 0, j)),
            pl.BlockSpec((1, MOD_ROWS, LANES), lambda j, h: (layer, 0, nj + j)),
            pl.BlockSpec((1, LANES), lambda j, h: (0, j)),
        ] + [pl.BlockSpec((1,) + op.shape[1:], lambda j, h: (j, 0, 0)) for op in ops],
        out_specs=out_specs,
        out_shape=out_shape,
        scratch_shapes=scratch,
        compiler_params=_params(("arbitrary", "arbitrary")),
        name="s5_core",
    )(x.reshape(nb * n_lat, d), xc.reshape(nb * n_ctx, d), mods_all, mods_all, d_skip.reshape(1, d), *ops)
    z = res[0].reshape(nb, n_lat, d)
    zc = res[1].reshape(nb, n_ctx, d) if ctx_out else None
    return z, zc


def _s5_out_kernel(z_ref, w_ref, b_ref, x_ref, g_ref, lg_ref, lb_ref, o_ref, *, is_ctx, nb):
    d = x_ref.shape[-1]
    acc = jnp.dot(z_ref[0], w_ref[0], preferred_element_type=F32) + b_ref[...]
    y = acc[:, :d] * jax.nn.sigmoid(acc[:, d:])
    gate = g_ref[0, pl.ds(_mod_row(is_ctx, nb), 1), :]
    o_ref[0] = _layer_norm(DEEPNORM_ALPHA * x_ref[0] + gate * y, lg_ref[...], lb_ref[...])


def _s5_out(z, w_bf, j, b_glu, x, mods_all, layer, ln_g, ln_b, is_ctx, nb):
    n, t, d = x.shape
    tm = min(t, ROW_TILE)
    kern = functools.partial(_s5_out_kernel, is_ctx=is_ctx, nb=nb)
    tok = pl.BlockSpec((1, tm, d), lambda b, i: (b, i, 0))
    return pl.pallas_call(
        kern,
        grid=(n, t // tm),
        in_specs=[
            tok,
            pl.BlockSpec((1, d, 2 * d), lambda b, i: (j, 0, 0), pipeline_mode=pl.Buffered(1)),
            _vec_spec(2 * d),
            tok,
            _mod_spec(layer, d, 2),
            _vec_spec(d), _vec_spec(d),
        ],
        out_specs=tok,
        out_shape=jax.ShapeDtypeStruct((n, t, d), F32),
        compiler_params=_params(("arbitrary", "arbitrary")),
        name="s5_out",
    )(z, w_bf, b_glu.reshape(1, 2 * d), x, mods_all, ln_g.reshape(1, d), ln_b.reshape(1, d))


def _pw1_kernel(x_ref, sh_ref, sc_ref, w_ref, b_ref, o_ref, *, is_ctx, nb):
    d = x_ref.shape[-1]
    row = _mod_row(is_ctx, nb)
    u = x_ref[0] * (1.0 + sc_ref[0, pl.ds(row, 1), :]) + sh_ref[0, pl.ds(row, 1), :]
    acc = jnp.dot(u.astype(BF16), w_ref[0], preferred_element_type=F32) + b_ref[...]
    o_ref[0] = acc[:, :d] * jax.nn.sigmoid(acc[:, d:])


def _pw1(x, mods_all, layer, w_bf, j, b_pw1, is_ctx, nb):
    n, t, d = x.shape
    tm = min(t, ROW_TILE)
    kern = functools.partial(_pw1_kernel, is_ctx=is_ctx, nb=nb)
    tok = pl.BlockSpec((1, tm, d), lambda b, i: (b, i, 0))
    return pl.pallas_call(
        kern,
        grid=(n, t // tm),
        in_specs=[
            tok,
            _mod_spec(layer, d, 0), _mod_spec(layer, d, 1),
            pl.BlockSpec((1, d, 2 * d), lambda b, i: (j, 0, 0), pipeline_mode=pl.Buffered(1)),
            _vec_spec(2 * d),
        ],
        out_specs=tok,
        out_shape=jax.ShapeDtypeStruct((n, t, d), F32),
        compiler_params=_params(("arbitrary", "arbitrary")),
        name="conv_pw1",
    )(x, mods_all, mods_all, w_bf, b_pw1.reshape(1, 2 * d))


def _conv_tail(cv, cg_ref, cb_ref, w2_ref, b2_ref, x, gate, lg_ref, lb_ref):
    hn = _layer_norm(cv, cg_ref[...], cb_ref[...])
    hn = hn * jax.nn.sigmoid(hn)
    y = jnp.dot(hn.astype(BF16), w2_ref[0], preferred_element_type=F32) + b2_ref[...]
    return _layer_norm(DEEPNORM_ALPHA * x + gate * y, lg_ref[...], lb_ref[...])


def _conv_lat_kernel(h_ref, wdw_ref, bdw_ref, cg_ref, cb_ref, w2_ref, b2_ref, x_ref, g_ref, lg_ref, lb_ref,
                     o_ref, hp_s, wb_s, cv_s, *, n_rows, wt):
    kw = wdw_ref.shape[0]
    pad = kw // 2
    d = h_ref.shape[-1]
    nsub = wt // SUBLANES
    zeros = jnp.zeros((pad, wt, d), F32)
    hp_s[0:pad] = zeros
    hp_s[pad + n_rows:pad + n_rows + pad] = zeros
    hp_s[pad:pad + n_rows] = h_ref[0]
    for k in range(kw):
        wb_s[k] = jnp.broadcast_to(wdw_ref[k:k + 1, :], (SUBLANES, d))
    bias = bdw_ref[...]

    def conv_row(r, carry):
        accs = [jnp.zeros((SUBLANES, d), F32) for _ in range(nsub)]
        for k in range(kw):
            w8 = wb_s[k]
            for q in range(nsub):
                accs[q] = accs[q] + w8 * hp_s[r + k, q * SUBLANES:(q + 1) * SUBLANES, :]
        for q in range(nsub):
            cv_s[pl.ds(pl.multiple_of(r * wt + q * SUBLANES, SUBLANES), SUBLANES), :] = accs[q] + bias
        return carry

    lax.fori_loop(0, n_rows, conv_row, 0)
    gate = g_ref[0, pl.ds(pl.program_id(0), 1), :]
    x = x_ref[0].reshape(n_rows * wt, d)
    out = _conv_tail(cv_s[...], cg_ref, cb_ref, w2_ref, b2_ref, x, gate, lg_ref, lb_ref)
    o_ref[0] = out.reshape(n_rows, wt, d)


def _conv_ctx_kernel(h_ref, wdw_ref, bdw_ref, cg_ref, cb_ref, w2_ref, b2_ref, x_ref, g_ref, lg_ref, lb_ref,
                     o_ref, hp_s, cv_s, *, n_tok, nb):
    kw = wdw_ref.shape[0]
    pad = kw // 2
    d = h_ref.shape[-1]
    lead = 2 * SUBLANES
    rblk = 128
    cblk = 2 * LANES
    hp_s[0:lead] = jnp.zeros((lead, d), F32)
    hp_s[lead + n_tok:lead + n_tok + lead] = jnp.zeros((lead, d), F32)
    hp_s[lead:lead + n_tok] = h_ref[0]
    span = ((kw - 1 + lead - pad) // SUBLANES) * SUBLANES

    def col_block(lc, carry):
        cols = pl.ds(pl.multiple_of(lc * cblk, cblk), cblk)
        for rc in range(n_tok // rblk):
            acc = jnp.zeros((rblk, cblk), F32)
            for q in range(SUBLANES):
                taps = [k for k in range(kw) if (k + lead - pad) % SUBLANES == q]
                if not taps:
                    continue
                shifted = hp_s[pl.ds(rc * rblk + q, rblk + span), cols]
                for k in taps:
                    o = k + lead - pad - q
                    acc = acc + wdw_ref[k:k + 1, cols] * shifted[o:o + rblk]
            cv_s[rc * rblk:(rc + 1) * rblk, cols] = acc + bdw_ref[:, cols]
        return carry

    lax.fori_loop(0, d // cblk, col_block, 0)
    gate = g_ref[0, nb:nb + 1, :]
    o_ref[0] = _conv_tail(cv_s[...], cg_ref, cb_ref, w2_ref, b2_ref, x_ref[0], gate, lg_ref, lb_ref)


def _conv_post(h, w_dw, b_dw, cv_g, cv_b, w2_bf, j, b2, x, mods_all, layer, ln_g, ln_b, n_rows, is_ctx, nb):
    n, t, d = x.shape
    kw = w_dw.shape[0]
    weights = [pl.BlockSpec((kw, d), lambda b, i: (0, 0)), _vec_spec(d), _vec_spec(d), _vec_spec(d),
               pl.BlockSpec((1, d, d), lambda b, i: (j, 0, 0), pipeline_mode=pl.Buffered(1)), _vec_spec(d)]
    tail = [_mod_spec(layer, d, 2), _vec_spec(d), _vec_spec(d)]
    args_w = (w_dw, b_dw.reshape(1, d), cv_g.reshape(1, d), cv_b.reshape(1, d), w2_bf, b2.reshape(1, d))
    args_t = (mods_all, ln_g.reshape(1, d), ln_b.reshape(1, d))
    if is_ctx:
        blk = pl.BlockSpec((1, t, d), lambda b, i: (b, 0, 0))
        return pl.pallas_call(
            functools.partial(_conv_ctx_kernel, n_tok=t, nb=nb),
            grid=(n, 1),
            in_specs=[blk] + weights + [blk] + tail,
            out_specs=blk,
            out_shape=jax.ShapeDtypeStruct((n, t, d), F32),
            scratch_shapes=[pltpu.VMEM((t + 4 * SUBLANES, d), F32), pltpu.VMEM((t, d), F32)],
            compiler_params=_params(("arbitrary", "arbitrary")),
            name="conv_post_ctx",
        )(h, *args_w, x, *args_t)
    wt = CONV_WT
    width = t // n_rows
    blk = pl.BlockSpec((1, n_rows, wt, d), lambda b, i: (b, 0, i, 0))
    out = pl.pallas_call(
        functools.partial(_conv_lat_kernel, n_rows=n_rows, wt=wt),
        grid=(n, width // wt),
        in_specs=[blk] + weights + [blk] + tail,
        out_specs=blk,
        out_shape=jax.ShapeDtypeStruct((n, n_rows, width, d), F32),
        scratch_shapes=[pltpu.VMEM((n_rows + 2 * (kw // 2), wt, d), F32),
                        pltpu.VMEM((kw, SUBLANES, d), F32),
                        pltpu.VMEM((n_rows * wt, d), F32)],
        compiler_params=_params(("arbitrary", "arbitrary")),
        name="conv_post",
    )(h.reshape(n, n_rows, width, d), *args_w, x.reshape(n, n_rows, width, d), *args_t)
    return out.reshape(n, t, d)


def _router_kernel(x_ref, sh_ref, sc_ref, wr_ref, u_ref, a_ref, *, is_ctx, nb):
    row = _mod_row(is_ctx, nb)
    u = x_ref[0] * (1.0 + sc_ref[0, pl.ds(row, 1), :]) + sh_ref[0, pl.ds(row, 1), :]
    u_ref[0] = u.astype(BF16)
    uh, ul = _split_bf16(u)
    wh, wl = _split_bf16(wr_ref[0])
    nt = (((1,), (1,)), ((), ()))
    logits = (lax.dot_general(wh, uh, nt, preferred_element_type=F32)
              + lax.dot_general(wh, ul, nt, preferred_element_type=F32)
              + lax.dot_general(wl, uh, nt, preferred_element_type=F32))
    m = jnp.max(logits, axis=0, keepdims=True)
    ex = jnp.exp(logits - m)
    a_ref[0] = ex / jnp.sum(ex, axis=0, keepdims=True)


def _router(x, mods_all, layer, w_router_t, is_ctx, nb):
    n, t, d = x.shape
    e = w_router_t.shape[1]
    tm = min(t, ROW_TILE)
    kern = functools.partial(_router_kernel, is_ctx=is_ctx, nb=nb)
    tok = pl.BlockSpec((1, tm, d), lambda b, i: (b, i, 0))
    return pl.pallas_call(
        kern,
        grid=(n, t // tm),
        in_specs=[tok, _mod_spec(layer, d, 3), _mod_spec(layer, d, 4),
                  pl.BlockSpec((1, e, d), lambda b, i: (layer, 0, 0))],
        out_specs=[tok, pl.BlockSpec((1, e, tm), lambda b, i: (b, 0, i))],
        out_shape=[jax.ShapeDtypeStruct((n, t, d), BF16), jax.ShapeDtypeStruct((n, e, t), F32)],
        compiler_params=_params(("arbitrary", "arbitrary")),
        name="router",
    )(x, mods_all, mods_all, w_router_t)


def _topk_kernel(a_ref, p_ref, pt_ref, *, cap):
    a = a_ref[0]
    e, t = a.shape
    capf = jnp.float32(cap)

    def count(mask):
        return jnp.sum(jnp.where(mask, 1.0, 0.0), axis=1, keepdims=True)

    def as_row_values(bits):
        return jnp.concatenate([pltpu.bitcast(bits, F32)] * (t // LANES), axis=1)

    thr_bits = jnp.zeros((e, LANES), jnp.int32)
    for bit in range(30, -1, -1):
        cand = thr_bits | jnp.int32(1 << bit)
        keep = count(a >= as_row_values(cand)) >= capf
        thr_bits = jnp.where(keep, cand, thr_bits)
    thr = as_row_values(thr_bits)
    gt = a > thr
    eq = a == thr
    need = capf - count(gt)
    tri = jnp.where(_iota((t, t), 0) <= _iota((t, t), 1), 1.0, 0.0).astype(BF16)
    eq_f = jnp.where(eq, 1.0, 0.0)
    eq_rank = jnp.dot(eq_f.astype(BF16), tri, preferred_element_type=F32) - eq_f
    sel = jnp.where(gt, 1.0, jnp.where(eq & (eq_rank < need), 1.0, 0.0))
    slot = jnp.dot(sel.astype(BF16), tri, preferred_element_type=F32) - 1.0
    pos = jnp.where(sel > 0.0, slot, -1.0)
    p_ref[0] = pos.astype(jnp.int32)
    padded = jnp.concatenate([pos, jnp.full((LANES - e, t), -1.0, F32)], axis=0)
    pt_ref[0] = padded.T.astype(jnp.int32)


def _topk(aff_t, cap):
    n, e, t = aff_t.shape
    kern = functools.partial(_topk_kernel, cap=cap)
    return pl.pallas_call(
        kern,
        grid=(n,),
        in_specs=[pl.BlockSpec((1, e, t), lambda b: (b, 0, 0))],
        out_specs=[pl.BlockSpec((1, e, t), lambda b: (b, 0, 0)),
                   pl.BlockSpec((1, t, LANES), lambda b: (b, 0, 0))],
        out_shape=[jax.ShapeDtypeStruct((n, e, t), jnp.int32), jax.ShapeDtypeStruct((n, t, LANES), jnp.int32)],
        compiler_params=_params(("arbitrary",)),
        name="topk",
    )(aff_t)


def _gather_kernel(u_ref, p_ref, a_ref, xs_ref, gs_ref, *, cap):
    e = pl.program_id(1)
    pos = p_ref[0, pl.ds(e, 1), :]
    t = pos.shape[1]
    hit = _iota((cap, t), 0) == pos
    onehot = jnp.where(hit, 1.0, 0.0).astype(BF16)
    xs_ref[0] = jnp.dot(onehot, u_ref[0], preferred_element_type=F32).astype(BF16)
    gate = jnp.sum(jnp.where(hit, a_ref[0, pl.ds(e, 1), :], 0.0), axis=1, keepdims=True)
    gs_ref[0] = jnp.broadcast_to(gate, (cap, LANES))


def _gather(u2, pos_t, aff_t, cap):
    n, t, d = u2.shape
    e = pos_t.shape[1]
    return pl.pallas_call(
        functools.partial(_gather_kernel, cap=cap),
        grid=(n, e),
        in_specs=[pl.BlockSpec((1, t, d), lambda b, ei: (b, 0, 0)),
                  pl.BlockSpec((1, e, t), lambda b, ei: (b, 0, 0)),
                  pl.BlockSpec((1, e, t), lambda b, ei: (b, 0, 0))],
        out_specs=[pl.BlockSpec((1, cap, d), lambda b, ei: (ei, b, 0)),
                   pl.BlockSpec((1, cap, LANES), lambda b, ei: (ei, b, 0))],
        out_shape=[jax.ShapeDtypeStruct((e, n * cap, d), BF16),
                   jax.ShapeDtypeStruct((e, n * cap, LANES), F32)],
        compiler_params=_params(("arbitrary", "arbitrary")),
        name="gather",
    )(u2, pos_t, aff_t)


def _ffn_kernel(*refs, n_streams):
    xs_refs = refs[0:2 * n_streams:2]
    gs_refs = refs[1:2 * n_streams:2]
    wg_ref, wu_ref, wo_ref = refs[2 * n_streams:2 * n_streams + 3]
    y_refs = refs[2 * n_streams + 3:3 * n_streams + 3]
    act_s, wo_s = refs[3 * n_streams + 3:3 * n_streams + 5]
    fc = pl.program_id(1)
    nfc = act_s.shape[0]
    bounds = [0]
    for r in xs_refs:
        bounds.append(bounds[-1] + r.shape[1])
    if n_streams == 1:
        x = xs_refs[0][0]
    else:
        x_s = refs[-1]

        @pl.when(fc == 0)
        def _():
            for k, r in enumerate(xs_refs):
                x_s[bounds[k]:bounds[k + 1], :] = r[0]

        x = x_s[...]
    g = jnp.dot(x, wg_ref[0, 0].astype(BF16), preferred_element_type=F32)
    up = jnp.dot(x, wu_ref[0, 0].astype(BF16), preferred_element_type=F32)
    act_s[fc] = ((g * jax.nn.sigmoid(g)) * up).astype(BF16)
    wo_s[fc] = wo_ref[0, 0].astype(BF16)

    @pl.when(fc == nfc - 1)
    def _():
        act = jnp.concatenate([act_s[k] for k in range(nfc)], axis=1)
        w_out = wo_s[...].reshape(nfc * wo_s.shape[1], wo_s.shape[2])
        y = jnp.dot(act, w_out, preferred_element_type=F32)
        for k in range(n_streams):
            y_refs[k][0] = (y[bounds[k]:bounds[k + 1]] * gs_refs[k][0][:, 0:1]).astype(BF16)


def _ffn(streams, w_in, w_out, layer):
    e, _, d = streams[0][0].shape
    ff = w_out.shape[2]
    nfc = ff // FF_TILE
    n_streams = len(streams)
    rows = [xs.shape[1] for xs, _ in streams]
    in_specs, args = [], []
    for (xs, gs), r in zip(streams, rows):
        in_specs += [pl.BlockSpec((1, r, d), lambda ei, fc: (ei, 0, 0), pipeline_mode=pl.Buffered(1)),
                     pl.BlockSpec((1, r, LANES), lambda ei, fc: (ei, 0, 0))]
        args += [xs, gs]
    in_specs += [pl.BlockSpec((1, 1, d, FF_TILE), lambda ei, fc: (layer, ei, 0, fc)),
                 pl.BlockSpec((1, 1, d, FF_TILE), lambda ei, fc: (layer, ei, 0, nfc + fc)),
                 pl.BlockSpec((1, 1, FF_TILE, d), lambda ei, fc: (layer, ei, fc, 0))]
    scratch = [pltpu.VMEM((nfc, sum(rows), FF_TILE), BF16), pltpu.VMEM((nfc, FF_TILE, d), BF16)]
    if n_streams > 1:
        scratch.append(pltpu.VMEM((sum(rows), d), BF16))
    return pl.pallas_call(
        functools.partial(_ffn_kernel, n_streams=n_streams),
        grid=(e, nfc),
        in_specs=in_specs,
        out_specs=[pl.BlockSpec((1, r, d), lambda ei, fc: (ei, 0, 0)) for r in rows],
        out_shape=[jax.ShapeDtypeStruct((e, r, d), BF16) for r in rows],
        scratch_shapes=scratch,
        compiler_params=_params(("arbitrary", "arbitrary")),
        name="expert_ffn",
    )(*args, w_in, w_in, w_out)


def _combine_kernel(y_ref, pt_ref, x_ref, g_ref, lg_ref, lb_ref, o_ref, oh_s, *, is_ctx, nb, cap):
    n_exp, _, d = y_ref.shape
    ec = n_exp * cap
    cw = min(ec, COMBINE_COLS)
    pos = pt_ref[0].astype(F32).astype(BF16)
    for c0 in range(0, ec, cw):
        src = _iota((LANES, cw), 0)
        dst = _iota((LANES, cw), 1) + c0
        expand = jnp.where((dst >= src * cap) & (dst < (src + 1) * cap), 1.0, 0.0).astype(BF16)
        pe = jnp.dot(pos, expand, preferred_element_type=F32)
        slot = (_iota((1, cw), 1) + c0) % cap
        oh_s[:, c0:c0 + cw] = jnp.where(pe == slot.astype(F32), 1.0, 0.0).astype(BF16)
    f = jnp.dot(oh_s[...], y_ref[...].reshape(ec, d), preferred_element_type=F32)
    gate = g_ref[0, pl.ds(_mod_row(is_ctx, nb), 1), :]
    o_ref[0] = _layer_norm(DEEPNORM_ALPHA * x_ref[0] + gate * f, lg_ref[...], lb_ref[...])


def _combine(y, pos_tm, x, mods_all, layer, ln_g, ln_b, cap, is_ctx, nb):
    n, t, d = x.shape
    e = y.shape[0]
    tm = min(t, ROW_TILE)
    kern = functools.partial(_combine_kernel, is_ctx=is_ctx, nb=nb, cap=cap)
    tok = pl.BlockSpec((1, tm, d), lambda b, i: (b, i, 0))
    return pl.pallas_call(
        kern,
        grid=(n, t // tm),
        in_specs=[
            pl.BlockSpec((e, cap, d), lambda b, i: (0, b, 0), pipeline_mode=pl.Buffered(1)),
            pl.BlockSpec((1, tm, LANES), lambda b, i: (b, i, 0)),
            tok,
            _mod_spec(layer, d, 5),
            _vec_spec(d), _vec_spec(d),
        ],
        out_specs=tok,
        out_shape=jax.ShapeDtypeStruct((n, t, d), F32),
        scratch_shapes=[pltpu.VMEM((tm, e * cap), BF16)],
        compiler_params=_params(("arbitrary", "arbitrary")),
        name="combine",
    )(y, pos_tm, x, mods_all, ln_g.reshape(1, d), ln_b.reshape(1, d))


def _moe_post(x1_lat, x1_ctx, mods_all, layer, w_router_t, w_in, w_out, ln_g, ln_b, nb):
    streams = [(x1_lat, False)] + ([(x1_ctx, True)] if x1_ctx is not None else [])
    caps = [EC_CAPACITY_FACTOR * s.shape[1] // N_EXPERTS for s, _ in streams]
    routed, gathered = [], []
    for (s, is_ctx), cap in zip(streams, caps):
        u2, aff_t = _router(s, mods_all, layer, w_router_t, is_ctx, nb)
        pos_t, pos_tm = _topk(aff_t, cap)
        gathered.append(_gather(u2, pos_t, aff_t, cap))
        routed.append(pos_tm)
    ys = _ffn(gathered, w_in, w_out, layer)
    outs = [_combine(y, pos_tm, s, mods_all, layer, ln_g, ln_b, cap, is_ctx, nb)
            for (s, is_ctx), pos_tm, cap, y in zip(streams, routed, caps, ys)]
    return outs[0], (outs[1] if len(outs) > 1 else None)


def kernel(x, c, ctx, c_ctx, ada_w, ada_b, ln_g, ln_b, s5_a_re, s5_a_im, s5_log_dt, s5_b_re, s5_b_im,
           s5_c_re, s5_c_im, s5_d, s5_w_glu, s5_b_glu, cv_w_pw1, cv_b_pw1, cv_w_dw, cv_b_dw, cv_ln_g,
           cv_ln_b, cv_w_pw2, cv_b_pw2, moe_w_router, moe_w_in, moe_w_out):
    nb, seq, d = x.shape
    assert nb + 1 <= MOD_ROWS and nb % 2 == 0 and d % LANES == 0
    rows = seq // GRID_W

    c8 = jnp.concatenate([c.astype(F32), c_ctx.astype(F32)[None], jnp.zeros((MOD_ROWS - nb - 1, d), F32)], axis=0)
    mods_all = _ada_all(c8, ada_w, ada_b)
    w_glu_bf = s5_w_glu.astype(BF16)
    w_pw1_bf = cv_w_pw1.astype(BF16)
    w_pw2_bf = cv_w_pw2.astype(BF16)
    w_router_t = jnp.swapaxes(moe_w_router, 1, 2)

    x_lat, x_ctx = x, ctx
    for i in range(DEPTH):
        is_s5 = (i % N_MIXERS) == 0
        j = i // N_MIXERS
        ctx_out = any((k % N_MIXERS) == 0 for k in range(i + 1, DEPTH))
        x1_ctx = None

        if is_s5:
            ops = _s5_operators(s5_a_re[j], s5_a_im[j], s5_log_dt[j], s5_b_re[j], s5_b_im[j],
                                s5_c_re[j], s5_c_im[j])
            z_lat, z_ctx = _s5_core(x_lat, x_ctx, mods_all, i, s5_d[j], ops, ctx_out)
            x1_lat = _s5_out(z_lat, w_glu_bf, j, s5_b_glu[j], x_lat, mods_all, i, ln_g[i, 0], ln_b[i, 0], False, nb)
            if ctx_out:
                x1_ctx = _s5_out(z_ctx, w_glu_bf, j, s5_b_glu[j], x_ctx, mods_all, i, ln_g[i, 0], ln_b[i, 0],
                                 True, nb)
        else:
            conv_args = (cv_w_dw[j], cv_b_dw[j], cv_ln_g[j], cv_ln_b[j], w_pw2_bf, j, cv_b_pw2[j])
            h_lat = _pw1(x_lat, mods_all, i, w_pw1_bf, j, cv_b_pw1[j], False, nb)
            x1_lat = _conv_post(h_lat, *conv_args, x_lat, mods_all, i, ln_g[i, 0], ln_b[i, 0], rows, False, nb)
            if ctx_out:
                h_ctx = _pw1(x_ctx, mods_all, i, w_pw1_bf, j, cv_b_pw1[j], True, nb)
                x1_ctx = _conv_post(h_ctx, *conv_args, x_ctx, mods_all, i, ln_g[i, 0], ln_b[i, 0], None, True, nb)

        x_lat, x_ctx_new = _moe_post(x1_lat, x1_ctx, mods_all, i, w_router_t, moe_w_in, moe_w_out,
                                     ln_g[i, 1], ln_b[i, 1], nb)
        if ctx_out:
            x_ctx = x_ctx_new
    return x_lat
```

```python
import functools

import jax
import jax.numpy as jnp
from jax import lax
from jax.experimental import pallas as pl
from jax.experimental.pallas import tpu as pltpu

F32 = jnp.float32
BF16 = jnp.bfloat16

DEPTH = 4
N_MIXERS = 2
GRID_W = 64
S5_H = 16
S5_P = 64
N_EXPERTS = 16
EC_CAPACITY_FACTOR = 2
DEEPNORM_ALPHA = (2.0 * DEPTH) ** 0.25
LN_EPS = 1e-5

LANES = 128
SUBLANES = 8
S5_CHUNK = 8
S5_GPT = LANES // S5_H
S5_STATE_COLS = S5_GPT * S5_P
MOD_ROWS = 8
VMEM_LIMIT = 56 * 1024 * 1024
ROW_TILE = 512
COMBINE_COLS = 1024
FF_TILE = 256
CONV_WT = 2 * SUBLANES


def _params(sem, vmem=VMEM_LIMIT):
    return pltpu.CompilerParams(dimension_semantics=sem, vmem_limit_bytes=vmem)


def _layer_norm(v, g, b):
    mu = jnp.mean(v, axis=-1, keepdims=True)
    c = v - mu
    var = jnp.mean(c * c, axis=-1, keepdims=True)
    return c * lax.rsqrt(var + LN_EPS) * g + b


def _split_bf16(v):
    hi = v.astype(BF16)
    lo = (v - hi.astype(F32)).astype(BF16)
    return hi, lo


def _mod_spec(layer, d, k):
    return pl.BlockSpec((1, MOD_ROWS, d), lambda *_: (layer, 0, k))


def _vec_spec(d):
    return pl.BlockSpec((1, d), lambda *_: (0, 0))


def _mod_row(is_ctx, nb):
    return nb if is_ctx else pl.program_id(0)


def _ada_kernel(c_ref, w_ref, b_ref, o_ref):
    c = c_ref[...]
    cond = c * jax.nn.sigmoid(c)
    hi, lo = _split_bf16(cond)
    lhs = jnp.concatenate([hi, lo], axis=0)
    r = jnp.dot(lhs, w_ref[0].astype(BF16), preferred_element_type=F32)
    o_ref[0] = r[:MOD_ROWS] + r[MOD_ROWS:] + b_ref[0]


def _ada_all(c8, ada_w, ada_b):
    depth, d, n = ada_w.shape
    tn = 1024
    return pl.pallas_call(
        _ada_kernel,
        grid=(depth, n // tn),
        in_specs=[
            pl.BlockSpec((MOD_ROWS, d), lambda i, k: (0, 0)),
            pl.BlockSpec((1, d, tn), lambda i, k: (i, 0, k)),
            pl.BlockSpec((1, 1, tn), lambda i, k: (i, 0, k)),
        ],
        out_specs=pl.BlockSpec((1, MOD_ROWS, tn), lambda i, k: (i, 0, k)),
        out_shape=jax.ShapeDtypeStruct((depth, MOD_ROWS, n), F32),
        compiler_params=_params(("arbitrary", "arbitrary")),
        name="adaln",
    )(c8, ada_w, ada_b.reshape(depth, 1, n))


def _cmul(ar, ai, br, bi):
    return ar * br - ai * bi, ar * bi + ai * br


def _s5_operators(a_re, a_im, log_dt, b_re, b_im, c_re, c_im):
    hp = lax.Precision.HIGHEST
    t = S5_CHUNK
    g = a_re.shape[1]
    nj = g // S5_GPT
    lam_r, lam_i = a_re.astype(F32), a_im.astype(F32)
    dt = jnp.exp(log_dt.astype(F32))[..., None]
    mag = jnp.exp(lam_r * dt)
    abar_r, abar_i = mag * jnp.cos(lam_i * dt), mag * jnp.sin(lam_i * dt)
    den = lam_r * lam_r + lam_i * lam_i
    xr, xi = abar_r - 1.0, abar_i
    coef_r, coef_i = (xr * lam_r + xi * lam_i) / den, (xi * lam_r - xr * lam_i) / den
    bb_r, bb_i = _cmul(coef_r[..., None], coef_i[..., None], b_re.astype(F32), b_im.astype(F32))
    cm_r, cm_i = c_re.astype(F32), c_im.astype(F32)
    pr, pi = [jnp.ones_like(abar_r)], [jnp.zeros_like(abar_r)]
    for _ in range(t):
        nr, ni = _cmul(pr[-1], pi[-1], abar_r, abar_i)
        pr.append(nr)
        pi.append(ni)
    pr, pi = jnp.stack(pr), jnp.stack(pi)

    def lag_kernels(d):
        qr, qi = _cmul(pr[:t, d, :, :, None], pi[:t, d, :, :, None], bb_r[d][None], bb_i[d][None])
        return (jnp.einsum('ghp,kgpj->kghj', cm_r[d], qr, precision=hp)
                - jnp.einsum('ghp,kgpj->kghj', cm_i[d], qi, precision=hp))

    kf, kb = lag_kernels(0), lag_kernels(1)
    kall = jnp.concatenate([kb[:0:-1], (kf[0] + kb[0])[None], kf[1:]], axis=0)
    nlag = 2 * t - 1
    kl = kall.reshape(nlag, nj, S5_GPT, S5_H, S5_H).transpose(1, 0, 2, 4, 3).reshape(nj, nlag * LANES, S5_H)

    def per_tile(re, im, perm, rows, cols):
        both = jnp.stack([re, im], axis=1)
        both = both.reshape(2, 2, nj, S5_GPT, both.shape[-2], both.shape[-1])
        return both.transpose(perm).reshape(nj, rows, cols)

    bt = per_tile(bb_r, bb_i, (2, 0, 1, 3, 5, 4), 4 * LANES, S5_P)
    ct = per_tile(cm_r, cm_i, (2, 0, 1, 3, 5, 4), 4 * S5_STATE_COLS, S5_H)

    exps_b = (jnp.arange(t - 1, -1, -1), jnp.arange(t))
    exps_c = (jnp.arange(1, t + 1), jnp.arange(t, 0, -1))

    def powers(exps):
        both = jnp.stack([jnp.stack([pr[exps[d], d], pi[exps[d], d]]) for d in range(2)])
        return both.reshape(2, 2, t, nj, S5_STATE_COLS)

    pw = powers(exps_b).transpose(3, 0, 1, 2, 4).reshape(nj, 4 * t, S5_STATE_COLS)
    pc = powers(exps_c).transpose(3, 0, 1, 4, 2).reshape(nj, 4 * S5_STATE_COLS, t)

    def decay_tiles(d):
        ar = pr[t, d].reshape(nj, S5_STATE_COLS // LANES, LANES)
        ai = pi[t, d].reshape(nj, S5_STATE_COLS // LANES, LANES)
        return jnp.concatenate([ar, ar, -ai, ai], axis=1)

    at = jnp.concatenate([decay_tiles(0), decay_tiles(1)], axis=1)
    return kl, bt, ct, pw, pc, at


def _iota(shape, axis):
    return lax.broadcasted_iota(jnp.int32, shape, axis)


def _group_blockdiag(comp, width, rows_per_group):
    k = comp.shape[1]
    n = S5_GPT * k
    rep = jnp.where((_iota((k, n), 1) & (k - 1)) == _iota((k, n), 0), 1.0, 0.0).astype(BF16)
    hi = comp.astype(BF16)
    rest = comp - hi.astype(F32)
    mid = rest.astype(BF16)
    lo = (rest - mid.astype(F32)).astype(BF16)
    tiled = (jnp.dot(hi, rep, preferred_element_type=F32) + jnp.dot(mid, rep, preferred_element_type=F32)
             + jnp.dot(lo, rep, preferred_element_type=F32))
    row_grp = (_iota((comp.shape[0], 1), 0) >> (rows_per_group.bit_length() - 1)) & (S5_GPT - 1)
    col_grp = _iota((1, n), 1) >> (width.bit_length() - 1)
    return jnp.where(row_grp == col_grp, tiled, 0.0)


def _build_s5_operators(kl_ref, bt_ref, ct_ref, pw_ref, pc_ref, wm_s, wb_s, wc_s):
    t = S5_CHUNK
    sc = S5_STATE_COLS
    lagk = _group_blockdiag(kl_ref[0], S5_H, S5_H).astype(BF16)
    for s in range(t):
        for u in range(t):
            lag = u - s + t - 1
            wm_s[s * LANES:(s + 1) * LANES, u * LANES:(u + 1) * LANES] = lagk[lag * LANES:(lag + 1) * LANES, :]
    bexp = _group_blockdiag(bt_ref[0], S5_P, S5_H)
    for d in range(2):
        b_re = bexp[(2 * d) * LANES:(2 * d + 1) * LANES]
        b_im = bexp[(2 * d + 1) * LANES:(2 * d + 2) * LANES]
        for s in range(t):
            p_re = pw_ref[0, (2 * d) * t + s:(2 * d) * t + s + 1, :]
            p_im = pw_ref[0, (2 * d + 1) * t + s:(2 * d + 1) * t + s + 1, :]
            rows = slice(s * LANES, (s + 1) * LANES)
            wb_s[rows, d * 2 * sc:d * 2 * sc + sc] = (b_re * p_re - b_im * p_im).astype(BF16)
            wb_s[rows, d * 2 * sc + sc:(d + 1) * 2 * sc] = (b_re * p_im + b_im * p_re).astype(BF16)
    cexp = _group_blockdiag(ct_ref[0], S5_H, S5_P)
    for d in range(2):
        c_re = cexp[(2 * d) * sc:(2 * d + 1) * sc]
        c_im = cexp[(2 * d + 1) * sc:(2 * d + 2) * sc]
        for u in range(t):
            p_re = pc_ref[0, (2 * d) * sc:(2 * d + 1) * sc, u:u + 1]
            p_im = pc_ref[0, (2 * d + 1) * sc:(2 * d + 2) * sc, u:u + 1]
            cols = slice(u * LANES, (u + 1) * LANES)
            wc_s[d * 2 * sc:d * 2 * sc + sc, cols] = (c_re * p_re - c_im * p_im).astype(BF16)
            wc_s[d * 2 * sc + sc:(d + 1) * 2 * sc, cols] = (-(c_re * p_im + c_im * p_re)).astype(BF16)


def _s5_kernel(x_ref, xc_ref, sh_ref, sc_ref, d_ref, kl_ref, bt_ref, ct_ref, pw_ref, pc_ref, at_ref, *rest,
               nb_half, n_lat, n_ctx, ctx_out):
    if ctx_out:
        z_ref, zc_ref, wm_s, wb_s, wc_s, xf_s, st_s, en_s, y_s, zb_s, zcb_s = rest
    else:
        z_ref, wm_s, wb_s, wc_s, xf_s, st_s, en_s, y_s, zb_s = rest
        zc_ref = zcb_s = None
    t = S5_CHUNK
    ncl = n_lat // t
    ncc = n_ctx // t
    lat_rows = nb_half * ncl
    half = pl.program_id(1)
    sc_cols = S5_STATE_COLS
    nst = sc_cols // LANES
    n_rows = nb_half * (ncl + ncc)

    @pl.when(half == 0)
    def _():
        _build_s5_operators(kl_ref, bt_ref, ct_ref, pw_ref, pc_ref, wm_s, wb_s, wc_s)

    for k in range(nb_half):
        b = half * nb_half + k
        scale = 1.0 + sc_ref[0, pl.ds(b, 1), :]
        shift = sh_ref[0, pl.ds(b, 1), :]
        for s in range(t):
            xf_s[k * ncl:(k + 1) * ncl, s * LANES:(s + 1) * LANES] = (
                x_ref[pl.ds(k * n_lat + s, ncl, stride=t), :] * scale + shift)
    scale_c = 1.0 + sc_ref[0, nb_half * 2:nb_half * 2 + 1, :]
    shift_c = sh_ref[0, nb_half * 2:nb_half * 2 + 1, :]
    for k in range(nb_half):
        for s in range(t):
            r0 = lat_rows + k * ncc
            xf_s[r0:r0 + ncc, s * LANES:(s + 1) * LANES] = (
                xc_ref[pl.ds(k * n_ctx + s, ncc, stride=t), :] * scale_c + shift_c)

    xb = xf_s[...].astype(BF16)
    dvec = jnp.concatenate([d_ref[...]] * t, axis=1)
    y_s[...] = jnp.dot(xb, wm_s[...], preferred_element_type=F32) + xf_s[...] * dvec

    for d in range(2):
        local = jnp.dot(xb, wb_s[:, d * 2 * sc_cols:(d + 1) * 2 * sc_cols], preferred_element_type=F32)
        for q in range(2 * nst):
            st_s[d, pl.ds(q, n_rows, stride=2 * nst), :] = local[:, q * LANES:(q + 1) * LANES]
    coef = [(at_ref[0, (2 * d) * 2 * nst:(2 * d + 1) * 2 * nst, :],
             at_ref[0, (2 * d + 1) * 2 * nst:(2 * d + 2) * 2 * nst, :]) for d in range(2)]

    def step(chunk_rows, h):
        new = []
        for d in range(2):
            for k in range(nb_half):
                tile = pl.ds(pl.multiple_of((chunk_rows[d] + k * chunk_rows[2]) * 2 * nst, 2 * nst), 2 * nst)
                prev, prev_sw = h[2 * (d * nb_half + k)], h[2 * (d * nb_half + k) + 1]
                local_state = st_s[d, tile, :]
                en_s[d, tile, :] = prev
                new.append(coef[d][0] * prev + coef[d][1] * prev_sw + local_state)
                new.append(coef[d][0] * prev_sw - coef[d][1] * prev + pltpu.roll(local_state, nst, axis=0))
        return tuple(new)

    zero = tuple(jnp.zeros((2 * nst, LANES), F32) for _ in range(4 * nb_half))
    h = lax.fori_loop(0, ncc, lambda i, h: step((lat_rows + i, lat_rows + ncc - 1 - i, ncc), h), zero, unroll=4)
    lax.fori_loop(0, ncl, lambda i, h: step((i, ncl - 1 - i, ncl), h), h, unroll=4)

    entering = jnp.concatenate([en_s[d, pl.ds(q, n_rows, stride=2 * nst), :]
                                for d in range(2) for q in range(2 * nst)], axis=1)
    y_s[...] += jnp.dot(entering.astype(BF16), wc_s[...], preferred_element_type=F32)

    z = jax.nn.gelu(y_s[...])
    for k in range(nb_half):
        for s in range(t):
            zb_s[pl.ds(k * n_lat + s, ncl, stride=t), :] = z[k * ncl:(k + 1) * ncl, s * LANES:(s + 1) * LANES]
    z_ref[...] = zb_s[...].astype(BF16)
    if ctx_out:
        for k in range(nb_half):
            for s in range(t):
                r0 = lat_rows + k * ncc
                zcb_s[pl.ds(k * n_ctx + s, ncc, stride=t), :] = z[r0:r0 + ncc, s * LANES:(s + 1) * LANES]
        zc_ref[...] = zcb_s[...].astype(BF16)


def _s5_core(x, xc, mods_all, layer, d_skip, ops, ctx_out):
    nb, n_lat, d = x.shape
    n_ctx = xc.shape[1]
    nj = d // LANES
    nb_half = nb // 2
    t = S5_CHUNK
    tl = t * LANES
    rows = nb_half * (n_lat + n_ctx) // t
    kern = functools.partial(_s5_kernel, nb_half=nb_half, n_lat=n_lat, n_ctx=n_ctx, ctx_out=ctx_out)
    out_shape = [jax.ShapeDtypeStruct((nb * n_lat, d), BF16)]
    out_specs = [pl.BlockSpec((nb_half * n_lat, LANES), lambda j, h: (h, j))]
    scratch = [pltpu.VMEM((tl, tl), BF16), pltpu.VMEM((tl, 4 * S5_STATE_COLS), BF16),
               pltpu.VMEM((4 * S5_STATE_COLS, tl), BF16),
               pltpu.VMEM((rows, tl), F32), pltpu.VMEM((2, rows * 2 * S5_STATE_COLS // LANES, LANES), F32),
               pltpu.VMEM((2, rows * 2 * S5_STATE_COLS // LANES, LANES), F32),
               pltpu.VMEM((rows, tl), F32), pltpu.VMEM((nb_half * n_lat, LANES), F32)]
    if ctx_out:
        out_shape.append(jax.ShapeDtypeStruct((nb * n_ctx, d), BF16))
        out_specs.append(pl.BlockSpec((nb_half * n_ctx, LANES), lambda j, h: (h, j)))
        scratch.append(pltpu.VMEM((nb_half * n_ctx, LANES), F32))
    res = pl.pallas_call(
        kern,
        grid=(nj, 2),
        in_specs=[
            pl.BlockSpec((nb_half * n_lat, LANES), lambda j, h: (h, j)),
            pl.BlockSpec((nb_half * n_ctx, LANES), lambda j, h: (h, j)),
            pl.BlockSpec((1, MOD_ROWS, LANES), lambda j, h: (layer, 0, j)),
            pl.BlockSpec((1, MOD_ROWS, LANES), lambda j, h: (layer, 0, nj + j)),
            pl.BlockSpec((1, LANES), lambda j, h: (0, j)),
        ] + [pl.BlockSpec((1,) + op.shape[1:], lambda j, h: (j, 0, 0)) for op in ops],
        out_specs=out_specs,
        out_shape=out_shape,
        scratch_shapes=scratch,
        compiler_params=_params(("arbitrary", "arbitrary")),
        name="s5_core",
    )(x.reshape(nb * n_lat, d), xc.reshape(nb * n_ctx, d), mods_all, mods_all, d_skip.reshape(1, d), *ops)
    z = res[0].reshape(nb, n_lat, d)
    zc = res[1].reshape(nb, n_ctx, d) if ctx_out else None
    return z, zc


def _s5_out_kernel(z_ref, w_ref, b_ref, x_ref, g_ref, lg_ref, lb_ref, o_ref, *, is_ctx, nb):
    d = x_ref.shape[-1]
    acc = jnp.dot(z_ref[0], w_ref[0], preferred_element_type=F32) + b_ref[...]
    y = acc[:, :d] * jax.nn.sigmoid(acc[:, d:])
    gate = g_ref[0, pl.ds(_mod_row(is_ctx, nb), 1), :]
    o_ref[0] = _layer_norm(DEEPNORM_ALPHA * x_ref[0] + gate * y, lg_ref[...], lb_ref[...])


def _s5_out(z, w_bf, j, b_glu, x, mods_all, layer, ln_g, ln_b, is_ctx, nb):
    n, t, d = x.shape
    tm = min(t, ROW_TILE)
    kern = functools.partial(_s5_out_kernel, is_ctx=is_ctx, nb=nb)
    tok = pl.BlockSpec((1, tm, d), lambda b, i: (b, i, 0))
    return pl.pallas_call(
        kern,
        grid=(n, t // tm),
        in_specs=[
            tok,
            pl.BlockSpec((1, d, 2 * d), lambda b, i: (j, 0, 0), pipeline_mode=pl.Buffered(1)),
            _vec_spec(2 * d),
            tok,
            _mod_spec(layer, d, 2),
            _vec_spec(d), _vec_spec(d),
        ],
        out_specs=tok,
        out_shape=jax.ShapeDtypeStruct((n, t, d), F32),
        compiler_params=_params(("arbitrary", "arbitrary")),
        name="s5_out",
    )(z, w_bf, b_glu.reshape(1, 2 * d), x, mods_all, ln_g.reshape(1, d), ln_b.reshape(1, d))


def _pw1_kernel(x_ref, sh_ref, sc_ref, w_ref, b_ref, o_ref, *, is_ctx, nb):
    d = x_ref.shape[-1]
    row = _mod_row(is_ctx, nb)
    u = x_ref[0] * (1.0 + sc_ref[0, pl.ds(row, 1), :]) + sh_ref[0, pl.ds(row, 1), :]
    acc = jnp.dot(u.astype(BF16), w_ref[0], preferred_element_type=F32) + b_ref[...]
    o_ref[0] = acc[:, :d] * jax.nn.sigmoid(acc[:, d:])


def _pw1(x, mods_all, layer, w_bf, j, b_pw1, is_ctx, nb):
    n, t, d = x.shape
    tm = min(t, ROW_TILE)
    kern = functools.partial(_pw1_kernel, is_ctx=is_ctx, nb=nb)
    tok = pl.BlockSpec((1, tm, d), lambda b, i: (b, i, 0))
    return pl.pallas_call(
        kern,
        grid=(n, t // tm),
        in_specs=[
            tok,
            _mod_spec(layer, d, 0), _mod_spec(layer, d, 1),
            pl.BlockSpec((1, d, 2 * d), lambda b, i: (j, 0, 0), pipeline_mode=pl.Buffered(1)),
            _vec_spec(2 * d),
        ],
        out_specs=tok,
        out_shape=jax.ShapeDtypeStruct((n, t, d), F32),
        compiler_params=_params(("arbitrary", "arbitrary")),
        name="conv_pw1",
    )(x, mods_all, mods_all, w_bf, b_pw1.reshape(1, 2 * d))


def _conv_tail(cv, cg_ref, cb_ref, w2_ref, b2_ref, x, gate, lg_ref, lb_ref):
    hn = _layer_norm(cv, cg_ref[...], cb_ref[...])
    hn = hn * jax.nn.sigmoid(hn)
    y = jnp.dot(hn.astype(BF16), w2_ref[0], preferred_element_type=F32) + b2_ref[...]
    return _layer_norm(DEEPNORM_ALPHA * x + gate * y, lg_ref[...], lb_ref[...])


def _conv_lat_kernel(h_ref, wdw_ref, bdw_ref, cg_ref, cb_ref, w2_ref, b2_ref, x_ref, g_ref, lg_ref, lb_ref,
                     o_ref, hp_s, wb_s, cv_s, *, n_rows, wt):
    kw = wdw_ref.shape[0]
    pad = kw // 2
    d = h_ref.shape[-1]
    nsub = wt // SUBLANES
    zeros = jnp.zeros((pad, wt, d), F32)
    hp_s[0:pad] = zeros
    hp_s[pad + n_rows:pad + n_rows + pad] = zeros
    hp_s[pad:pad + n_rows] = h_ref[0]
    for k in range(kw):
        wb_s[k] = jnp.broadcast_to(wdw_ref[k:k + 1, :], (SUBLANES, d))

    bias = bdw_ref[...]

    def conv_row(r, carry):
        accs = [jnp.zeros((SUBLANES, d), F32) for _ in range(nsub)]
        for k in range(kw):
            w8 = wb_s[k]
            for q in range(nsub):
                accs[q] = accs[q] + w8 * hp_s[r + k, q * SUBLANES:(q + 1) * SUBLANES, :]
        for q in range(nsub):
            cv_s[pl.ds(pl.multiple_of(r * wt + q * SUBLANES, SUBLANES), SUBLANES), :] = accs[q] + bias
        return carry

    lax.fori_loop(0, n_rows, conv_row, 0)
    gate = g_ref[0, pl.ds(pl.program_id(0), 1), :]
    x = x_ref[0].reshape(n_rows * wt, d)
    out = _conv_tail(cv_s[...], cg_ref, cb_ref, w2_ref, b2_ref, x, gate, lg_ref, lb_ref)
    o_ref[0] = out.reshape(n_rows, wt, d)


def _conv_ctx_kernel(h_ref, wdw_ref, bdw_ref, cg_ref, cb_ref, w2_ref, b2_ref, x_ref, g_ref, lg_ref, lb_ref,
                     o_ref, hp_s, cv_s, *, n_tok, nb):
    kw = wdw_ref.shape[0]
    pad = kw // 2
    d = h_ref.shape[-1]
    lead = 2 * SUBLANES
    rblk = 128
    cblk = 2 * LANES
    hp_s[0:lead] = jnp.zeros((lead, d), F32)
    hp_s[lead + n_tok:lead + n_tok + lead] = jnp.zeros((lead, d), F32)
    hp_s[lead:lead + n_tok] = h_ref[0]
    span = ((kw - 1 + lead - pad) // SUBLANES) * SUBLANES

    def col_block(lc, carry):
        cols = pl.ds(pl.multiple_of(lc * cblk, cblk), cblk)
        for rc in range(n_tok // rblk):
            acc = jnp.zeros((rblk, cblk), F32)
            for q in range(SUBLANES):
                taps = [k for k in range(kw) if (k + lead - pad) % SUBLANES == q]
                if not taps:
                    continue
                shifted = hp_s[pl.ds(rc * rblk + q, rblk + span), cols]
                for k in taps:
                    o = k + lead - pad - q
                    acc = acc + wdw_ref[k:k + 1, cols] * shifted[o:o + rblk]
            cv_s[rc * rblk:(rc + 1) * rblk, cols] = acc + bdw_ref[:, cols]
        return carry

    lax.fori_loop(0, d // cblk, col_block, 0)
    gate = g_ref[0, nb:nb + 1, :]
    o_ref[0] = _conv_tail(cv_s[...], cg_ref, cb_ref, w2_ref, b2_ref, x_ref[0], gate, lg_ref, lb_ref)


def _conv_post(h, w_dw, b_dw, cv_g, cv_b, w2_bf, j, b2, x, mods_all, layer, ln_g, ln_b, n_rows, is_ctx, nb):
    n, t, d = x.shape
    kw = w_dw.shape[0]
    weights = [pl.BlockSpec((kw, d), lambda b, i: (0, 0)), _vec_spec(d), _vec_spec(d), _vec_spec(d),
               pl.BlockSpec((1, d, d), lambda b, i: (j, 0, 0), pipeline_mode=pl.Buffered(1)), _vec_spec(d)]
    tail = [_mod_spec(layer, d, 2), _vec_spec(d), _vec_spec(d)]
    args_w = (w_dw, b_dw.reshape(1, d), cv_g.reshape(1, d), cv_b.reshape(1, d), w2_bf, b2.reshape(1, d))
    args_t = (mods_all, ln_g.reshape(1, d), ln_b.reshape(1, d))
    if is_ctx:
        blk = pl.BlockSpec((1, t, d), lambda b, i: (b, 0, 0))
        return pl.pallas_call(
            functools.partial(_conv_ctx_kernel, n_tok=t, nb=nb),
            grid=(n, 1),
            in_specs=[blk] + weights + [blk] + tail,
            out_specs=blk,
            out_shape=jax.ShapeDtypeStruct((n, t, d), F32),
            scratch_shapes=[pltpu.VMEM((t + 4 * SUBLANES, d), F32), pltpu.VMEM((t, d), F32)],
            compiler_params=_params(("arbitrary", "arbitrary")),
            name="conv_post_ctx",
        )(h, *args_w, x, *args_t)
    wt = CONV_WT
    width = t // n_rows
    blk = pl.BlockSpec((1, n_rows, wt, d), lambda b, i: (b, 0, i, 0))
    out = pl.pallas_call(
        functools.partial(_conv_lat_kernel, n_rows=n_rows, wt=wt),
        grid=(n, width // wt),
        in_specs=[blk] + weights + [blk] + tail,
        out_specs=blk,
        out_shape=jax.ShapeDtypeStruct((n, n_rows, width, d), F32),
        scratch_shapes=[pltpu.VMEM((n_rows + 2 * (kw // 2), wt, d), F32),
                        pltpu.VMEM((kw, SUBLANES, d), F32),
                        pltpu.VMEM((n_rows * wt, d), F32)],
        compiler_params=_params(("arbitrary", "arbitrary")),
        name="conv_post",
    )(h.reshape(n, n_rows, width, d), *args_w, x.reshape(n, n_rows, width, d), *args_t)
    return out.reshape(n, t, d)


def _router_kernel(x_ref, sh_ref, sc_ref, wr_ref, u_ref, a_ref, *, is_ctx, nb):
    row = _mod_row(is_ctx, nb)
    u = x_ref[0] * (1.0 + sc_ref[0, pl.ds(row, 1), :]) + sh_ref[0, pl.ds(row, 1), :]
    u_ref[0] = u.astype(BF16)
    uh, ul = _split_bf16(u)
    wh, wl = _split_bf16(wr_ref[0])
    nt = (((1,), (1,)), ((), ()))
    logits = (lax.dot_general(wh, uh, nt, preferred_element_type=F32)
              + lax.dot_general(wh, ul, nt, preferred_element_type=F32)
              + lax.dot_general(wl, uh, nt, preferred_element_type=F32))
    m = jnp.max(logits, axis=0, keepdims=True)
    ex = jnp.exp(logits - m)
    a_ref[0] = ex / jnp.sum(ex, axis=0, keepdims=True)


def _router(x, mods_all, layer, w_router_t, is_ctx, nb):
    n, t, d = x.shape
    e = w_router_t.shape[1]
    tm = min(t, ROW_TILE)
    kern = functools.partial(_router_kernel, is_ctx=is_ctx, nb=nb)
    tok = pl.BlockSpec((1, tm, d), lambda b, i: (b, i, 0))
    return pl.pallas_call(
        kern,
        grid=(n, t // tm),
        in_specs=[tok, _mod_spec(layer, d, 3), _mod_spec(layer, d, 4),
                  pl.BlockSpec((1, e, d), lambda b, i: (layer, 0, 0))],
        out_specs=[tok, pl.BlockSpec((1, e, tm), lambda b, i: (b, 0, i))],
        out_shape=[jax.ShapeDtypeStruct((n, t, d), BF16), jax.ShapeDtypeStruct((n, e, t), F32)],
        compiler_params=_params(("arbitrary", "arbitrary")),
        name="router",
    )(x, mods_all, mods_all, w_router_t)


def _topk_kernel(a_ref, p_ref, pt_ref, *, cap):
    a = a_ref[0]
    e, t = a.shape
    capf = jnp.float32(cap)

    def count(mask):
        return jnp.sum(jnp.where(mask, 1.0, 0.0), axis=1, keepdims=True)

    def as_row_values(bits):
        return jnp.concatenate([pltpu.bitcast(bits, F32)] * (t // LANES), axis=1)

    thr_bits = jnp.zeros((e, LANES), jnp.int32)
    for bit in range(30, -1, -1):
        cand = thr_bits | jnp.int32(1 << bit)
        keep = count(a >= as_row_values(cand)) >= capf
        thr_bits = jnp.where(keep, cand, thr_bits)
    thr = as_row_values(thr_bits)
    gt = a > thr
    eq = a == thr
    need = capf - count(gt)
    tri = jnp.where(_iota((t, t), 0) <= _iota((t, t), 1), 1.0, 0.0).astype(BF16)
    eq_f = jnp.where(eq, 1.0, 0.0)
    eq_rank = jnp.dot(eq_f.astype(BF16), tri, preferred_element_type=F32) - eq_f
    sel = jnp.where(gt, 1.0, jnp.where(eq & (eq_rank < need), 1.0, 0.0))
    slot = jnp.dot(sel.astype(BF16), tri, preferred_element_type=F32) - 1.0
    pos = jnp.where(sel > 0.0, slot, -1.0)
    p_ref[0] = pos.astype(jnp.int32)
    padded = jnp.concatenate([pos, jnp.full((LANES - e, t), -1.0, F32)], axis=0)
    pt_ref[0] = padded.T.astype(jnp.int32)


def _topk(aff_t, cap):
    n, e, t = aff_t.shape
    kern = functools.partial(_topk_kernel, cap=cap)
    return pl.pallas_call(
        kern,
        grid=(n,),
        in_specs=[pl.BlockSpec((1, e, t), lambda b: (b, 0, 0))],
        out_specs=[pl.BlockSpec((1, e, t), lambda b: (b, 0, 0)),
                   pl.BlockSpec((1, t, LANES), lambda b: (b, 0, 0))],
        out_shape=[jax.ShapeDtypeStruct((n, e, t), jnp.int32), jax.ShapeDtypeStruct((n, t, LANES), jnp.int32)],
        compiler_params=_params(("arbitrary",)),
        name="topk",
    )(aff_t)


def _gather_kernel(u_ref, p_ref, a_ref, xs_ref, gs_ref, *, cap):
    e = pl.program_id(1)
    pos = p_ref[0, pl.ds(e, 1), :]
    t = pos.shape[1]
    hit = _iota((cap, t), 0) == pos
    onehot = jnp.where(hit, 1.0, 0.0).astype(BF16)
    xs_ref[0] = jnp.dot(onehot, u_ref[0], preferred_element_type=F32).astype(BF16)
    gate = jnp.sum(jnp.where(hit, a_ref[0, pl.ds(e, 1), :], 0.0), axis=1, keepdims=True)
    gs_ref[0] = jnp.broadcast_to(gate, (cap, LANES))


def _gather(u2, pos_t, aff_t, cap):
    n, t, d = u2.shape
    e = pos_t.shape[1]
    return pl.pallas_call(
        functools.partial(_gather_kernel, cap=cap),
        grid=(n, e),
        in_specs=[pl.BlockSpec((1, t, d), lambda b, ei: (b, 0, 0)),
                  pl.BlockSpec((1, e, t), lambda b, ei: (b, 0, 0)),
                  pl.BlockSpec((1, e, t), lambda b, ei: (b, 0, 0))],
        out_specs=[pl.BlockSpec((1, cap, d), lambda b, ei: (ei, b, 0)),
                   pl.BlockSpec((1, cap, LANES), lambda b, ei: (ei, b, 0))],
        out_shape=[jax.ShapeDtypeStruct((e, n * cap, d), BF16),
                   jax.ShapeDtypeStruct((e, n * cap, LANES), F32)],
        compiler_params=_params(("arbitrary", "arbitrary")),
        name="gather",
    )(u2, pos_t, aff_t)


def _ffn_kernel(*refs, n_streams):
    xs_refs = refs[0:2 * n_streams:2]
    gs_refs = refs[1:2 * n_streams:2]
    wg_ref, wu_ref, wo_ref = refs[2 * n_streams:2 * n_streams + 3]
    y_refs = refs[2 * n_streams + 3:3 * n_streams + 3]
    act_s, wo_s = refs[3 * n_streams + 3:3 * n_streams + 5]
    fc = pl.program_id(1)
    nfc = act_s.shape[0]
    bounds = [0]
    for r in xs_refs:
        bounds.append(bounds[-1] + r.shape[1])
    if n_streams == 1:
        x = xs_refs[0][0]
    else:
        x_s = refs[-1]

        @pl.when(fc == 0)
        def _():
            for k, r in enumerate(xs_refs):
                x_s[bounds[k]:bounds[k + 1], :] = r[0]

        x = x_s[...]
    g = jnp.dot(x, wg_ref[0, 0].astype(BF16), preferred_element_type=F32)
    up = jnp.dot(x, wu_ref[0, 0].astype(BF16), preferred_element_type=F32)
    act_s[fc] = ((g * jax.nn.sigmoid(g)) * up).astype(BF16)
    wo_s[fc] = wo_ref[0, 0].astype(BF16)

    @pl.when(fc == nfc - 1)
    def _():
        act = jnp.concatenate([act_s[k] for k in range(nfc)], axis=1)
        w_out = wo_s[...].reshape(nfc * wo_s.shape[1], wo_s.shape[2])
        y = jnp.dot(act, w_out, preferred_element_type=F32)
        for k in range(n_streams):
            y_refs[k][0] = (y[bounds[k]:bounds[k + 1]] * gs_refs[k][0][:, 0:1]).astype(BF16)


def _ffn(streams, w_in, w_out, layer):
    e, _, d = streams[0][0].shape
    ff = w_out.shape[2]
    nfc = ff // FF_TILE
    n_streams = len(streams)
    rows = [xs.shape[1] for xs, _ in streams]
    in_specs, args = [], []
    for (xs, gs), r in zip(streams, rows):
        in_specs += [pl.BlockSpec((1, r, d), lambda ei, fc: (ei, 0, 0), pipeline_mode=pl.Buffered(1)),
                     pl.BlockSpec((1, r, LANES), lambda ei, fc: (ei, 0, 0))]
        args += [xs, gs]
    in_specs += [pl.BlockSpec((1, 1, d, FF_TILE), lambda ei, fc: (layer, ei, 0, fc)),
                 pl.BlockSpec((1, 1, d, FF_TILE), lambda ei, fc: (layer, ei, 0, nfc + fc)),
                 pl.BlockSpec((1, 1, FF_TILE, d), lambda ei, fc: (layer, ei, fc, 0))]
    scratch = [pltpu.VMEM((nfc, sum(rows), FF_TILE), BF16), pltpu.VMEM((nfc, FF_TILE, d), BF16)]
    if n_streams > 1:
        scratch.append(pltpu.VMEM((sum(rows), d), BF16))
    return pl.pallas_call(
        functools.partial(_ffn_kernel, n_streams=n_streams),
        grid=(e, nfc),
        in_specs=in_specs,
        out_specs=[pl.BlockSpec((1, r, d), lambda ei, fc: (ei, 0, 0)) for r in rows],
        out_shape=[jax.ShapeDtypeStruct((e, r, d), BF16) for r in rows],
        scratch_shapes=scratch,
        compiler_params=_params(("arbitrary", "arbitrary")),
        name="expert_ffn",
    )(*args, w_in, w_in, w_out)


def _combine_kernel(y_ref, pt_ref, x_ref, g_ref, lg_ref, lb_ref, o_ref, oh_s, *, is_ctx, nb, cap):
    n_exp, _, d = y_ref.shape
    ec = n_exp * cap
    cw = min(ec, COMBINE_COLS)
    pos = pt_ref[0].astype(F32).astype(BF16)
    for c0 in range(0, ec, cw):
        src = _iota((LANES, cw), 0)
        dst = _iota((LANES, cw), 1) + c0
        expand = jnp.where((dst >= src * cap) & (dst < (src + 1) * cap), 1.0, 0.0).astype(BF16)
        pe = jnp.dot(pos, expand, preferred_element_type=F32)
        slot = (_iota((1, cw), 1) + c0) % cap
        oh_s[:, c0:c0 + cw] = jnp.where(pe == slot.astype(F32), 1.0, 0.0).astype(BF16)
    f = jnp.dot(oh_s[...], y_ref[...].reshape(ec, d), preferred_element_type=F32)
    gate = g_ref[0, pl.ds(_mod_row(is_ctx, nb), 1), :]
    o_ref[0] = _layer_norm(DEEPNORM_ALPHA * x_ref[0] + gate * f, lg_ref[...], lb_ref[...])


def _combine(y, pos_tm, x, mods_all, layer, ln_g, ln_b, cap, is_ctx, nb):
    n, t, d = x.shape
    e = y.shape[0]
    tm = min(t, ROW_TILE)
    kern = functools.partial(_combine_kernel, is_ctx=is_ctx, nb=nb, cap=cap)
    tok = pl.BlockSpec((1, tm, d), lambda b, i: (b, i, 0))
    return pl.pallas_call(
        kern,
        grid=(n, t // tm),
        in_specs=[
            pl.BlockSpec((e, cap, d), lambda b, i: (0, b, 0), pipeline_mode=pl.Buffered(1)),
            pl.BlockSpec((1, tm, LANES), lambda b, i: (b, i, 0)),
            tok,
            _mod_spec(layer, d, 5),
            _vec_spec(d), _vec_spec(d),
        ],
        out_specs=tok,
        out_shape=jax.ShapeDtypeStruct((n, t, d), F32),
        scratch_shapes=[pltpu.VMEM((tm, e * cap), BF16)],
        compiler_params=_params(("arbitrary", "arbitrary")),
        name="combine",
    )(y, pos_tm, x, mods_all, ln_g.reshape(1, d), ln_b.reshape(1, d))


def _moe_post(x1_lat, x1_ctx, mods_all, layer, w_router_t, w_in, w_out, ln_g, ln_b, nb):
    streams = [(x1_lat, False)] + ([(x1_ctx, True)] if x1_ctx is not None else [])
    caps = [EC_CAPACITY_FACTOR * s.shape[1] // N_EXPERTS for s, _ in streams]
    routed, gathered = [], []
    for (s, is_ctx), cap in zip(streams, caps):
        u2, aff_t = _router(s, mods_all, layer, w_router_t, is_ctx, nb)
        pos_t, pos_tm = _topk(aff_t, cap)
        gathered.append(_gather(u2, pos_t, aff_t, cap))
        routed.append(pos_tm)
    ys = _ffn(gathered, w_in, w_out, layer)
    outs = [_combine(y, pos_tm, s, mods_all, layer, ln_g, ln_b, cap, is_ctx, nb)
            for (s, is_ctx), pos_tm, cap, y in zip(streams, routed, caps, ys)]
    return outs[0], (outs[1] if len(outs) > 1 else None)


def kernel(x, c, ctx, c_ctx, ada_w, ada_b, ln_g, ln_b, s5_a_re, s5_a_im, s5_log_dt, s5_b_re, s5_b_im,
           s5_c_re, s5_c_im, s5_d, s5_w_glu, s5_b_glu, cv_w_pw1, cv_b_pw1, cv_w_dw, cv_b_dw, cv_ln_g,
           cv_ln_b, cv_w_pw2, cv_b_pw2, moe_w_router, moe_w_in, moe_w_out):
    nb, seq, d = x.shape
    assert nb + 1 <= MOD_ROWS and nb % 2 == 0 and d % LANES == 0
    rows = seq // GRID_W

    c8 = jnp.concatenate([c.astype(F32), c_ctx.astype(F32)[None], jnp.zeros((MOD_ROWS - nb - 1, d), F32)], axis=0)
    mods_all = _ada_all(c8, ada_w, ada_b)
    w_glu_bf = s5_w_glu.astype(BF16)
    w_pw1_bf = cv_w_pw1.astype(BF16)
    w_pw2_bf = cv_w_pw2.astype(BF16)
    w_router_t = jnp.swapaxes(moe_w_router, 1, 2)

    x_lat, x_ctx = x, ctx
    for i in range(DEPTH):
        is_s5 = (i % N_MIXERS) == 0
        j = i // N_MIXERS
        ctx_out = any((k % N_MIXERS) == 0 for k in range(i + 1, DEPTH))
        x1_ctx = None

        if is_s5:
            ops = _s5_operators(s5_a_re[j], s5_a_im[j], s5_log_dt[j], s5_b_re[j], s5_b_im[j],
                                s5_c_re[j], s5_c_im[j])
            z_lat, z_ctx = _s5_core(x_lat, x_ctx, mods_all, i, s5_d[j], ops, ctx_out)
            x1_lat = _s5_out(z_lat, w_glu_bf, j, s5_b_glu[j], x_lat, mods_all, i, ln_g[i, 0], ln_b[i, 0], False, nb)
            if ctx_out:
                x1_ctx = _s5_out(z_ctx, w_glu_bf, j, s5_b_glu[j], x_ctx, mods_all, i, ln_g[i, 0], ln_b[i, 0],
                                 True, nb)
        else:
            conv_args = (cv_w_dw[j], cv_b_dw[j], cv_ln_g[j], cv_ln_b[j], w_pw2_bf, j, cv_b_pw2[j])
            h_lat = _pw1(x_lat, mods_all, i, w_pw1_bf, j, cv_b_pw1[j], False, nb)
            x1_lat = _conv_post(h_lat, *conv_args, x_lat, mods_all, i, ln_g[i, 0], ln_b[i, 0], rows, False, nb)
            if ctx_out:
                h_ctx = _pw1(x_ctx, mods_all, i, w_pw1_bf, j, cv_b_pw1[j], True, nb)
                x1_ctx = _conv_post(h_ctx, *conv_args, x_ctx, mods_all, i, ln_g[i, 0], ln_b[i, 0], None, True, nb)

        x_lat, x_ctx_new = _moe_post(x1_lat, x1_ctx, mods_all, i, w_router_t, moe_w_in, moe_w_out,
                                     ln_g[i, 1], ln_b[i, 1], nb)
        if ctx_out:
            x_ctx = x_ctx_new
    return x_lat
```

```python
import functools

import jax
import jax.numpy as jnp
from jax import lax
from jax.experimental import pallas as pl
from jax.experimental.pallas import tpu as pltpu

F32 = jnp.float32
BF16 = jnp.bfloat16

DEPTH = 4
N_MIXERS = 2
GRID_W = 64
S5_H = 16
S5_P = 64
N_EXPERTS = 16
EC_CAPACITY_FACTOR = 2
DEEPNORM_ALPHA = (2.0 * DEPTH) ** 0.25
LN_EPS = 1e-5

LANES = 128
SUBLANES = 8
S5_CHUNK = 8
S5_GPT = LANES // S5_H
S5_STATE_COLS = S5_GPT * S5_P
MOD_ROWS = 8
VMEM_LIMIT = 56 * 1024 * 1024
ROW_TILE = 512
COMBINE_COLS = 1024
TOK_SPLIT_BITS = 6
FF_TILE = 256
CONV_WT = 2 * SUBLANES


def _params(sem, vmem=VMEM_LIMIT):
    return pltpu.CompilerParams(dimension_semantics=sem, vmem_limit_bytes=vmem)


def _layer_norm(v, g, b):
    mu = jnp.mean(v, axis=-1, keepdims=True)
    c = v - mu
    var = jnp.mean(c * c, axis=-1, keepdims=True)
    return c * lax.rsqrt(var + LN_EPS) * g + b


def _split_bf16(v):
    hi = v.astype(BF16)
    lo = (v - hi.astype(F32)).astype(BF16)
    return hi, lo


def _mod_spec(layer, d, k):
    return pl.BlockSpec((1, MOD_ROWS, d), lambda *_: (layer, 0, k))


def _vec_spec(d):
    return pl.BlockSpec((1, d), lambda *_: (0, 0))


def _mod_row(is_ctx, nb):
    return nb if is_ctx else pl.program_id(0)


def _ada_kernel(c_ref, w_ref, b_ref, o_ref):
    c = c_ref[...]
    cond = c * jax.nn.sigmoid(c)
    hi, lo = _split_bf16(cond)
    lhs = jnp.concatenate([hi, lo], axis=0)
    r = jnp.dot(lhs, w_ref[0].astype(BF16), preferred_element_type=F32)
    o_ref[0] = r[:MOD_ROWS] + r[MOD_ROWS:] + b_ref[0]


def _ada_all(c8, ada_w, ada_b):
    depth, d, n = ada_w.shape
    tn = 1024
    return pl.pallas_call(
        _ada_kernel,
        grid=(depth, n // tn),
        in_specs=[
            pl.BlockSpec((MOD_ROWS, d), lambda i, k: (0, 0)),
            pl.BlockSpec((1, d, tn), lambda i, k: (i, 0, k)),
            pl.BlockSpec((1, 1, tn), lambda i, k: (i, 0, k)),
        ],
        out_specs=pl.BlockSpec((1, MOD_ROWS, tn), lambda i, k: (i, 0, k)),
        out_shape=jax.ShapeDtypeStruct((depth, MOD_ROWS, n), F32),
        compiler_params=_params(("arbitrary", "arbitrary")),
        name="adaln",
    )(c8, ada_w, ada_b.reshape(depth, 1, n))


def _cmul(ar, ai, br, bi):
    return ar * br - ai * bi, ar * bi + ai * br


def _s5_operators(a_re, a_im, log_dt, b_re, b_im, c_re, c_im):
    hp = lax.Precision.HIGHEST
    t = S5_CHUNK
    g = a_re.shape[1]
    nj = g // S5_GPT
    lam_r, lam_i = a_re.astype(F32), a_im.astype(F32)
    dt = jnp.exp(log_dt.astype(F32))[..., None]
    mag = jnp.exp(lam_r * dt)
    abar_r, abar_i = mag * jnp.cos(lam_i * dt), mag * jnp.sin(lam_i * dt)
    den = lam_r * lam_r + lam_i * lam_i
    xr, xi = abar_r - 1.0, abar_i
    coef_r, coef_i = (xr * lam_r + xi * lam_i) / den, (xi * lam_r - xr * lam_i) / den
    bb_r, bb_i = _cmul(coef_r[..., None], coef_i[..., None], b_re.astype(F32), b_im.astype(F32))
    cm_r, cm_i = c_re.astype(F32), c_im.astype(F32)
    pr, pi = [jnp.ones_like(abar_r)], [jnp.zeros_like(abar_r)]
    for _ in range(t):
        nr, ni = _cmul(pr[-1], pi[-1], abar_r, abar_i)
        pr.append(nr)
        pi.append(ni)
    pr, pi = jnp.stack(pr), jnp.stack(pi)

    def lag_kernels(d):
        qr, qi = _cmul(pr[:t, d, :, :, None], pi[:t, d, :, :, None], bb_r[d][None], bb_i[d][None])
        return (jnp.einsum('ghp,kgpj->kghj', cm_r[d], qr, precision=hp)
                - jnp.einsum('ghp,kgpj->kghj', cm_i[d], qi, precision=hp))

    kf, kb = lag_kernels(0), lag_kernels(1)
    kall = jnp.concatenate([kb[:0:-1], (kf[0] + kb[0])[None], kf[1:]], axis=0)
    nlag = 2 * t - 1
    kl = kall.reshape(nlag, nj, S5_GPT, S5_H, S5_H).transpose(1, 0, 2, 4, 3).reshape(nj, nlag * LANES, S5_H)

    def per_tile(re, im, perm, rows, cols):
        both = jnp.stack([re, im], axis=1)
        both = both.reshape(2, 2, nj, S5_GPT, both.shape[-2], both.shape[-1])
        return both.transpose(perm).reshape(nj, rows, cols)

    bt = per_tile(bb_r, bb_i, (2, 0, 1, 3, 5, 4), 4 * LANES, S5_P)
    ct = per_tile(cm_r, cm_i, (2, 0, 1, 3, 5, 4), 4 * S5_STATE_COLS, S5_H)

    exps_b = (jnp.arange(t - 1, -1, -1), jnp.arange(t))
    exps_c = (jnp.arange(1, t + 1), jnp.arange(t, 0, -1))

    def powers(exps):
        both = jnp.stack([jnp.stack([pr[exps[d], d], pi[exps[d], d]]) for d in range(2)])
        return both.reshape(2, 2, t, nj, S5_STATE_COLS)

    pw = powers(exps_b).transpose(3, 0, 1, 2, 4).reshape(nj, 4 * t, S5_STATE_COLS)
    pc = powers(exps_c).transpose(3, 0, 1, 4, 2).reshape(nj, 4 * S5_STATE_COLS, t)

    def decay_tiles(d):
        ar = pr[t, d].reshape(nj, S5_STATE_COLS // LANES, LANES)
        ai = pi[t, d].reshape(nj, S5_STATE_COLS // LANES, LANES)
        return jnp.concatenate([ar, ar, -ai, ai], axis=1)

    at = jnp.concatenate([decay_tiles(0), decay_tiles(1)], axis=1)
    return kl, bt, ct, pw, pc, at


def _iota(shape, axis):
    return lax.broadcasted_iota(jnp.int32, shape, axis)


def _group_blockdiag(comp, width, rows_per_group):
    k = comp.shape[1]
    n = S5_GPT * k
    rep = jnp.where((_iota((k, n), 1) & (k - 1)) == _iota((k, n), 0), 1.0, 0.0).astype(BF16)
    hi = comp.astype(BF16)
    rest = comp - hi.astype(F32)
    mid = rest.astype(BF16)
    lo = (rest - mid.astype(F32)).astype(BF16)
    tiled = (jnp.dot(hi, rep, preferred_element_type=F32) + jnp.dot(mid, rep, preferred_element_type=F32)
             + jnp.dot(lo, rep, preferred_element_type=F32))
    row_grp = (_iota((comp.shape[0], 1), 0) >> (rows_per_group.bit_length() - 1)) & (S5_GPT - 1)
    col_grp = _iota((1, n), 1) >> (width.bit_length() - 1)
    return jnp.where(row_grp == col_grp, tiled, 0.0)


def _build_s5_operators(kl_ref, bt_ref, ct_ref, pw_ref, pc_ref, wm_s, wb_s, wc_s):
    t = S5_CHUNK
    sc = S5_STATE_COLS
    lagk = _group_blockdiag(kl_ref[0], S5_H, S5_H).astype(BF16)
    for s in range(t):
        for u in range(t):
            lag = u - s + t - 1
            wm_s[s * LANES:(s + 1) * LANES, u * LANES:(u + 1) * LANES] = lagk[lag * LANES:(lag + 1) * LANES, :]
    bexp = _group_blockdiag(bt_ref[0], S5_P, S5_H)
    for d in range(2):
        b_re = bexp[(2 * d) * LANES:(2 * d + 1) * LANES]
        b_im = bexp[(2 * d + 1) * LANES:(2 * d + 2) * LANES]
        for s in range(t):
            p_re = pw_ref[0, (2 * d) * t + s:(2 * d) * t + s + 1, :]
            p_im = pw_ref[0, (2 * d + 1) * t + s:(2 * d + 1) * t + s + 1, :]
            rows = slice(s * LANES, (s + 1) * LANES)
            wb_s[rows, d * 2 * sc:d * 2 * sc + sc] = (b_re * p_re - b_im * p_im).astype(BF16)
            wb_s[rows, d * 2 * sc + sc:(d + 1) * 2 * sc] = (b_re * p_im + b_im * p_re).astype(BF16)
    cexp = _group_blockdiag(ct_ref[0], S5_H, S5_P)
    for d in range(2):
        c_re = cexp[(2 * d) * sc:(2 * d + 1) * sc]
        c_im = cexp[(2 * d + 1) * sc:(2 * d + 2) * sc]
        for u in range(t):
            p_re = pc_ref[0, (2 * d) * sc:(2 * d + 1) * sc, u:u + 1]
            p_im = pc_ref[0, (2 * d + 1) * sc:(2 * d + 2) * sc, u:u + 1]
            cols = slice(u * LANES, (u + 1) * LANES)
            wc_s[d * 2 * sc:d * 2 * sc + sc, cols] = (c_re * p_re - c_im * p_im).astype(BF16)
            wc_s[d * 2 * sc + sc:(d + 1) * 2 * sc, cols] = (-(c_re * p_im + c_im * p_re)).astype(BF16)


def _s5_kernel(x_ref, xc_ref, sh_ref, sc_ref, d_ref, kl_ref, bt_ref, ct_ref, pw_ref, pc_ref, at_ref, *rest,
               nb_half, n_lat, n_ctx, ctx_out):
    if ctx_out:
        z_ref, zc_ref, wm_s, wb_s, wc_s, xf_s, st_s, en_s, y_s, zb_s, zcb_s = rest
    else:
        z_ref, wm_s, wb_s, wc_s, xf_s, st_s, en_s, y_s, zb_s = rest
        zc_ref = zcb_s = None
    t = S5_CHUNK
    ncl = n_lat // t
    ncc = n_ctx // t
    lat_rows = nb_half * ncl
    half = pl.program_id(1)
    sc_cols = S5_STATE_COLS
    nst = sc_cols // LANES
    n_rows = nb_half * (ncl + ncc)

    @pl.when(half == 0)
    def _():
        _build_s5_operators(kl_ref, bt_ref, ct_ref, pw_ref, pc_ref, wm_s, wb_s, wc_s)

    for k in range(nb_half):
        b = half * nb_half + k
        scale = 1.0 + sc_ref[0, pl.ds(b, 1), :]
        shift = sh_ref[0, pl.ds(b, 1), :]
        for s in range(t):
            xf_s[k * ncl:(k + 1) * ncl, s * LANES:(s + 1) * LANES] = (
                x_ref[pl.ds(k * n_lat + s, ncl, stride=t), :] * scale + shift)
    scale_c = 1.0 + sc_ref[0, nb_half * 2:nb_half * 2 + 1, :]
    shift_c = sh_ref[0, nb_half * 2:nb_half * 2 + 1, :]
    for k in range(nb_half):
        for s in range(t):
            r0 = lat_rows + k * ncc
            xf_s[r0:r0 + ncc, s * LANES:(s + 1) * LANES] = (
                xc_ref[pl.ds(k * n_ctx + s, ncc, stride=t), :] * scale_c + shift_c)

    xb = xf_s[...].astype(BF16)
    dvec = jnp.concatenate([d_ref[...]] * t, axis=1)
    y_s[...] = jnp.dot(xb, wm_s[...], preferred_element_type=F32) + xf_s[...] * dvec

    for d in range(2):
        local = jnp.dot(xb, wb_s[:, d * 2 * sc_cols:(d + 1) * 2 * sc_cols], preferred_element_type=F32)
        for q in range(2 * nst):
            st_s[d, pl.ds(q, n_rows, stride=2 * nst), :] = local[:, q * LANES:(q + 1) * LANES]
    coef = [(at_ref[0, (2 * d) * 2 * nst:(2 * d + 1) * 2 * nst, :],
             at_ref[0, (2 * d + 1) * 2 * nst:(2 * d + 2) * 2 * nst, :]) for d in range(2)]

    def step(chunk_rows, h):
        new = []
        for d in range(2):
            for k in range(nb_half):
                tile = pl.ds(pl.multiple_of((chunk_rows[d] + k * chunk_rows[2]) * 2 * nst, 2 * nst), 2 * nst)
                prev, prev_sw = h[2 * (d * nb_half + k)], h[2 * (d * nb_half + k) + 1]
                local_state = st_s[d, tile, :]
                en_s[d, tile, :] = prev
                new.append(coef[d][0] * prev + coef[d][1] * prev_sw + local_state)
                new.append(coef[d][0] * prev_sw - coef[d][1] * prev + pltpu.roll(local_state, nst, axis=0))
        return tuple(new)

    zero = tuple(jnp.zeros((2 * nst, LANES), F32) for _ in range(4 * nb_half))
    h = lax.fori_loop(0, ncc, lambda i, h: step((lat_rows + i, lat_rows + ncc - 1 - i, ncc), h), zero, unroll=4)
    lax.fori_loop(0, ncl, lambda i, h: step((i, ncl - 1 - i, ncl), h), h, unroll=4)

    entering = jnp.concatenate([en_s[d, pl.ds(q, n_rows, stride=2 * nst), :]
                                for d in range(2) for q in range(2 * nst)], axis=1)
    y_s[...] += jnp.dot(entering.astype(BF16), wc_s[...], preferred_element_type=F32)

    z = jax.nn.gelu(y_s[...])
    for k in range(nb_half):
        for s in range(t):
            zb_s[pl.ds(k * n_lat + s, ncl, stride=t), :] = z[k * ncl:(k + 1) * ncl, s * LANES:(s + 1) * LANES]
    z_ref[...] = zb_s[...].astype(BF16)
    if ctx_out:
        for k in range(nb_half):
            for s in range(t):
                r0 = lat_rows + k * ncc
                zcb_s[pl.ds(k * n_ctx + s, ncc, stride=t), :] = z[r0:r0 + ncc, s * LANES:(s + 1) * LANES]
        zc_ref[...] = zcb_s[...].astype(BF16)


def _s5_core(x, xc, mods_all, layer, d_skip, ops, ctx_out):
    nb, n_lat, d = x.shape
    n_ctx = xc.shape[1]
    nj = d // LANES
    nb_half = nb // 2
    t = S5_CHUNK
    tl = t * LANES
    rows = nb_half * (n_lat + n_ctx) // t
    kern = functools.partial(_s5_kernel, nb_half=nb_half, n_lat=n_lat, n_ctx=n_ctx, ctx_out=ctx_out)
    out_shape = [jax.ShapeDtypeStruct((nb * n_lat, d), BF16)]
    out_specs = [pl.BlockSpec((nb_half * n_lat, LANES), lambda j, h: (h, j))]
    scratch = [pltpu.VMEM((tl, tl), BF16), pltpu.VMEM((tl, 4 * S5_STATE_COLS), BF16),
               pltpu.VMEM((4 * S5_STATE_COLS, tl), BF16),
               pltpu.VMEM((rows, tl), F32), pltpu.VMEM((2, rows * 2 * S5_STATE_COLS // LANES, LANES), F32),
               pltpu.VMEM((2, rows * 2 * S5_STATE_COLS // LANES, LANES), F32),
               pltpu.VMEM((rows, tl), F32), pltpu.VMEM((nb_half * n_lat, LANES), F32)]
    if ctx_out:
        out_shape.append(jax.ShapeDtypeStruct((nb * n_ctx, d), BF16))
        out_specs.append(pl.BlockSpec((nb_half * n_ctx, LANES), lambda j, h: (h, j)))
        scratch.append(pltpu.VMEM((nb_half * n_ctx, LANES), F32))
    res = pl.pallas_call(
        kern,
        grid=(nj, 2),
        in_specs=[
            pl.BlockSpec((nb_half * n_lat, LANES), lambda j, h: (h, j)),
            pl.BlockSpec((nb_half * n_ctx, LANES), lambda j, h: (h, j)),
            pl.BlockSpec((1, MOD_ROWS, LANES), lambda j, h: (layer, 0, j)),
            pl.BlockSpec((1, MOD_ROWS, LANES), lambda j, h: (layer, 0, nj + j)),
            pl.BlockSpec((1, LANES), lambda j, h: (0, j)),
        ] + [pl.BlockSpec((1,) + op.shape[1:], lambda j, h: (j, 0, 0)) for op in ops],
        out_specs=out_specs,
        out_shape=out_shape,
        scratch_shapes=scratch,
        compiler_params=_params(("arbitrary", "arbitrary")),
        name="s5_core",
    )(x.reshape(nb * n_lat, d), xc.reshape(nb * n_ctx, d), mods_all, mods_all, d_skip.reshape(1, d), *ops)
    z = res[0].reshape(nb, n_lat, d)
    zc = res[1].reshape(nb, n_ctx, d) if ctx_out else None
    return z, zc


def _s5_out_kernel(z_ref, w_ref, b_ref, x_ref, g_ref, lg_ref, lb_ref, o_ref, *, is_ctx, nb):
    d = x_ref.shape[-1]
    acc = jnp.dot(z_ref[0], w_ref[0], preferred_element_type=F32) + b_ref[...]
    y = acc[:, :d] * jax.nn.sigmoid(acc[:, d:])
    gate = g_ref[0, pl.ds(_mod_row(is_ctx, nb), 1), :]
    o_ref[0] = _layer_norm(DEEPNORM_ALPHA * x_ref[0] + gate * y, lg_ref[...], lb_ref[...])


def _s5_out(z, w_bf, j, b_glu, x, mods_all, layer, ln_g, ln_b, is_ctx, nb):
    n, t, d = x.shape
    tm = min(t, ROW_TILE)
    kern = functools.partial(_s5_out_kernel, is_ctx=is_ctx, nb=nb)
    tok = pl.BlockSpec((1, tm, d), lambda b, i: (b, i, 0))
    return pl.pallas_call(
        kern,
        grid=(n, t // tm),
        in_specs=[
            tok,
            pl.BlockSpec((1, d, 2 * d), lambda b, i: (j, 0, 0), pipeline_mode=pl.Buffered(1)),
            _vec_spec(2 * d),
            tok,
            _mod_spec(layer, d, 2),
            _vec_spec(d), _vec_spec(d),
        ],
        out_specs=tok,
        out_shape=jax.ShapeDtypeStruct((n, t, d), F32),
        compiler_params=_params(("arbitrary", "arbitrary")),
        name="s5_out",
    )(z, w_bf, b_glu.reshape(1, 2 * d), x, mods_all, ln_g.reshape(1, d), ln_b.reshape(1, d))


def _pw1_kernel(x_ref, sh_ref, sc_ref, w_ref, b_ref, o_ref, *, is_ctx, nb):
    d = x_ref.shape[-1]
    row = _mod_row(is_ctx, nb)
    u = x_ref[0] * (1.0 + sc_ref[0, pl.ds(row, 1), :]) + sh_ref[0, pl.ds(row, 1), :]
    acc = jnp.dot(u.astype(BF16), w_ref[0], preferred_element_type=F32) + b_ref[...]
    o_ref[0] = acc[:, :d] * jax.nn.sigmoid(acc[:, d:])


def _pw1(x, mods_all, layer, w_bf, j, b_pw1, is_ctx, nb):
    n, t, d = x.shape
    tm = min(t, ROW_TILE)
    kern = functools.partial(_pw1_kernel, is_ctx=is_ctx, nb=nb)
    tok = pl.BlockSpec((1, tm, d), lambda b, i: (b, i, 0))
    return pl.pallas_call(
        kern,
        grid=(n, t // tm),
        in_specs=[
            tok,
            _mod_spec(layer, d, 0), _mod_spec(layer, d, 1),
            pl.BlockSpec((1, d, 2 * d), lambda b, i: (j, 0, 0), pipeline_mode=pl.Buffered(1)),
            _vec_spec(2 * d),
        ],
        out_specs=tok,
        out_shape=jax.ShapeDtypeStruct((n, t, d), F32),
        compiler_params=_params(("arbitrary", "arbitrary")),
        name="conv_pw1",
    )(x, mods_all, mods_all, w_bf, b_pw1.reshape(1, 2 * d))


def _conv_tail(cv, cg_ref, cb_ref, w2_ref, b2_ref, x, gate, lg_ref, lb_ref):
    hn = _layer_norm(cv, cg_ref[...], cb_ref[...])
    hn = hn * jax.nn.sigmoid(hn)
    y = jnp.dot(hn.astype(BF16), w2_ref[0], preferred_element_type=F32) + b2_ref[...]
    return _layer_norm(DEEPNORM_ALPHA * x + gate * y, lg_ref[...], lb_ref[...])


def _conv_lat_kernel(h_ref, wdw_ref, bdw_ref, cg_ref, cb_ref, w2_ref, b2_ref, x_ref, g_ref, lg_ref, lb_ref,
                     o_ref, hp_s, wb_s, cv_s, *, n_rows, wt):
    kw = wdw_ref.shape[0]
    pad = kw // 2
    d = h_ref.shape[-1]
    nsub = wt // SUBLANES
    zeros = jnp.zeros((pad, wt, d), F32)
    hp_s[0:pad] = zeros
    hp_s[pad + n_rows:pad + n_rows + pad] = zeros
    hp_s[pad:pad + n_rows] = h_ref[0]
    for k in range(kw):
        wb_s[k] = jnp.broadcast_to(wdw_ref[k:k + 1, :], (SUBLANES, d))

    bias = bdw_ref[...]

    def conv_row(r, carry):
        accs = [jnp.zeros((SUBLANES, d), F32) for _ in range(nsub)]
        for k in range(kw):
            w8 = wb_s[k]
            for q in range(nsub):
                accs[q] = accs[q] + w8 * hp_s[r + k, q * SUBLANES:(q + 1) * SUBLANES, :]
        for q in range(nsub):
            cv_s[pl.ds(pl.multiple_of(r * wt + q * SUBLANES, SUBLANES), SUBLANES), :] = accs[q] + bias
        return carry

    lax.fori_loop(0, n_rows, conv_row, 0)
    gate = g_ref[0, pl.ds(pl.program_id(0), 1), :]
    x = x_ref[0].reshape(n_rows * wt, d)
    out = _conv_tail(cv_s[...], cg_ref, cb_ref, w2_ref, b2_ref, x, gate, lg_ref, lb_ref)
    o_ref[0] = out.reshape(n_rows, wt, d)


def _conv_ctx_kernel(h_ref, wdw_ref, bdw_ref, cg_ref, cb_ref, w2_ref, b2_ref, x_ref, g_ref, lg_ref, lb_ref,
                     o_ref, hp_s, cv_s, *, n_tok, nb):
    kw = wdw_ref.shape[0]
    pad = kw // 2
    d = h_ref.shape[-1]
    lead = 2 * SUBLANES
    rblk = 128
    cblk = 2 * LANES
    hp_s[0:lead] = jnp.zeros((lead, d), F32)
    hp_s[lead + n_tok:lead + n_tok + lead] = jnp.zeros((lead, d), F32)
    hp_s[lead:lead + n_tok] = h_ref[0]
    span = ((kw - 1 + lead - pad) // SUBLANES) * SUBLANES

    def col_block(lc, carry):
        cols = pl.ds(pl.multiple_of(lc * cblk, cblk), cblk)
        for rc in range(n_tok // rblk):
            acc = jnp.zeros((rblk, cblk), F32)
            for q in range(SUBLANES):
                taps = [k for k in range(kw) if (k + lead - pad) % SUBLANES == q]
                if not taps:
                    continue
                shifted = hp_s[pl.ds(rc * rblk + q, rblk + span), cols]
                for k in taps:
                    o = k + lead - pad - q
                    acc = acc + wdw_ref[k:k + 1, cols] * shifted[o:o + rblk]
            cv_s[rc * rblk:(rc + 1) * rblk, cols] = acc + bdw_ref[:, cols]
        return carry

    lax.fori_loop(0, d // cblk, col_block, 0)
    gate = g_ref[0, nb:nb + 1, :]
    o_ref[0] = _conv_tail(cv_s[...], cg_ref, cb_ref, w2_ref, b2_ref, x_ref[0], gate, lg_ref, lb_ref)


def _conv_post(h, w_dw, b_dw, cv_g, cv_b, w2_bf, j, b2, x, mods_all, layer, ln_g, ln_b, n_rows, is_ctx, nb):
    n, t, d = x.shape
    kw = w_dw.shape[0]
    weights = [pl.BlockSpec((kw, d), lambda b, i: (0, 0)), _vec_spec(d), _vec_spec(d), _vec_spec(d),
               pl.BlockSpec((1, d, d), lambda b, i: (j, 0, 0), pipeline_mode=pl.Buffered(1)), _vec_spec(d)]
    tail = [_mod_spec(layer, d, 2), _vec_spec(d), _vec_spec(d)]
    args_w = (w_dw, b_dw.reshape(1, d), cv_g.reshape(1, d), cv_b.reshape(1, d), w2_bf, b2.reshape(1, d))
    args_t = (mods_all, ln_g.reshape(1, d), ln_b.reshape(1, d))
    if is_ctx:
        blk = pl.BlockSpec((1, t, d), lambda b, i: (b, 0, 0))
        return pl.pallas_call(
            functools.partial(_conv_ctx_kernel, n_tok=t, nb=nb),
            grid=(n, 1),
            in_specs=[blk] + weights + [blk] + tail,
            out_specs=blk,
            out_shape=jax.ShapeDtypeStruct((n, t, d), F32),
            scratch_shapes=[pltpu.VMEM((t + 4 * SUBLANES, d), F32), pltpu.VMEM((t, d), F32)],
            compiler_params=_params(("arbitrary", "arbitrary")),
            name="conv_post_ctx",
        )(h, *args_w, x, *args_t)
    wt = CONV_WT
    width = t // n_rows
    blk = pl.BlockSpec((1, n_rows, wt, d), lambda b, i: (b, 0, i, 0))
    out = pl.pallas_call(
        functools.partial(_conv_lat_kernel, n_rows=n_rows, wt=wt),
        grid=(n, width // wt),
        in_specs=[blk] + weights + [blk] + tail,
        out_specs=blk,
        out_shape=jax.ShapeDtypeStruct((n, n_rows, width, d), F32),
        scratch_shapes=[pltpu.VMEM((n_rows + 2 * (kw // 2), wt, d), F32),
                        pltpu.VMEM((kw, SUBLANES, d), F32),
                        pltpu.VMEM((n_rows * wt, d), F32)],
        compiler_params=_params(("arbitrary", "arbitrary")),
        name="conv_post",
    )(h.reshape(n, n_rows, width, d), *args_w, x.reshape(n, n_rows, width, d), *args_t)
    return out.reshape(n, t, d)


def _router_kernel(x_ref, sh_ref, sc_ref, wr_ref, u_ref, a_ref, *, is_ctx, nb):
    row = _mod_row(is_ctx, nb)
    u = x_ref[0] * (1.0 + sc_ref[0, pl.ds(row, 1), :]) + sh_ref[0, pl.ds(row, 1), :]
    u_ref[0] = u
    uh, ul = _split_bf16(u)
    wh, wl = _split_bf16(wr_ref[0])
    nt = (((1,), (1,)), ((), ()))
    logits = (lax.dot_general(wh, uh, nt, preferred_element_type=F32)
              + lax.dot_general(wh, ul, nt, preferred_element_type=F32)
              + lax.dot_general(wl, uh, nt, preferred_element_type=F32))
    m = jnp.max(logits, axis=0, keepdims=True)
    ex = jnp.exp(logits - m)
    a_ref[0] = ex / jnp.sum(ex, axis=0, keepdims=True)


def _router(x, mods_all, layer, w_router_t, is_ctx, nb):
    n, t, d = x.shape
    e = w_router_t.shape[1]
    tm = min(t, ROW_TILE)
    kern = functools.partial(_router_kernel, is_ctx=is_ctx, nb=nb)
    tok = pl.BlockSpec((1, tm, d), lambda b, i: (b, i, 0))
    return pl.pallas_call(
        kern,
        grid=(n, t // tm),
        in_specs=[tok, _mod_spec(layer, d, 3), _mod_spec(layer, d, 4),
                  pl.BlockSpec((1, e, d), lambda b, i: (layer, 0, 0))],
        out_specs=[tok, pl.BlockSpec((1, e, tm), lambda b, i: (b, 0, i))],
        out_shape=[jax.ShapeDtypeStruct((n, t, d), F32), jax.ShapeDtypeStruct((n, e, t), F32)],
        compiler_params=_params(("arbitrary", "arbitrary")),
        name="router",
    )(x, mods_all, mods_all, w_router_t)


def _topk_kernel(a_ref, pt_ref, idx_ref, gate_ref, *, cap):
    a = a_ref[0]
    e, t = a.shape
    capf = jnp.float32(cap)

    def count(mask):
        return jnp.sum(jnp.where(mask, 1.0, 0.0), axis=1, keepdims=True)

    def as_row_values(bits):
        return jnp.concatenate([pltpu.bitcast(bits, F32)] * (t // LANES), axis=1)

    thr_bits = jnp.zeros((e, LANES), jnp.int32)
    for bit in range(30, -1, -1):
        cand = thr_bits | jnp.int32(1 << bit)
        keep = count(a >= as_row_values(cand)) >= capf
        thr_bits = jnp.where(keep, cand, thr_bits)
    thr = as_row_values(thr_bits)
    gt = a > thr
    eq = a == thr
    need = capf - count(gt)
    tri = jnp.where(_iota((t, t), 0) <= _iota((t, t), 1), 1.0, 0.0).astype(BF16)
    eq_f = jnp.where(eq, 1.0, 0.0)
    eq_rank = jnp.dot(eq_f.astype(BF16), tri, preferred_element_type=F32) - eq_f
    sel = jnp.where(gt, 1.0, jnp.where(eq & (eq_rank < need), 1.0, 0.0))
    slot = jnp.dot(sel.astype(BF16), tri, preferred_element_type=F32) - 1.0
    pos = jnp.where(sel > 0.0, slot, -1.0)
    padded = jnp.concatenate([pos, jnp.full((LANES - e, t), -1.0, F32)], axis=0)
    pos_tm = padded.T
    pt_ref[0] = pos_tm.astype(jnp.int32)
    tok = _iota((1, t), 1)
    tok_hi = (tok >> TOK_SPLIT_BITS).astype(F32)
    tok_lo = (tok & ((1 << TOK_SPLIT_BITS) - 1)).astype(F32)
    slots = _iota((1, cap), 1).astype(F32)
    for ei in range(e):
        onehot_t = jnp.where(pos_tm[:, ei:ei + 1] == slots, 1.0, 0.0).astype(BF16)
        g = a[ei:ei + 1, :]
        g_hi = g.astype(BF16).astype(F32)
        g_mid = (g - g_hi).astype(BF16).astype(F32)
        g_lo = (g - g_hi) - g_mid
        lhs = jnp.concatenate([tok_hi, tok_lo, g_hi, g_mid, g_lo, jnp.zeros((SUBLANES - 5, t), F32)], axis=0)
        res = jnp.dot(lhs.astype(BF16), onehot_t, preferred_element_type=F32)
        idx_ref[0, ei:ei + 1, :] = (res[0:1] * float(1 << TOK_SPLIT_BITS) + res[1:2]).astype(jnp.int32)
        gate_ref[0, ei:ei + 1, :] = res[2:3] + res[3:4] + res[4:5]


def _topk(aff_t, cap):
    n, e, t = aff_t.shape
    kern = functools.partial(_topk_kernel, cap=cap)
    return pl.pallas_call(
        kern,
        grid=(n,),
        in_specs=[pl.BlockSpec((1, e, t), lambda b: (b, 0, 0))],
        out_specs=[pl.BlockSpec((1, t, LANES), lambda b: (b, 0, 0)),
                   pl.BlockSpec((1, e, cap), lambda b: (b, 0, 0)),
                   pl.BlockSpec((1, e, cap), lambda b: (b, 0, 0))],
        out_shape=[jax.ShapeDtypeStruct((n, t, LANES), jnp.int32), jax.ShapeDtypeStruct((n, e, cap), jnp.int32),
                   jax.ShapeDtypeStruct((n, e, cap), F32)],
        compiler_params=_params(("arbitrary",)),
        name="topk",
    )(aff_t)


def _ffn_kernel(idx_ref, *refs, rows, n_exp):
    n_streams = len(rows)
    src_refs = refs[:n_streams]
    gate_ref, wg_ref, wu_ref, wo_ref = refs[n_streams:n_streams + 4]
    y_refs = refs[n_streams + 4:2 * n_streams + 4]
    act_s, wo_s, x_s, land_s, sem = refs[2 * n_streams + 4:]
    e = pl.program_id(0)
    fc = pl.program_id(1)
    nfc = act_s.shape[0]
    r_tot = sum(rows)
    bounds = [0]
    for r in rows:
        bounds.append(bounds[-1] + r)

    def row_copy(expert, row, k):
        tok = idx_ref[expert * r_tot + row]
        return pltpu.make_async_copy(src_refs[k].at[pl.ds(tok, 1), :], land_s.at[pl.ds(row, 1), :], sem)

    def wait_all_rows():
        pltpu.make_async_copy(src_refs[0].at[pl.ds(0, r_tot), :], land_s, sem).wait()

    @pl.when((e == 0) & (fc == 0))
    def _():
        for k in range(n_streams):
            def issue(row, carry, k=k):
                row_copy(0, row, k).start()
                return carry
            lax.fori_loop(bounds[k], bounds[k + 1], issue, 0)

    @pl.when(fc == 0)
    def _():
        wait_all_rows()
        x_s[...] = land_s[...].astype(BF16)

    nxt = lax.rem(e + 1, n_exp)
    for k in range(n_streams):
        share = rows[k] // nfc
        for i in range(share):
            row_copy(nxt, bounds[k] + fc * share + i, k).start()

    x = x_s[...]
    g = jnp.dot(x, wg_ref[0, 0].astype(BF16), preferred_element_type=F32)
    up = jnp.dot(x, wu_ref[0, 0].astype(BF16), preferred_element_type=F32)
    act_s[fc] = ((g * jax.nn.sigmoid(g)) * up).astype(BF16)
    wo_s[fc] = wo_ref[0, 0].astype(BF16)

    @pl.when(fc == nfc - 1)
    def _():
        w_out = wo_s[...].reshape(nfc * wo_s.shape[1], wo_s.shape[2])
        gate_col = jnp.broadcast_to(gate_ref[0], (LANES, r_tot)).T[:, 0:1]
        half = r_tot // 2
        for r0, r1 in ((0, half), (half, r_tot)):
            act = jnp.concatenate([act_s[c, r0:r1, :] for c in range(nfc)], axis=1)
            y = (jnp.dot(act, w_out, preferred_element_type=F32) * gate_col[r0:r1]).astype(BF16)
            for k in range(n_streams):
                lo, hi = max(r0, bounds[k]), min(r1, bounds[k + 1])
                if lo < hi:
                    y_refs[k][0, lo - bounds[k]:hi - bounds[k], :] = y[lo - r0:hi - r0]

    @pl.when((e == n_exp - 1) & (fc == nfc - 1))
    def _():
        wait_all_rows()


def _ffn(streams, idx, gates, w_in, w_out, layer):
    d = streams[0].shape[2]
    e = gates.shape[0]
    ff = w_out.shape[2]
    nfc = ff // FF_TILE
    rows = tuple(s.shape[0] * (EC_CAPACITY_FACTOR * s.shape[1] // N_EXPERTS) for s in streams)
    r_tot = sum(rows)
    assert all(r % nfc == 0 for r in rows) and r_tot % LANES == 0
    return pl.pallas_call(
        functools.partial(_ffn_kernel, rows=rows, n_exp=e),
        grid_spec=pltpu.PrefetchScalarGridSpec(
            num_scalar_prefetch=1,
            grid=(e, nfc),
            in_specs=[pl.BlockSpec(memory_space=pl.ANY) for _ in streams] + [
                pl.BlockSpec((1, 1, r_tot), lambda ei, fc, ix: (ei, 0, 0)),
                pl.BlockSpec((1, 1, d, FF_TILE), lambda ei, fc, ix: (layer, ei, 0, fc)),
                pl.BlockSpec((1, 1, d, FF_TILE), lambda ei, fc, ix: (layer, ei, 0, nfc + fc)),
                pl.BlockSpec((1, 1, FF_TILE, d), lambda ei, fc, ix: (layer, ei, fc, 0))],
            out_specs=[pl.BlockSpec((1, r, d), lambda ei, fc, ix: (ei, 0, 0)) for r in rows],
            scratch_shapes=[pltpu.VMEM((nfc, r_tot, FF_TILE), BF16), pltpu.VMEM((nfc, FF_TILE, d), BF16),
                            pltpu.VMEM((r_tot, d), BF16), pltpu.VMEM((r_tot, d), F32),
                            pltpu.SemaphoreType.DMA(())],
        ),
        out_shape=[jax.ShapeDtypeStruct((e, r, d), BF16) for r in rows],
        compiler_params=_params(("arbitrary", "arbitrary")),
        name="expert_ffn",
    )(idx, *[s.reshape(s.shape[0] * s.shape[1], d) for s in streams], gates, w_in, w_in, w_out)


def _combine_kernel(y_ref, pt_ref, x_ref, g_ref, lg_ref, lb_ref, o_ref, oh_s, *, is_ctx, nb, cap):
    n_exp, _, d = y_ref.shape
    ec = n_exp * cap
    cw = min(ec, COMBINE_COLS)
    pos = pt_ref[0].astype(F32).astype(BF16)
    for c0 in range(0, ec, cw):
        src = _iota((LANES, cw), 0)
        dst = _iota((LANES, cw), 1) + c0
        expand = jnp.where((dst >= src * cap) & (dst < (src + 1) * cap), 1.0, 0.0).astype(BF16)
        pe = jnp.dot(pos, expand, preferred_element_type=F32)
        slot = (_iota((1, cw), 1) + c0) % cap
        oh_s[:, c0:c0 + cw] = jnp.where(pe == slot.astype(F32), 1.0, 0.0).astype(BF16)
    f = jnp.dot(oh_s[...], y_ref[...].reshape(ec, d), preferred_element_type=F32)
    gate = g_ref[0, pl.ds(_mod_row(is_ctx, nb), 1), :]
    o_ref[0] = _layer_norm(DEEPNORM_ALPHA * x_ref[0] + gate * f, lg_ref[...], lb_ref[...])


def _combine(y, pos_tm, x, mods_all, layer, ln_g, ln_b, cap, is_ctx, nb):
    n, t, d = x.shape
    e = y.shape[0]
    tm = min(t, ROW_TILE)
    kern = functools.partial(_combine_kernel, is_ctx=is_ctx, nb=nb, cap=cap)
    tok = pl.BlockSpec((1, tm, d), lambda b, i: (b, i, 0))
    return pl.pallas_call(
        kern,
        grid=(n, t // tm),
        in_specs=[
            pl.BlockSpec((e, cap, d), lambda b, i: (0, b, 0), pipeline_mode=pl.Buffered(1)),
            pl.BlockSpec((1, tm, LANES), lambda b, i: (b, i, 0)),
            tok,
            _mod_spec(layer, d, 5),
            _vec_spec(d), _vec_spec(d),
        ],
        out_specs=tok,
        out_shape=jax.ShapeDtypeStruct((n, t, d), F32),
        scratch_shapes=[pltpu.VMEM((tm, e * cap), BF16)],
        compiler_params=_params(("arbitrary", "arbitrary")),
        name="combine",
    )(y, pos_tm, x, mods_all, ln_g.reshape(1, d), ln_b.reshape(1, d))


def _moe_post(x1_lat, x1_ctx, mods_all, layer, w_router_t, w_in, w_out, ln_g, ln_b, nb):
    streams = [(x1_lat, False)] + ([(x1_ctx, True)] if x1_ctx is not None else [])
    caps = [EC_CAPACITY_FACTOR * s.shape[1] // N_EXPERTS for s, _ in streams]
    routed, tokens, idx_parts, gate_parts = [], [], [], []
    for (s, is_ctx), cap in zip(streams, caps):
        n, t = s.shape[:2]
        u2, aff_t = _router(s, mods_all, layer, w_router_t, is_ctx, nb)
        pos_tm, idx, gate = _topk(aff_t, cap)
        routed.append(pos_tm)
        tokens.append(u2)
        flat = idx + (jnp.arange(n, dtype=jnp.int32) * t)[:, None, None]
        idx_parts.append(jnp.swapaxes(flat, 0, 1).reshape(N_EXPERTS, n * cap))
        gate_parts.append(jnp.swapaxes(gate, 0, 1).reshape(N_EXPERTS, n * cap))
    idx_all = jnp.concatenate(idx_parts, axis=1).reshape(-1)
    gates = jnp.concatenate(gate_parts, axis=1)[:, None, :]
    ys = _ffn(tokens, idx_all, gates, w_in, w_out, layer)
    outs = [_combine(y, pos_tm, s, mods_all, layer, ln_g, ln_b, cap, is_ctx, nb)
            for (s, is_ctx), pos_tm, cap, y in zip(streams, routed, caps, ys)]
    return outs[0], (outs[1] if len(outs) > 1 else None)


def kernel(x, c, ctx, c_ctx, ada_w, ada_b, ln_g, ln_b, s5_a_re, s5_a_im, s5_log_dt, s5_b_re, s5_b_im,
           s5_c_re, s5_c_im, s5_d, s5_w_glu, s5_b_glu, cv_w_pw1, cv_b_pw1, cv_w_dw, cv_b_dw, cv_ln_g,
           cv_ln_b, cv_w_pw2, cv_b_pw2, moe_w_router, moe_w_in, moe_w_out):
    nb, seq, d = x.shape
    assert nb + 1 <= MOD_ROWS and nb % 2 == 0 and d % LANES == 0
    rows = seq // GRID_W

    c8 = jnp.concatenate([c.astype(F32), c_ctx.astype(F32)[None], jnp.zeros((MOD_ROWS - nb - 1, d), F32)], axis=0)
    mods_all = _ada_all(c8, ada_w, ada_b)
    w_glu_bf = s5_w_glu.astype(BF16)
    w_pw1_bf = cv_w_pw1.astype(BF16)
    w_pw2_bf = cv_w_pw2.astype(BF16)
    w_router_t = jnp.swapaxes(moe_w_router, 1, 2)

    x_lat, x_ctx = x, ctx
    for i in range(DEPTH):
        is_s5 = (i % N_MIXERS) == 0
        j = i // N_MIXERS
        ctx_out = any((k % N_MIXERS) == 0 for k in range(i + 1, DEPTH))
        x1_ctx = None

        if is_s5:
            ops = _s5_operators(s5_a_re[j], s5_a_im[j], s5_log_dt[j], s5_b_re[j], s5_b_im[j],
                                s5_c_re[j], s5_c_im[j])
            z_lat, z_ctx = _s5_core(x_lat, x_ctx, mods_all, i, s5_d[j], ops, ctx_out)
            x1_lat = _s5_out(z_lat, w_glu_bf, j, s5_b_glu[j], x_lat, mods_all, i, ln_g[i, 0], ln_b[i, 0], False, nb)
            if ctx_out:
                x1_ctx = _s5_out(z_ctx, w_glu_bf, j, s5_b_glu[j], x_ctx, mods_all, i, ln_g[i, 0], ln_b[i, 0],
                                 True, nb)
        else:
            conv_args = (cv_w_dw[j], cv_b_dw[j], cv_ln_g[j], cv_ln_b[j], w_pw2_bf, j, cv_b_pw2[j])
            h_lat = _pw1(x_lat, mods_all, i, w_pw1_bf, j, cv_b_pw1[j], False, nb)
            x1_lat = _conv_post(h_lat, *conv_args, x_lat, mods_all, i, ln_g[i, 0], ln_b[i, 0], rows, False, nb)
            if ctx_out:
                h_ctx = _pw1(x_ctx, mods_all, i, w_pw1_bf, j, cv_b_pw1[j], True, nb)
                x1_ctx = _conv_post(h_ctx, *conv_args, x_ctx, mods_all, i, ln_g[i, 0], ln_b[i, 0], None, True, nb)

        x_lat, x_ctx_new = _moe_post(x1_lat, x1_ctx, mods_all, i, w_router_t, moe_w_in, moe_w_out,
                                     ln_g[i, 1], ln_b[i, 1], nb)
        if ctx_out:
            x_ctx = x_ctx_new
    return x_lat
```

```python
import functools

import jax
import jax.numpy as jnp
from jax import lax
from jax.experimental import pallas as pl
from jax.experimental.pallas import tpu as pltpu

F32 = jnp.float32
BF16 = jnp.bfloat16

DEPTH = 4
N_MIXERS = 2
GRID_W = 64
S5_H = 16
S5_P = 64
N_EXPERTS = 16
EC_CAPACITY_FACTOR = 2
DEEPNORM_ALPHA = (2.0 * DEPTH) ** 0.25
LN_EPS = 1e-5

LANES = 128
SUBLANES = 8
S5_CHUNK = 8
S5_GPT = LANES // S5_H
S5_STATE_COLS = S5_GPT * S5_P
MOD_ROWS = 8
VMEM_LIMIT = 56 * 1024 * 1024
ROW_TILE = 512
COMBINE_COLS = 1024
TOK_SPLIT_BITS = 6
FF_TILE = 256
CONV_WT = 2 * SUBLANES


def _params(sem, vmem=VMEM_LIMIT):
    return pltpu.CompilerParams(dimension_semantics=sem, vmem_limit_bytes=vmem)


def _layer_norm(v, g, b):
    mu = jnp.mean(v, axis=-1, keepdims=True)
    c = v - mu
    var = jnp.mean(c * c, axis=-1, keepdims=True)
    return c * lax.rsqrt(var + LN_EPS) * g + b


def _split_bf16(v):
    hi = v.astype(BF16)
    lo = (v - hi.astype(F32)).astype(BF16)
    return hi, lo


def _mod_spec(layer, d, k):
    return pl.BlockSpec((1, MOD_ROWS, d), lambda *_: (layer, 0, k))


def _vec_spec(d):
    return pl.BlockSpec((1, d), lambda *_: (0, 0))


def _mod_row(is_ctx, nb):
    return nb if is_ctx else pl.program_id(0)


def _ada_kernel(c_ref, w_ref, b_ref, o_ref):
    c = c_ref[...]
    cond = c * jax.nn.sigmoid(c)
    hi, lo = _split_bf16(cond)
    lhs = jnp.concatenate([hi, lo], axis=0)
    r = jnp.dot(lhs, w_ref[0].astype(BF16), preferred_element_type=F32)
    o_ref[0] = r[:MOD_ROWS] + r[MOD_ROWS:] + b_ref[0]


def _ada_all(c8, ada_w, ada_b):
    depth, d, n = ada_w.shape
    tn = 1024
    return pl.pallas_call(
        _ada_kernel,
        grid=(depth, n // tn),
        in_specs=[
            pl.BlockSpec((MOD_ROWS, d), lambda i, k: (0, 0)),
            pl.BlockSpec((1, d, tn), lambda i, k: (i, 0, k)),
            pl.BlockSpec((1, 1, tn), lambda i, k: (i, 0, k)),
        ],
        out_specs=pl.BlockSpec((1, MOD_ROWS, tn), lambda i, k: (i, 0, k)),
        out_shape=jax.ShapeDtypeStruct((depth, MOD_ROWS, n), F32),
        compiler_params=_params(("arbitrary", "arbitrary")),
        name="adaln",
    )(c8, ada_w, ada_b.reshape(depth, 1, n))


def _cmul(ar, ai, br, bi):
    return ar * br - ai * bi, ar * bi + ai * br


def _s5_operators(a_re, a_im, log_dt, b_re, b_im, c_re, c_im):
    hp = lax.Precision.HIGHEST
    t = S5_CHUNK
    g = a_re.shape[1]
    nj = g // S5_GPT
    lam_r, lam_i = a_re.astype(F32), a_im.astype(F32)
    dt = jnp.exp(log_dt.astype(F32))[..., None]
    mag = jnp.exp(lam_r * dt)
    abar_r, abar_i = mag * jnp.cos(lam_i * dt), mag * jnp.sin(lam_i * dt)
    den = lam_r * lam_r + lam_i * lam_i
    xr, xi = abar_r - 1.0, abar_i
    coef_r, coef_i = (xr * lam_r + xi * lam_i) / den, (xi * lam_r - xr * lam_i) / den
    bb_r, bb_i = _cmul(coef_r[..., None], coef_i[..., None], b_re.astype(F32), b_im.astype(F32))
    cm_r, cm_i = c_re.astype(F32), c_im.astype(F32)
    pr, pi = [jnp.ones_like(abar_r)], [jnp.zeros_like(abar_r)]
    for _ in range(t):
        nr, ni = _cmul(pr[-1], pi[-1], abar_r, abar_i)
        pr.append(nr)
        pi.append(ni)
    pr, pi = jnp.stack(pr), jnp.stack(pi)

    def lag_kernels(d):
        qr, qi = _cmul(pr[:t, d, :, :, None], pi[:t, d, :, :, None], bb_r[d][None], bb_i[d][None])
        return (jnp.einsum('ghp,kgpj->kghj', cm_r[d], qr, precision=hp)
                - jnp.einsum('ghp,kgpj->kghj', cm_i[d], qi, precision=hp))

    kf, kb = lag_kernels(0), lag_kernels(1)
    kall = jnp.concatenate([kb[:0:-1], (kf[0] + kb[0])[None], kf[1:]], axis=0)
    nlag = 2 * t - 1
    kl = kall.reshape(nlag, nj, S5_GPT, S5_H, S5_H).transpose(1, 0, 2, 4, 3).reshape(nj, nlag * LANES, S5_H)

    def per_tile(re, im, perm, rows, cols):
        both = jnp.stack([re, im], axis=1)
        both = both.reshape(2, 2, nj, S5_GPT, both.shape[-2], both.shape[-1])
        return both.transpose(perm).reshape(nj, rows, cols)

    bt = per_tile(bb_r, bb_i, (2, 0, 1, 3, 5, 4), 4 * LANES, S5_P)
    ct = per_tile(cm_r, cm_i, (2, 0, 1, 3, 5, 4), 4 * S5_STATE_COLS, S5_H)

    exps_b = (jnp.arange(t - 1, -1, -1), jnp.arange(t))
    exps_c = (jnp.arange(1, t + 1), jnp.arange(t, 0, -1))

    def powers(exps):
        both = jnp.stack([jnp.stack([pr[exps[d], d], pi[exps[d], d]]) for d in range(2)])
        return both.reshape(2, 2, t, nj, S5_STATE_COLS)

    pw = powers(exps_b).transpose(3, 0, 1, 2, 4).reshape(nj, 4 * t, S5_STATE_COLS)
    pc = powers(exps_c).transpose(3, 0, 1, 4, 2).reshape(nj, 4 * S5_STATE_COLS, t)

    def decay_tiles(d):
        ar = pr[t, d].reshape(nj, S5_STATE_COLS // LANES, LANES)
        ai = pi[t, d].reshape(nj, S5_STATE_COLS // LANES, LANES)
        return jnp.concatenate([ar, ar, -ai, ai], axis=1)

    at = jnp.concatenate([decay_tiles(0), decay_tiles(1)], axis=1)
    return kl, bt, ct, pw, pc, at


def _iota(shape, axis):
    return lax.broadcasted_iota(jnp.int32, shape, axis)


def _group_blockdiag(comp, width, rows_per_group):
    k = comp.shape[1]
    n = S5_GPT * k
    rep = jnp.where((_iota((k, n), 1) & (k - 1)) == _iota((k, n), 0), 1.0, 0.0).astype(BF16)
    hi = comp.astype(BF16)
    rest = comp - hi.astype(F32)
    mid = rest.astype(BF16)
    lo = (rest - mid.astype(F32)).astype(BF16)
    tiled = (jnp.dot(hi, rep, preferred_element_type=F32) + jnp.dot(mid, rep, preferred_element_type=F32)
             + jnp.dot(lo, rep, preferred_element_type=F32))
    row_grp = (_iota((comp.shape[0], 1), 0) >> (rows_per_group.bit_length() - 1)) & (S5_GPT - 1)
    col_grp = _iota((1, n), 1) >> (width.bit_length() - 1)
    return jnp.where(row_grp == col_grp, tiled, 0.0)


def _build_s5_operators(kl_ref, bt_ref, ct_ref, pw_ref, pc_ref, wm_s, wb_s, wc_s):
    t = S5_CHUNK
    sc = S5_STATE_COLS
    lagk = _group_blockdiag(kl_ref[0], S5_H, S5_H).astype(BF16)
    for s in range(t):
        for u in range(t):
            lag = u - s + t - 1
            wm_s[s * LANES:(s + 1) * LANES, u * LANES:(u + 1) * LANES] = lagk[lag * LANES:(lag + 1) * LANES, :]
    bexp = _group_blockdiag(bt_ref[0], S5_P, S5_H)
    for d in range(2):
        b_re = bexp[(2 * d) * LANES:(2 * d + 1) * LANES]
        b_im = bexp[(2 * d + 1) * LANES:(2 * d + 2) * LANES]
        for s in range(t):
            p_re = pw_ref[0, (2 * d) * t + s:(2 * d) * t + s + 1, :]
            p_im = pw_ref[0, (2 * d + 1) * t + s:(2 * d + 1) * t + s + 1, :]
            rows = slice(s * LANES, (s + 1) * LANES)
            wb_s[rows, d * 2 * sc:d * 2 * sc + sc] = (b_re * p_re - b_im * p_im).astype(BF16)
            wb_s[rows, d * 2 * sc + sc:(d + 1) * 2 * sc] = (b_re * p_im + b_im * p_re).astype(BF16)
    cexp = _group_blockdiag(ct_ref[0], S5_H, S5_P)
    for d in range(2):
        c_re = cexp[(2 * d) * sc:(2 * d + 1) * sc]
        c_im = cexp[(2 * d + 1) * sc:(2 * d + 2) * sc]
        for u in range(t):
            p_re = pc_ref[0, (2 * d) * sc:(2 * d + 1) * sc, u:u + 1]
            p_im = pc_ref[0, (2 * d + 1) * sc:(2 * d + 2) * sc, u:u + 1]
            cols = slice(u * LANES, (u + 1) * LANES)
            wc_s[d * 2 * sc:d * 2 * sc + sc, cols] = (c_re * p_re - c_im * p_im).astype(BF16)
            wc_s[d * 2 * sc + sc:(d + 1) * 2 * sc, cols] = (-(c_re * p_im + c_im * p_re)).astype(BF16)


def _s5_kernel(x_ref, xc_ref, sh_ref, sc_ref, d_ref, kl_ref, bt_ref, ct_ref, pw_ref, pc_ref, at_ref, *rest,
               nb_half, n_lat, n_ctx, ctx_out):
    if ctx_out:
        z_ref, zc_ref, wm_s, wb_s, wc_s, xf_s, st_s, en_s, y_s, zb_s, zcb_s = rest
    else:
        z_ref, wm_s, wb_s, wc_s, xf_s, st_s, en_s, y_s, zb_s = rest
        zc_ref = zcb_s = None
    t = S5_CHUNK
    ncl = n_lat // t
    ncc = n_ctx // t
    lat_rows = nb_half * ncl
    half = pl.program_id(1)
    sc_cols = S5_STATE_COLS
    nst = sc_cols // LANES
    n_rows = nb_half * (ncl + ncc)

    @pl.when(half == 0)
    def _():
        _build_s5_operators(kl_ref, bt_ref, ct_ref, pw_ref, pc_ref, wm_s, wb_s, wc_s)

    for k in range(nb_half):
        b = half * nb_half + k
        scale = 1.0 + sc_ref[0, pl.ds(b, 1), :]
        shift = sh_ref[0, pl.ds(b, 1), :]
        for s in range(t):
            xf_s[k * ncl:(k + 1) * ncl, s * LANES:(s + 1) * LANES] = (
                x_ref[pl.ds(k * n_lat + s, ncl, stride=t), :] * scale + shift)
    scale_c = 1.0 + sc_ref[0, nb_half * 2:nb_half * 2 + 1, :]
    shift_c = sh_ref[0, nb_half * 2:nb_half * 2 + 1, :]
    for k in range(nb_half):
        for s in range(t):
            r0 = lat_rows + k * ncc
            xf_s[r0:r0 + ncc, s * LANES:(s + 1) * LANES] = (
                xc_ref[pl.ds(k * n_ctx + s, ncc, stride=t), :] * scale_c + shift_c)

    xb = xf_s[...].astype(BF16)
    dvec = jnp.concatenate([d_ref[...]] * t, axis=1)
    y_s[...] = jnp.dot(xb, wm_s[...], preferred_element_type=F32) + xf_s[...] * dvec

    for d in range(2):
        local = jnp.dot(xb, wb_s[:, d * 2 * sc_cols:(d + 1) * 2 * sc_cols], preferred_element_type=F32)
        for q in range(2 * nst):
            st_s[d, pl.ds(q, n_rows, stride=2 * nst), :] = local[:, q * LANES:(q + 1) * LANES]
    coef = [(at_ref[0, (2 * d) * 2 * nst:(2 * d + 1) * 2 * nst, :],
             at_ref[0, (2 * d + 1) * 2 * nst:(2 * d + 2) * 2 * nst, :]) for d in range(2)]

    def step(chunk_rows, h):
        new = []
        for d in range(2):
            for k in range(nb_half):
                tile = pl.ds(pl.multiple_of((chunk_rows[d] + k * chunk_rows[2]) * 2 * nst, 2 * nst), 2 * nst)
                prev, prev_sw = h[2 * (d * nb_half + k)], h[2 * (d * nb_half + k) + 1]
                local_state = st_s[d, tile, :]
                en_s[d, tile, :] = prev
                new.append(coef[d][0] * prev + coef[d][1] * prev_sw + local_state)
                new.append(coef[d][0] * prev_sw - coef[d][1] * prev + pltpu.roll(local_state, nst, axis=0))
        return tuple(new)

    zero = tuple(jnp.zeros((2 * nst, LANES), F32) for _ in range(4 * nb_half))
    h = lax.fori_loop(0, ncc, lambda i, h: step((lat_rows + i, lat_rows + ncc - 1 - i, ncc), h), zero, unroll=4)
    lax.fori_loop(0, ncl, lambda i, h: step((i, ncl - 1 - i, ncl), h), h, unroll=4)

    entering = jnp.concatenate([en_s[d, pl.ds(q, n_rows, stride=2 * nst), :]
                                for d in range(2) for q in range(2 * nst)], axis=1)
    y_s[...] += jnp.dot(entering.astype(BF16), wc_s[...], preferred_element_type=F32)

    z = jax.nn.gelu(y_s[...])
    for k in range(nb_half):
        for s in range(t):
            zb_s[pl.ds(k * n_lat + s, ncl, stride=t), :] = z[k * ncl:(k + 1) * ncl, s * LANES:(s + 1) * LANES]
    z_ref[...] = zb_s[...].astype(BF16)
    if ctx_out:
        for k in range(nb_half):
            for s in range(t):
                r0 = lat_rows + k * ncc
                zcb_s[pl.ds(k * n_ctx + s, ncc, stride=t), :] = z[r0:r0 + ncc, s * LANES:(s + 1) * LANES]
        zc_ref[...] = zcb_s[...].astype(BF16)


def _s5_core(x, xc, mods_all, layer, d_skip, ops, ctx_out):
    nb, n_lat, d = x.shape
    n_ctx = xc.shape[1]
    nj = d // LANES
    nb_half = nb // 2
    t = S5_CHUNK
    tl = t * LANES
    rows = nb_half * (n_lat + n_ctx) // t
    kern = functools.partial(_s5_kernel, nb_half=nb_half, n_lat=n_lat, n_ctx=n_ctx, ctx_out=ctx_out)
    out_shape = [jax.ShapeDtypeStruct((nb * n_lat, d), BF16)]
    out_specs = [pl.BlockSpec((nb_half * n_lat, LANES), lambda j, h: (h, j))]
    scratch = [pltpu.VMEM((tl, tl), BF16), pltpu.VMEM((tl, 4 * S5_STATE_COLS), BF16),
               pltpu.VMEM((4 * S5_STATE_COLS, tl), BF16),
               pltpu.VMEM((rows, tl), F32), pltpu.VMEM((2, rows * 2 * S5_STATE_COLS // LANES, LANES), F32),
               pltpu.VMEM((2, rows * 2 * S5_STATE_COLS // LANES, LANES), F32),
               pltpu.VMEM((rows, tl), F32), pltpu.VMEM((nb_half * n_lat, LANES), F32)]
    if ctx_out:
        out_shape.append(jax.ShapeDtypeStruct((nb * n_ctx, d), BF16))
        out_specs.append(pl.BlockSpec((nb_half * n_ctx, LANES), lambda j, h: (h, j)))
        scratch.append(pltpu.VMEM((nb_half * n_ctx, LANES), F32))
    res = pl.pallas_call(
        kern,
        grid=(nj, 2),
        in_specs=[
            pl.BlockSpec((nb_half * n_lat, LANES), lambda j, h: (h, j)),
            pl.BlockSpec((nb_half * n_ctx, LANES), lambda j, h: (h, j)),
            pl.BlockSpec((1, MOD_ROWS, LANES), lambda j, h: (layer, 0, j)),
            pl.BlockSpec((1, MOD_ROWS, LANES), lambda j, h: (layer, 0, nj + j)),
            pl.BlockSpec((1, LANES), lambda j, h: (0, j)),
        ] + [pl.BlockSpec((1,) + op.shape[1:], lambda j, h: (j, 0, 0)) for op in ops],
        out_specs=out_specs,
        out_shape=out_shape,
        scratch_shapes=scratch,
        compiler_params=_params(("arbitrary", "arbitrary")),
        name="s5_core",
    )(x.reshape(nb * n_lat, d), xc.reshape(nb * n_ctx, d), mods_all, mods_all, d_skip.reshape(1, d), *ops)
    z = res[0].reshape(nb, n_lat, d)
    zc = res[1].reshape(nb, n_ctx, d) if ctx_out else None
    return z, zc


def _s5_out_kernel(z_ref, w_ref, b_ref, x_ref, g_ref, lg_ref, lb_ref, o_ref, *, is_ctx, nb):
    d = x_ref.shape[-1]
    acc = jnp.dot(z_ref[0], w_ref[0], preferred_element_type=F32) + b_ref[...]
    y = acc[:, :d] * jax.nn.sigmoid(acc[:, d:])
    gate = g_ref[0, pl.ds(_mod_row(is_ctx, nb), 1), :]
    o_ref[0] = _layer_norm(DEEPNORM_ALPHA * x_ref[0] + gate * y, lg_ref[...], lb_ref[...])


def _s5_out(z, w_bf, j, b_glu, x, mods_all, layer, ln_g, ln_b, is_ctx, nb):
    n, t, d = x.shape
    tm = min(t, ROW_TILE)
    kern = functools.partial(_s5_out_kernel, is_ctx=is_ctx, nb=nb)
    tok = pl.BlockSpec((1, tm, d), lambda b, i: (b, i, 0))
    return pl.pallas_call(
        kern,
        grid=(n, t // tm),
        in_specs=[
            tok,
            pl.BlockSpec((1, d, 2 * d), lambda b, i: (j, 0, 0), pipeline_mode=pl.Buffered(1)),
            _vec_spec(2 * d),
            tok,
            _mod_spec(layer, d, 2),
            _vec_spec(d), _vec_spec(d),
        ],
        out_specs=tok,
        out_shape=jax.ShapeDtypeStruct((n, t, d), F32),
        compiler_params=_params(("arbitrary", "arbitrary")),
        name="s5_out",
    )(z, w_bf, b_glu.reshape(1, 2 * d), x, mods_all, ln_g.reshape(1, d), ln_b.reshape(1, d))


def _pw1_kernel(x_ref, sh_ref, sc_ref, w_ref, b_ref, o_ref, *, is_ctx, nb):
    d = x_ref.shape[-1]
    row = _mod_row(is_ctx, nb)
    u = x_ref[0] * (1.0 + sc_ref[0, pl.ds(row, 1), :]) + sh_ref[0, pl.ds(row, 1), :]
    acc = jnp.dot(u.astype(BF16), w_ref[0], preferred_element_type=F32) + b_ref[...]
    o_ref[0] = acc[:, :d] * jax.nn.sigmoid(acc[:, d:])


def _pw1(x, mods_all, layer, w_bf, j, b_pw1, is_ctx, nb):
    n, t, d = x.shape
    tm = min(t, ROW_TILE)
    kern = functools.partial(_pw1_kernel, is_ctx=is_ctx, nb=nb)
    tok = pl.BlockSpec((1, tm, d), lambda b, i: (b, i, 0))
    return pl.pallas_call(
        kern,
        grid=(n, t // tm),
        in_specs=[
            tok,
            _mod_spec(layer, d, 0), _mod_spec(layer, d, 1),
            pl.BlockSpec((1, d, 2 * d), lambda b, i: (j, 0, 0), pipeline_mode=pl.Buffered(1)),
            _vec_spec(2 * d),
        ],
        out_specs=tok,
        out_shape=jax.ShapeDtypeStruct((n, t, d), F32),
        compiler_params=_params(("arbitrary", "arbitrary")),
        name="conv_pw1",
    )(x, mods_all, mods_all, w_bf, b_pw1.reshape(1, 2 * d))


def _conv_tail(cv, cg_ref, cb_ref, w2_ref, b2_ref, x, gate, lg_ref, lb_ref):
    hn = _layer_norm(cv, cg_ref[...], cb_ref[...])
    hn = hn * jax.nn.sigmoid(hn)
    y = jnp.dot(hn.astype(BF16), w2_ref[0], preferred_element_type=F32) + b2_ref[...]
    return _layer_norm(DEEPNORM_ALPHA * x + gate * y, lg_ref[...], lb_ref[...])


def _conv_lat_kernel(h_ref, wdw_ref, bdw_ref, cg_ref, cb_ref, w2_ref, b2_ref, x_ref, g_ref, lg_ref, lb_ref,
                     o_ref, wb_s, cv_s, *, n_rows, wt):
    kw = wdw_ref.shape[0]
    pad = kw // 2
    d = h_ref.shape[-1]
    nsub = wt // SUBLANES
    for k in range(kw):
        wb_s[k] = jnp.broadcast_to(wdw_ref[k:k + 1, :], (SUBLANES, d))
    bias = bdw_ref[...]
    for r in range(n_rows):
        accs = [None] * nsub
        for k in range(max(0, pad - r), min(kw, n_rows + pad - r)):
            w8 = wb_s[k]
            for q in range(nsub):
                term = w8 * h_ref[0, r + k - pad, q * SUBLANES:(q + 1) * SUBLANES, :]
                accs[q] = term if accs[q] is None else accs[q] + term
        for q in range(nsub):
            cv_s[r * wt + q * SUBLANES:r * wt + (q + 1) * SUBLANES, :] = accs[q] + bias
    gate = g_ref[0, pl.ds(pl.program_id(0), 1), :]
    x = x_ref[0].reshape(n_rows * wt, d)
    out = _conv_tail(cv_s[...], cg_ref, cb_ref, w2_ref, b2_ref, x, gate, lg_ref, lb_ref)
    o_ref[0] = out.reshape(n_rows, wt, d)


def _conv_ctx_kernel(h_ref, wdw_ref, bdw_ref, cg_ref, cb_ref, w2_ref, b2_ref, x_ref, g_ref, lg_ref, lb_ref,
                     o_ref, hp_s, cv_s, *, n_tok, nb):
    kw = wdw_ref.shape[0]
    pad = kw // 2
    d = h_ref.shape[-1]
    lead = 2 * SUBLANES
    rblk = 128
    cblk = 2 * LANES
    hp_s[0:lead] = jnp.zeros((lead, d), F32)
    hp_s[lead + n_tok:lead + n_tok + lead] = jnp.zeros((lead, d), F32)
    hp_s[lead:lead + n_tok] = h_ref[0]
    span = ((kw - 1 + lead - pad) // SUBLANES) * SUBLANES

    def col_block(lc, carry):
        cols = pl.ds(pl.multiple_of(lc * cblk, cblk), cblk)
        for rc in range(n_tok // rblk):
            acc = jnp.zeros((rblk, cblk), F32)
            for q in range(SUBLANES):
                taps = [k for k in range(kw) if (k + lead - pad) % SUBLANES == q]
                if not taps:
                    continue
                shifted = hp_s[pl.ds(rc * rblk + q, rblk + span), cols]
                for k in taps:
                    o = k + lead - pad - q
                    acc = acc + wdw_ref[k:k + 1, cols] * shifted[o:o + rblk]
            cv_s[rc * rblk:(rc + 1) * rblk, cols] = acc + bdw_ref[:, cols]
        return carry

    lax.fori_loop(0, d // cblk, col_block, 0)
    gate = g_ref[0, nb:nb + 1, :]
    o_ref[0] = _conv_tail(cv_s[...], cg_ref, cb_ref, w2_ref, b2_ref, x_ref[0], gate, lg_ref, lb_ref)


def _conv_post(h, w_dw, b_dw, cv_g, cv_b, w2_bf, j, b2, x, mods_all, layer, ln_g, ln_b, n_rows, is_ctx, nb):
    n, t, d = x.shape
    kw = w_dw.shape[0]
    weights = [pl.BlockSpec((kw, d), lambda b, i: (0, 0)), _vec_spec(d), _vec_spec(d), _vec_spec(d),
               pl.BlockSpec((1, d, d), lambda b, i: (j, 0, 0), pipeline_mode=pl.Buffered(1)), _vec_spec(d)]
    tail = [_mod_spec(layer, d, 2), _vec_spec(d), _vec_spec(d)]
    args_w = (w_dw, b_dw.reshape(1, d), cv_g.reshape(1, d), cv_b.reshape(1, d), w2_bf, b2.reshape(1, d))
    args_t = (mods_all, ln_g.reshape(1, d), ln_b.reshape(1, d))
    if is_ctx:
        blk = pl.BlockSpec((1, t, d), lambda b, i: (b, 0, 0))
        return pl.pallas_call(
            functools.partial(_conv_ctx_kernel, n_tok=t, nb=nb),
            grid=(n, 1),
            in_specs=[blk] + weights + [blk] + tail,
            out_specs=blk,
            out_shape=jax.ShapeDtypeStruct((n, t, d), F32),
            scratch_shapes=[pltpu.VMEM((t + 4 * SUBLANES, d), F32), pltpu.VMEM((t, d), F32)],
            compiler_params=_params(("arbitrary", "arbitrary")),
            name="conv_post_ctx",
        )(h, *args_w, x, *args_t)
    wt = CONV_WT
    width = t // n_rows
    blk = pl.BlockSpec((1, n_rows, wt, d), lambda b, i: (b, 0, i, 0))
    out = pl.pallas_call(
        functools.partial(_conv_lat_kernel, n_rows=n_rows, wt=wt),
        grid=(n, width // wt),
        in_specs=[blk] + weights + [blk] + tail,
        out_specs=blk,
        out_shape=jax.ShapeDtypeStruct((n, n_rows, width, d), F32),
        scratch_shapes=[pltpu.VMEM((kw, SUBLANES, d), F32),
                        pltpu.VMEM((n_rows * wt, d), F32)],
        compiler_params=_params(("arbitrary", "arbitrary")),
        name="conv_post",
    )(h.reshape(n, n_rows, width, d), *args_w, x.reshape(n, n_rows, width, d), *args_t)
    return out.reshape(n, t, d)


def _router_kernel(x_ref, sh_ref, sc_ref, wr_ref, u_ref, a_ref, *, is_ctx, nb):
    row = _mod_row(is_ctx, nb)
    u = x_ref[0] * (1.0 + sc_ref[0, pl.ds(row, 1), :]) + sh_ref[0, pl.ds(row, 1), :]
    u_ref[0] = u
    uh, ul = _split_bf16(u)
    wh, wl = _split_bf16(wr_ref[0])
    nt = (((1,), (1,)), ((), ()))
    logits = (lax.dot_general(wh, uh, nt, preferred_element_type=F32)
              + lax.dot_general(wh, ul, nt, preferred_element_type=F32)
              + lax.dot_general(wl, uh, nt, preferred_element_type=F32))
    m = jnp.max(logits, axis=0, keepdims=True)
    ex = jnp.exp(logits - m)
    a_ref[0] = ex / jnp.sum(ex, axis=0, keepdims=True)


def _router(x, mods_all, layer, w_router_t, is_ctx, nb):
    n, t, d = x.shape
    e = w_router_t.shape[1]
    tm = min(t, ROW_TILE)
    kern = functools.partial(_router_kernel, is_ctx=is_ctx, nb=nb)
    tok = pl.BlockSpec((1, tm, d), lambda b, i: (b, i, 0))
    return pl.pallas_call(
        kern,
        grid=(n, t // tm),
        in_specs=[tok, _mod_spec(layer, d, 3), _mod_spec(layer, d, 4),
                  pl.BlockSpec((1, e, d), lambda b, i: (layer, 0, 0))],
        out_specs=[tok, pl.BlockSpec((1, e, tm), lambda b, i: (b, 0, i))],
        out_shape=[jax.ShapeDtypeStruct((n, t, d), F32), jax.ShapeDtypeStruct((n, e, t), F32)],
        compiler_params=_params(("arbitrary", "arbitrary")),
        name="router",
    )(x, mods_all, mods_all, w_router_t)


def _topk_kernel(a_ref, pt_ref, idx_ref, gate_ref, *, cap):
    a = a_ref[0]
    e, t = a.shape
    capf = jnp.float32(cap)

    def count(mask):
        return jnp.sum(jnp.where(mask, 1.0, 0.0), axis=1, keepdims=True)

    def as_row_values(bits):
        return jnp.concatenate([pltpu.bitcast(bits, F32)] * (t // LANES), axis=1)

    thr_bits = jnp.zeros((e, LANES), jnp.int32)
    for bit in range(30, -1, -1):
        cand = thr_bits | jnp.int32(1 << bit)
        keep = count(a >= as_row_values(cand)) >= capf
        thr_bits = jnp.where(keep, cand, thr_bits)
    thr = as_row_values(thr_bits)
    gt = a > thr
    eq = a == thr
    need = capf - count(gt)
    tri = jnp.where(_iota((t, t), 0) <= _iota((t, t), 1), 1.0, 0.0).astype(BF16)
    eq_f = jnp.where(eq, 1.0, 0.0)
    eq_rank = jnp.dot(eq_f.astype(BF16), tri, preferred_element_type=F32) - eq_f
    sel = jnp.where(gt, 1.0, jnp.where(eq & (eq_rank < need), 1.0, 0.0))
    slot = jnp.dot(sel.astype(BF16), tri, preferred_element_type=F32) - 1.0
    pos = jnp.where(sel > 0.0, slot, -1.0)
    padded = jnp.concatenate([pos, jnp.full((LANES - e, t), -1.0, F32)], axis=0)
    pos_tm = padded.T
    pt_ref[0] = pos_tm.astype(jnp.int32)
    tok = _iota((1, t), 1)
    tok_hi = (tok >> TOK_SPLIT_BITS).astype(F32)
    tok_lo = (tok & ((1 << TOK_SPLIT_BITS) - 1)).astype(F32)
    slots = _iota((1, cap), 1).astype(F32)
    for ei in range(e):
        onehot_t = jnp.where(pos_tm[:, ei:ei + 1] == slots, 1.0, 0.0).astype(BF16)
        g = a[ei:ei + 1, :]
        g_hi = g.astype(BF16).astype(F32)
        g_mid = (g - g_hi).astype(BF16).astype(F32)
        g_lo = (g - g_hi) - g_mid
        lhs = jnp.concatenate([tok_hi, tok_lo, g_hi, g_mid, g_lo, jnp.zeros((SUBLANES - 5, t), F32)], axis=0)
        res = jnp.dot(lhs.astype(BF16), onehot_t, preferred_element_type=F32)
        idx_ref[0, ei:ei + 1, :] = (res[0:1] * float(1 << TOK_SPLIT_BITS) + res[1:2]).astype(jnp.int32)
        gate_ref[0, ei:ei + 1, :] = res[2:3] + res[3:4] + res[4:5]


def _topk(aff_t, cap):
    n, e, t = aff_t.shape
    kern = functools.partial(_topk_kernel, cap=cap)
    return pl.pallas_call(
        kern,
        grid=(n,),
        in_specs=[pl.BlockSpec((1, e, t), lambda b: (b, 0, 0))],
        out_specs=[pl.BlockSpec((1, t, LANES), lambda b: (b, 0, 0)),
                   pl.BlockSpec((1, e, cap), lambda b: (b, 0, 0)),
                   pl.BlockSpec((1, e, cap), lambda b: (b, 0, 0))],
        out_shape=[jax.ShapeDtypeStruct((n, t, LANES), jnp.int32), jax.ShapeDtypeStruct((n, e, cap), jnp.int32),
                   jax.ShapeDtypeStruct((n, e, cap), F32)],
        compiler_params=_params(("arbitrary",)),
        name="topk",
    )(aff_t)


def _ffn_kernel(idx_ref, *refs, rows, n_exp):
    n_streams = len(rows)
    src_refs = refs[:n_streams]
    gate_ref, wg_ref, wu_ref, wo_ref = refs[n_streams:n_streams + 4]
    y_refs = refs[n_streams + 4:2 * n_streams + 4]
    act_s, wo_s, x_s, land_s, sem = refs[2 * n_streams + 4:]
    e = pl.program_id(0)
    fc = pl.program_id(1)
    nfc = act_s.shape[0]
    r_tot = sum(rows)
    bounds = [0]
    for r in rows:
        bounds.append(bounds[-1] + r)

    def row_copy(expert, row, k):
        tok = idx_ref[expert * r_tot + row]
        return pltpu.make_async_copy(src_refs[k].at[pl.ds(tok, 1), :], land_s.at[pl.ds(row, 1), :], sem)

    def wait_all_rows():
        pltpu.make_async_copy(src_refs[0].at[pl.ds(0, r_tot), :], land_s, sem).wait()

    @pl.when((e == 0) & (fc == 0))
    def _():
        for k in range(n_streams):
            def issue(row, carry, k=k):
                row_copy(0, row, k).start()
                return carry
            lax.fori_loop(bounds[k], bounds[k + 1], issue, 0)

    @pl.when(fc == 0)
    def _():
        wait_all_rows()
        x_s[...] = land_s[...].astype(BF16)

    nxt = lax.rem(e + 1, n_exp)
    for k in range(n_streams):
        share = rows[k] // nfc
        for i in range(share):
            row_copy(nxt, bounds[k] + fc * share + i, k).start()

    x = x_s[...]
    g = jnp.dot(x, wg_ref[0, 0].astype(BF16), preferred_element_type=F32)
    up = jnp.dot(x, wu_ref[0, 0].astype(BF16), preferred_element_type=F32)
    act_s[fc] = ((g * jax.nn.sigmoid(g)) * up).astype(BF16)
    wo_s[fc] = wo_ref[0, 0].astype(BF16)

    @pl.when(fc == nfc - 1)
    def _():
        w_out = wo_s[...].reshape(nfc * wo_s.shape[1], wo_s.shape[2])
        gate_col = jnp.broadcast_to(gate_ref[0], (LANES, r_tot)).T[:, 0:1]
        half = r_tot // 2
        for r0, r1 in ((0, half), (half, r_tot)):
            act = jnp.concatenate([act_s[c, r0:r1, :] for c in range(nfc)], axis=1)
            y = (jnp.dot(act, w_out, preferred_element_type=F32) * gate_col[r0:r1]).astype(BF16)
            for k in range(n_streams):
                lo, hi = max(r0, bounds[k]), min(r1, bounds[k + 1])
                if lo < hi:
                    y_refs[k][0, lo - bounds[k]:hi - bounds[k], :] = y[lo - r0:hi - r0]

    @pl.when((e == n_exp - 1) & (fc == nfc - 1))
    def _():
        wait_all_rows()


def _ffn(streams, idx, gates, w_in, w_out, layer):
    d = streams[0].shape[2]
    e = gates.shape[0]
    ff = w_out.shape[2]
    nfc = ff // FF_TILE
    rows = tuple(s.shape[0] * (EC_CAPACITY_FACTOR * s.shape[1] // N_EXPERTS) for s in streams)
    r_tot = sum(rows)
    assert all(r % nfc == 0 for r in rows) and r_tot % LANES == 0
    return pl.pallas_call(
        functools.partial(_ffn_kernel, rows=rows, n_exp=e),
        grid_spec=pltpu.PrefetchScalarGridSpec(
            num_scalar_prefetch=1,
            grid=(e, nfc),
            in_specs=[pl.BlockSpec(memory_space=pl.ANY) for _ in streams] + [
                pl.BlockSpec((1, 1, r_tot), lambda ei, fc, ix: (ei, 0, 0)),
                pl.BlockSpec((1, 1, d, FF_TILE), lambda ei, fc, ix: (layer, ei, 0, fc)),
                pl.BlockSpec((1, 1, d, FF_TILE), lambda ei, fc, ix: (layer, ei, 0, nfc + fc)),
                pl.BlockSpec((1, 1, FF_TILE, d), lambda ei, fc, ix: (layer, ei, fc, 0))],
            out_specs=[pl.BlockSpec((1, r, d), lambda ei, fc, ix: (ei, 0, 0)) for r in rows],
            scratch_shapes=[pltpu.VMEM((nfc, r_tot, FF_TILE), BF16), pltpu.VMEM((nfc, FF_TILE, d), BF16),
                            pltpu.VMEM((r_tot, d), BF16), pltpu.VMEM((r_tot, d), F32),
                            pltpu.SemaphoreType.DMA(())],
        ),
        out_shape=[jax.ShapeDtypeStruct((e, r, d), BF16) for r in rows],
        compiler_params=_params(("arbitrary", "arbitrary")),
        name="expert_ffn",
    )(idx, *[s.reshape(s.shape[0] * s.shape[1], d) for s in streams], gates, w_in, w_in, w_out)


def _combine_kernel(y_ref, pt_ref, x_ref, g_ref, lg_ref, lb_ref, o_ref, oh_s, *, is_ctx, nb, cap):
    n_exp, _, d = y_ref.shape
    ec = n_exp * cap
    cw = min(ec, COMBINE_COLS)
    pos = pt_ref[0].astype(F32).astype(BF16)
    for c0 in range(0, ec, cw):
        src = _iota((LANES, cw), 0)
        dst = _iota((LANES, cw), 1) + c0
        expand = jnp.where((dst >= src * cap) & (dst < (src + 1) * cap), 1.0, 0.0).astype(BF16)
        pe = jnp.dot(pos, expand, preferred_element_type=F32)
        slot = (_iota((1, cw), 1) + c0) % cap
        oh_s[:, c0:c0 + cw] = jnp.where(pe == slot.astype(F32), 1.0, 0.0).astype(BF16)
    f = jnp.dot(oh_s[...], y_ref[...].reshape(ec, d), preferred_element_type=F32)
    gate = g_ref[0, pl.ds(_mod_row(is_ctx, nb), 1), :]
    o_ref[0] = _layer_norm(DEEPNORM_ALPHA * x_ref[0] + gate * f, lg_ref[...], lb_ref[...])


def _combine(y, pos_tm, x, mods_all, layer, ln_g, ln_b, cap, is_ctx, nb):
    n, t, d = x.shape
    e = y.shape[0]
    tm = min(t, ROW_TILE)
    kern = functools.partial(_combine_kernel, is_ctx=is_ctx, nb=nb, cap=cap)
    tok = pl.BlockSpec((1, tm, d), lambda b, i: (b, i, 0))
    return pl.pallas_call(
        kern,
        grid=(n, t // tm),
        in_specs=[
            pl.BlockSpec((e, cap, d), lambda b, i: (0, b, 0), pipeline_mode=pl.Buffered(1)),
            pl.BlockSpec((1, tm, LANES), lambda b, i: (b, i, 0)),
            tok,
            _mod_spec(layer, d, 5),
            _vec_spec(d), _vec_spec(d),
        ],
        out_specs=tok,
        out_shape=jax.ShapeDtypeStruct((n, t, d), F32),
        scratch_shapes=[pltpu.VMEM((tm, e * cap), BF16)],
        compiler_params=_params(("arbitrary", "arbitrary")),
        name="combine",
    )(y, pos_tm, x, mods_all, ln_g.reshape(1, d), ln_b.reshape(1, d))


def _moe_post(x1_lat, x1_ctx, mods_all, layer, w_router_t, w_in, w_out, ln_g, ln_b, nb):
    streams = [(x1_lat, False)] + ([(x1_ctx, True)] if x1_ctx is not None else [])
    caps = [EC_CAPACITY_FACTOR * s.shape[1] // N_EXPERTS for s, _ in streams]
    routed, tokens, idx_parts, gate_parts = [], [], [], []
    for (s, is_ctx), cap in zip(streams, caps):
        n, t = s.shape[:2]
        u2, aff_t = _router(s, mods_all, layer, w_router_t, is_ctx, nb)
        pos_tm, idx, gate = _topk(aff_t, cap)
        routed.append(pos_tm)
        tokens.append(u2)
        flat = idx + (jnp.arange(n, dtype=jnp.int32) * t)[:, None, None]
        idx_parts.append(jnp.swapaxes(flat, 0, 1).reshape(N_EXPERTS, n * cap))
        gate_parts.append(jnp.swapaxes(gate, 0, 1).reshape(N_EXPERTS, n * cap))
    idx_all = jnp.concatenate(idx_parts, axis=1).reshape(-1)
    gates = jnp.concatenate(gate_parts, axis=1)[:, None, :]
    ys = _ffn(tokens, idx_all, gates, w_in, w_out, layer)
    outs = [_combine(y, pos_tm, s, mods_all, layer, ln_g, ln_b, cap, is_ctx, nb)
            for (s, is_ctx), pos_tm, cap, y in zip(streams, routed, caps, ys)]
    return outs[0], (outs[1] if len(outs) > 1 else None)


def kernel(x, c, ctx, c_ctx, ada_w, ada_b, ln_g, ln_b, s5_a_re, s5_a_im, s5_log_dt, s5_b_re, s5_b_im,
           s5_c_re, s5_c_im, s5_d, s5_w_glu, s5_b_glu, cv_w_pw1, cv_b_pw1, cv_w_dw, cv_b_dw, cv_ln_g,
           cv_ln_b, cv_w_pw2, cv_b_pw2, moe_w_router, moe_w_in, moe_w_out):
    nb, seq, d = x.shape
    assert nb + 1 <= MOD_ROWS and nb % 2 == 0 and d % LANES == 0
    rows = seq // GRID_W

    c8 = jnp.concatenate([c.astype(F32), c_ctx.astype(F32)[None], jnp.zeros((MOD_ROWS - nb - 1, d), F32)], axis=0)
    mods_all = _ada_all(c8, ada_w, ada_b)
    w_glu_bf = s5_w_glu.astype(BF16)
    w_pw1_bf = cv_w_pw1.astype(BF16)
    w_pw2_bf = cv_w_pw2.astype(BF16)
    w_router_t = jnp.swapaxes(moe_w_router, 1, 2)

    x_lat, x_ctx = x, ctx
    for i in range(DEPTH):
        is_s5 = (i % N_MIXERS) == 0
        j = i // N_MIXERS
        ctx_out = any((k % N_MIXERS) == 0 for k in range(i + 1, DEPTH))
        x1_ctx = None

        if is_s5:
            ops = _s5_operators(s5_a_re[j], s5_a_im[j], s5_log_dt[j], s5_b_re[j], s5_b_im[j],
                                s5_c_re[j], s5_c_im[j])
            z_lat, z_ctx = _s5_core(x_lat, x_ctx, mods_all, i, s5_d[j], ops, ctx_out)
            x1_lat = _s5_out(z_lat, w_glu_bf, j, s5_b_glu[j], x_lat, mods_all, i, ln_g[i, 0], ln_b[i, 0], False, nb)
            if ctx_out:
                x1_ctx = _s5_out(z_ctx, w_glu_bf, j, s5_b_glu[j], x_ctx, mods_all, i, ln_g[i, 0], ln_b[i, 0],
                                 True, nb)
        else:
            conv_args = (cv_w_dw[j], cv_b_dw[j], cv_ln_g[j], cv_ln_b[j], w_pw2_bf, j, cv_b_pw2[j])
            h_lat = _pw1(x_lat, mods_all, i, w_pw1_bf, j, cv_b_pw1[j], False, nb)
            x1_lat = _conv_post(h_lat, *conv_args, x_lat, mods_all, i, ln_g[i, 0], ln_b[i, 0], rows, False, nb)
            if ctx_out:
                h_ctx = _pw1(x_ctx, mods_all, i, w_pw1_bf, j, cv_b_pw1[j], True, nb)
                x1_ctx = _conv_post(h_ctx, *conv_args, x_ctx, mods_all, i, ln_g[i, 0], ln_b[i, 0], None, True, nb)

        x_lat, x_ctx_new = _moe_post(x1_lat, x1_ctx, mods_all, i, w_router_t, moe_w_in, moe_w_out,
                                     ln_g[i, 1], ln_b[i, 1], nb)
        if ctx_out:
            x_ctx = x_ctx_new
    return x_lat
```

```python
import functools

import jax
import jax.numpy as jnp
from jax import lax
from jax.experimental import pallas as pl
from jax.experimental.pallas import tpu as pltpu

F32 = jnp.float32
BF16 = jnp.bfloat16

DEPTH = 4
N_MIXERS = 2
GRID_W = 64
S5_H = 16
S5_P = 64
N_EXPERTS = 16
EC_CAPACITY_FACTOR = 2
DEEPNORM_ALPHA = (2.0 * DEPTH) ** 0.25
LN_EPS = 1e-5

LANES = 128
SUBLANES = 8
S5_CHUNK = 8
S5_GPT = LANES // S5_H
S5_STATE_COLS = S5_GPT * S5_P
MOD_ROWS = 8
VMEM_LIMIT = 56 * 1024 * 1024
ROW_TILE = 512
COMBINE_COLS = 1024
TOK_SPLIT_BITS = 6
FF_TILE = 256
CONV_WT = 2 * SUBLANES


def _params(sem, vmem=VMEM_LIMIT):
    return pltpu.CompilerParams(dimension_semantics=sem, vmem_limit_bytes=vmem)


def _layer_norm(v, g, b):
    mu = jnp.mean(v, axis=-1, keepdims=True)
    c = v - mu
    var = jnp.mean(c * c, axis=-1, keepdims=True)
    return c * lax.rsqrt(var + LN_EPS) * g + b


def _split_bf16(v):
    hi = v.astype(BF16)
    lo = (v - hi.astype(F32)).astype(BF16)
    return hi, lo


def _mod_spec(layer, d, k):
    return pl.BlockSpec((1, MOD_ROWS, d), lambda *_: (layer, 0, k))


def _vec_spec(d):
    return pl.BlockSpec((1, d), lambda *_: (0, 0))


def _mod_row(is_ctx, nb):
    return nb if is_ctx else pl.program_id(0)


def _ada_kernel(c_ref, w_ref, b_ref, o_ref):
    c = c_ref[...]
    cond = c * jax.nn.sigmoid(c)
    hi, lo = _split_bf16(cond)
    lhs = jnp.concatenate([hi, lo], axis=0)
    r = jnp.dot(lhs, w_ref[0].astype(BF16), preferred_element_type=F32)
    o_ref[0] = r[:MOD_ROWS] + r[MOD_ROWS:] + b_ref[0]


def _ada_all(c8, ada_w, ada_b):
    depth, d, n = ada_w.shape
    tn = 1024
    return pl.pallas_call(
        _ada_kernel,
        grid=(depth, n // tn),
        in_specs=[
            pl.BlockSpec((MOD_ROWS, d), lambda i, k: (0, 0)),
            pl.BlockSpec((1, d, tn), lambda i, k: (i, 0, k)),
            pl.BlockSpec((1, 1, tn), lambda i, k: (i, 0, k)),
        ],
        out_specs=pl.BlockSpec((1, MOD_ROWS, tn), lambda i, k: (i, 0, k)),
        out_shape=jax.ShapeDtypeStruct((depth, MOD_ROWS, n), F32),
        compiler_params=_params(("arbitrary", "arbitrary")),
        name="adaln",
    )(c8, ada_w, ada_b.reshape(depth, 1, n))


def _cmul(ar, ai, br, bi):
    return ar * br - ai * bi, ar * bi + ai * br


def _s5_operators(a_re, a_im, log_dt, b_re, b_im, c_re, c_im):
    hp = lax.Precision.HIGHEST
    t = S5_CHUNK
    g = a_re.shape[1]
    nj = g // S5_GPT
    lam_r, lam_i = a_re.astype(F32), a_im.astype(F32)
    dt = jnp.exp(log_dt.astype(F32))[..., None]
    mag = jnp.exp(lam_r * dt)
    abar_r, abar_i = mag * jnp.cos(lam_i * dt), mag * jnp.sin(lam_i * dt)
    den = lam_r * lam_r + lam_i * lam_i
    xr, xi = abar_r - 1.0, abar_i
    coef_r, coef_i = (xr * lam_r + xi * lam_i) / den, (xi * lam_r - xr * lam_i) / den
    bb_r, bb_i = _cmul(coef_r[..., None], coef_i[..., None], b_re.astype(F32), b_im.astype(F32))
    cm_r, cm_i = c_re.astype(F32), c_im.astype(F32)
    pr, pi = [jnp.ones_like(abar_r)], [jnp.zeros_like(abar_r)]
    for _ in range(t):
        nr, ni = _cmul(pr[-1], pi[-1], abar_r, abar_i)
        pr.append(nr)
        pi.append(ni)
    pr, pi = jnp.stack(pr), jnp.stack(pi)

    def lag_kernels(d):
        qr, qi = _cmul(pr[:t, d, :, :, None], pi[:t, d, :, :, None], bb_r[d][None], bb_i[d][None])
        return (jnp.einsum('ghp,kgpj->kghj', cm_r[d], qr, precision=hp)
                - jnp.einsum('ghp,kgpj->kghj', cm_i[d], qi, precision=hp))

    kf, kb = lag_kernels(0), lag_kernels(1)
    kall = jnp.concatenate([kb[:0:-1], (kf[0] + kb[0])[None], kf[1:]], axis=0)
    nlag = 2 * t - 1
    kl = kall.reshape(nlag, nj, S5_GPT, S5_H, S5_H).transpose(1, 0, 2, 4, 3).reshape(nj, nlag * LANES, S5_H)

    def per_tile(re, im, perm, rows, cols):
        both = jnp.stack([re, im], axis=1)
        both = both.reshape(2, 2, nj, S5_GPT, both.shape[-2], both.shape[-1])
        return both.transpose(perm).reshape(nj, rows, cols)

    bt = per_tile(bb_r, bb_i, (2, 0, 1, 3, 5, 4), 4 * LANES, S5_P)
    ct = per_tile(cm_r, cm_i, (2, 0, 1, 3, 5, 4), 4 * S5_STATE_COLS, S5_H)

    exps_b = (jnp.arange(t - 1, -1, -1), jnp.arange(t))
    exps_c = (jnp.arange(1, t + 1), jnp.arange(t, 0, -1))

    def powers(exps):
        both = jnp.stack([jnp.stack([pr[exps[d], d], pi[exps[d], d]]) for d in range(2)])
        return both.reshape(2, 2, t, nj, S5_STATE_COLS)

    pw = powers(exps_b).transpose(3, 0, 1, 2, 4).reshape(nj, 4 * t, S5_STATE_COLS)
    pc = powers(exps_c).transpose(3, 0, 1, 4, 2).reshape(nj, 4 * S5_STATE_COLS, t)

    def decay_tiles(d):
        ar = pr[t, d].reshape(nj, S5_STATE_COLS // LANES, LANES)
        ai = pi[t, d].reshape(nj, S5_STATE_COLS // LANES, LANES)
        return jnp.concatenate([ar, ar, -ai, ai], axis=1)

    at = jnp.concatenate([decay_tiles(0), decay_tiles(1)], axis=1)
    return kl, bt, ct, pw, pc, at


def _iota(shape, axis):
    return lax.broadcasted_iota(jnp.int32, shape, axis)


def _group_blockdiag(comp, width, rows_per_group):
    k = comp.shape[1]
    n = S5_GPT * k
    rep = jnp.where((_iota((k, n), 1) & (k - 1)) == _iota((k, n), 0), 1.0, 0.0).astype(BF16)
    hi = comp.astype(BF16)
    rest = comp - hi.astype(F32)
    mid = rest.astype(BF16)
    lo = (rest - mid.astype(F32)).astype(BF16)
    tiled = (jnp.dot(hi, rep, preferred_element_type=F32) + jnp.dot(mid, rep, preferred_element_type=F32)
             + jnp.dot(lo, rep, preferred_element_type=F32))
    row_grp = (_iota((comp.shape[0], 1), 0) >> (rows_per_group.bit_length() - 1)) & (S5_GPT - 1)
    col_grp = _iota((1, n), 1) >> (width.bit_length() - 1)
    return jnp.where(row_grp == col_grp, tiled, 0.0)


def _build_s5_operators(kl_ref, bt_ref, ct_ref, pw_ref, pc_ref, wm_s, wb_s, wc_s):
    t = S5_CHUNK
    sc = S5_STATE_COLS
    lagk = _group_blockdiag(kl_ref[0], S5_H, S5_H).astype(BF16)
    for s in range(t):
        for u in range(t):
            lag = u - s + t - 1
            wm_s[s * LANES:(s + 1) * LANES, u * LANES:(u + 1) * LANES] = lagk[lag * LANES:(lag + 1) * LANES, :]
    bexp = _group_blockdiag(bt_ref[0], S5_P, S5_H)
    for d in range(2):
        b_re = bexp[(2 * d) * LANES:(2 * d + 1) * LANES]
        b_im = bexp[(2 * d + 1) * LANES:(2 * d + 2) * LANES]
        for s in range(t):
            p_re = pw_ref[0, (2 * d) * t + s:(2 * d) * t + s + 1, :]
            p_im = pw_ref[0, (2 * d + 1) * t + s:(2 * d + 1) * t + s + 1, :]
            rows = slice(s * LANES, (s + 1) * LANES)
            wb_s[rows, d * 2 * sc:d * 2 * sc + sc] = (b_re * p_re - b_im * p_im).astype(BF16)
            wb_s[rows, d * 2 * sc + sc:(d + 1) * 2 * sc] = (b_re * p_im + b_im * p_re).astype(BF16)
    cexp = _group_blockdiag(ct_ref[0], S5_H, S5_P)
    for d in range(2):
        c_re = cexp[(2 * d) * sc:(2 * d + 1) * sc]
        c_im = cexp[(2 * d + 1) * sc:(2 * d + 2) * sc]
        for u in range(t):
            p_re = pc_ref[0, (2 * d) * sc:(2 * d + 1) * sc, u:u + 1]
            p_im = pc_ref[0, (2 * d + 1) * sc:(2 * d + 2) * sc, u:u + 1]
            cols = slice(u * LANES, (u + 1) * LANES)
            wc_s[d * 2 * sc:d * 2 * sc + sc, cols] = (c_re * p_re - c_im * p_im).astype(BF16)
            wc_s[d * 2 * sc + sc:(d + 1) * 2 * sc, cols] = (-(c_re * p_im + c_im * p_re)).astype(BF16)


def _s5_kernel(x_ref, xc_ref, sh_ref, sc_ref, d_ref, kl_ref, bt_ref, ct_ref, pw_ref, pc_ref, at_ref, *rest,
               nb_half, n_lat, n_ctx, ctx_out):
    if ctx_out:
        z_ref, zc_ref, wm_s, wb_s, wc_s, xf_s, st_s, en_s, y_s, zb_s, zcb_s = rest
    else:
        z_ref, wm_s, wb_s, wc_s, xf_s, st_s, en_s, y_s, zb_s = rest
        zc_ref = zcb_s = None
    t = S5_CHUNK
    ncl = n_lat // t
    ncc = n_ctx // t
    lat_rows = nb_half * ncl
    half = pl.program_id(1)
    sc_cols = S5_STATE_COLS
    nst = sc_cols // LANES
    n_rows = nb_half * (ncl + ncc)

    @pl.when(half == 0)
    def _():
        _build_s5_operators(kl_ref, bt_ref, ct_ref, pw_ref, pc_ref, wm_s, wb_s, wc_s)

    for k in range(nb_half):
        b = half * nb_half + k
        scale = 1.0 + sc_ref[0, pl.ds(b, 1), :]
        shift = sh_ref[0, pl.ds(b, 1), :]
        for s in range(t):
            xf_s[k * ncl:(k + 1) * ncl, s * LANES:(s + 1) * LANES] = (
                x_ref[pl.ds(k * n_lat + s, ncl, stride=t), :] * scale + shift)
    scale_c = 1.0 + sc_ref[0, nb_half * 2:nb_half * 2 + 1, :]
    shift_c = sh_ref[0, nb_half * 2:nb_half * 2 + 1, :]
    for k in range(nb_half):
        for s in range(t):
            r0 = lat_rows + k * ncc
            xf_s[r0:r0 + ncc, s * LANES:(s + 1) * LANES] = (
                xc_ref[pl.ds(k * n_ctx + s, ncc, stride=t), :] * scale_c + shift_c)

    xb = xf_s[...].astype(BF16)
    dvec = jnp.concatenate([d_ref[...]] * t, axis=1)
    y_s[...] = jnp.dot(xb, wm_s[...], preferred_element_type=F32) + xf_s[...] * dvec

    for d in range(2):
        local = jnp.dot(xb, wb_s[:, d * 2 * sc_cols:(d + 1) * 2 * sc_cols], preferred_element_type=F32)
        for q in range(2 * nst):
            st_s[d, pl.ds(q, n_rows, stride=2 * nst), :] = local[:, q * LANES:(q + 1) * LANES]
    coef = [(at_ref[0, (2 * d) * 2 * nst:(2 * d + 1) * 2 * nst, :],
             at_ref[0, (2 * d + 1) * 2 * nst:(2 * d + 2) * 2 * nst, :]) for d in range(2)]

    def step(chunk_rows, h):
        new = []
        for d in range(2):
            for k in range(nb_half):
                tile = pl.ds(pl.multiple_of((chunk_rows[d] + k * chunk_rows[2]) * 2 * nst, 2 * nst), 2 * nst)
                prev, prev_sw = h[2 * (d * nb_half + k)], h[2 * (d * nb_half + k) + 1]
                local_state = st_s[d, tile, :]
                en_s[d, tile, :] = prev
                new.append(coef[d][0] * prev + coef[d][1] * prev_sw + local_state)
                new.append(coef[d][0] * prev_sw - coef[d][1] * prev + pltpu.roll(local_state, nst, axis=0))
        return tuple(new)

    zero = tuple(jnp.zeros((2 * nst, LANES), F32) for _ in range(4 * nb_half))
    h = lax.fori_loop(0, ncc, lambda i, h: step((lat_rows + i, lat_rows + ncc - 1 - i, ncc), h), zero, unroll=4)
    lax.fori_loop(0, ncl, lambda i, h: step((i, ncl - 1 - i, ncl), h), h, unroll=4)

    entering = jnp.concatenate([en_s[d, pl.ds(q, n_rows, stride=2 * nst), :]
                                for d in range(2) for q in range(2 * nst)], axis=1)
    y_s[...] += jnp.dot(entering.astype(BF16), wc_s[...], preferred_element_type=F32)

    z = jax.nn.gelu(y_s[...])
    for k in range(nb_half):
        for s in range(t):
            zb_s[pl.ds(k * n_lat + s, ncl, stride=t), :] = z[k * ncl:(k + 1) * ncl, s * LANES:(s + 1) * LANES]
    z_ref[...] = zb_s[...].astype(BF16)
    if ctx_out:
        for k in range(nb_half):
            for s in range(t):
                r0 = lat_rows + k * ncc
                zcb_s[pl.ds(k * n_ctx + s, ncc, stride=t), :] = z[r0:r0 + ncc, s * LANES:(s + 1) * LANES]
        zc_ref[...] = zcb_s[...].astype(BF16)


def _s5_core(x, xc, mods_all, layer, d_skip, ops, ctx_out):
    nb, n_lat, d = x.shape
    n_ctx = xc.shape[1]
    nj = d // LANES
    nb_half = nb // 2
    t = S5_CHUNK
    tl = t * LANES
    rows = nb_half * (n_lat + n_ctx) // t
    kern = functools.partial(_s5_kernel, nb_half=nb_half, n_lat=n_lat, n_ctx=n_ctx, ctx_out=ctx_out)
    out_shape = [jax.ShapeDtypeStruct((nb * n_lat, d), BF16)]
    out_specs = [pl.BlockSpec((nb_half * n_lat, LANES), lambda j, h: (h, j))]
    scratch = [pltpu.VMEM((tl, tl), BF16), pltpu.VMEM((tl, 4 * S5_STATE_COLS), BF16),
               pltpu.VMEM((4 * S5_STATE_COLS, tl), BF16),
               pltpu.VMEM((rows, tl), F32), pltpu.VMEM((2, rows * 2 * S5_STATE_COLS // LANES, LANES), F32),
               pltpu.VMEM((2, rows * 2 * S5_STATE_COLS // LANES, LANES), F32),
               pltpu.VMEM((rows, tl), F32), pltpu.VMEM((nb_half * n_lat, LANES), F32)]
    if ctx_out:
        out_shape.append(jax.ShapeDtypeStruct((nb * n_ctx, d), BF16))
        out_specs.append(pl.BlockSpec((nb_half * n_ctx, LANES), lambda j, h: (h, j)))
        scratch.append(pltpu.VMEM((nb_half * n_ctx, LANES), F32))
    res = pl.pallas_call(
        kern,
        grid=(nj, 2),
        in_specs=[
            pl.BlockSpec((nb_half * n_lat, LANES), lambda j, h: (h, j)),
            pl.BlockSpec((nb_half * n_ctx, LANES), lambda j, h: (h, j)),
            pl.BlockSpec((1, MOD_ROWS, LANES), lambda j, h: (layer, 0, j)),
            pl.BlockSpec((1, MOD_ROWS, LANES), lambda j, h: (layer, 0, nj + j)),
            pl.BlockSpec((1, LANES), lambda j, h: (0, j)),
        ] + [pl.BlockSpec((1,) + op.shape[1:], lambda j, h: (j, 0, 0)) for op in ops],
        out_specs=out_specs,
        out_shape=out_shape,
        scratch_shapes=scratch,
        compiler_params=_params(("arbitrary", "arbitrary")),
        name="s5_core",
    )(x.reshape(nb * n_lat, d), xc.reshape(nb * n_ctx, d), mods_all, mods_all, d_skip.reshape(1, d), *ops)
    z = res[0].reshape(nb, n_lat, d)
    zc = res[1].reshape(nb, n_ctx, d) if ctx_out else None
    return z, zc


def _s5_out_kernel(z_ref, w_ref, b_ref, x_ref, g_ref, lg_ref, lb_ref, o_ref, *, is_ctx, nb):
    d = x_ref.shape[-1]
    acc = jnp.dot(z_ref[0], w_ref[0], preferred_element_type=F32) + b_ref[...]
    y = acc[:, :d] * jax.nn.sigmoid(acc[:, d:])
    gate = g_ref[0, pl.ds(_mod_row(is_ctx, nb), 1), :]
    o_ref[0] = _layer_norm(DEEPNORM_ALPHA * x_ref[0] + gate * y, lg_ref[...], lb_ref[...])


def _s5_out(z, w_bf, j, b_glu, x, mods_all, layer, ln_g, ln_b, is_ctx, nb):
    n, t, d = x.shape
    tm = min(t, ROW_TILE)
    kern = functools.partial(_s5_out_kernel, is_ctx=is_ctx, nb=nb)
    tok = pl.BlockSpec((1, tm, d), lambda b, i: (b, i, 0))
    return pl.pallas_call(
        kern,
        grid=(n, t // tm),
        in_specs=[
            tok,
            pl.BlockSpec((1, d, 2 * d), lambda b, i: (j, 0, 0), pipeline_mode=pl.Buffered(1)),
            _vec_spec(2 * d),
            tok,
            _mod_spec(layer, d, 2),
            _vec_spec(d), _vec_spec(d),
        ],
        out_specs=tok,
        out_shape=jax.ShapeDtypeStruct((n, t, d), F32),
        compiler_params=_params(("arbitrary", "arbitrary")),
        name="s5_out",
    )(z, w_bf, b_glu.reshape(1, 2 * d), x, mods_all, ln_g.reshape(1, d), ln_b.reshape(1, d))


def _pw1_kernel(x_ref, sh_ref, sc_ref, w_ref, b_ref, o_ref, *, is_ctx, nb):
    d = x_ref.shape[-1]
    row = _mod_row(is_ctx, nb)
    u = x_ref[0] * (1.0 + sc_ref[0, pl.ds(row, 1), :]) + sh_ref[0, pl.ds(row, 1), :]
    acc = jnp.dot(u.astype(BF16), w_ref[0], preferred_element_type=F32) + b_ref[...]
    o_ref[0] = acc[:, :d] * jax.nn.sigmoid(acc[:, d:])


def _pw1(x, mods_all, layer, w_bf, j, b_pw1, is_ctx, nb):
    n, t, d = x.shape
    tm = min(t, ROW_TILE)
    kern = functools.partial(_pw1_kernel, is_ctx=is_ctx, nb=nb)
    tok = pl.BlockSpec((1, tm, d), lambda b, i: (b, i, 0))
    return pl.pallas_call(
        kern,
        grid=(n, t // tm),
        in_specs=[
            tok,
            _mod_spec(layer, d, 0), _mod_spec(layer, d, 1),
            pl.BlockSpec((1, d, 2 * d), lambda b, i: (j, 0, 0), pipeline_mode=pl.Buffered(1)),
            _vec_spec(2 * d),
        ],
        out_specs=tok,
        out_shape=jax.ShapeDtypeStruct((n, t, d), F32),
        compiler_params=_params(("arbitrary", "arbitrary")),
        name="conv_pw1",
    )(x, mods_all, mods_all, w_bf, b_pw1.reshape(1, 2 * d))


def _conv_tail(cv, cg_ref, cb_ref, w2_ref, b2_ref, x, gate, lg_ref, lb_ref):
    hn = _layer_norm(cv, cg_ref[...], cb_ref[...])
    hn = hn * jax.nn.sigmoid(hn)
    y = jnp.dot(hn.astype(BF16), w2_ref[0], preferred_element_type=F32) + b2_ref[...]
    return _layer_norm(DEEPNORM_ALPHA * x + gate * y, lg_ref[...], lb_ref[...])


def _conv_lat_kernel(h_ref, wdw_ref, bdw_ref, cg_ref, cb_ref, w2_ref, b2_ref, x_ref, g_ref, lg_ref, lb_ref,
                     o_ref, wb_s, cv_s, *, n_rows, wt):
    kw = wdw_ref.shape[0]
    pad = kw // 2
    d = h_ref.shape[-1]
    nsub = wt // SUBLANES
    for k in range(kw):
        wb_s[k] = jnp.broadcast_to(wdw_ref[k:k + 1, :], (SUBLANES, d))
    bias = bdw_ref[...]
    for r in range(n_rows):
        accs = [None] * nsub
        for k in range(max(0, pad - r), min(kw, n_rows + pad - r)):
            w8 = wb_s[k]
            for q in range(nsub):
                term = w8 * h_ref[0, r + k - pad, q * SUBLANES:(q + 1) * SUBLANES, :]
                accs[q] = term if accs[q] is None else accs[q] + term
        for q in range(nsub):
            cv_s[r * wt + q * SUBLANES:r * wt + (q + 1) * SUBLANES, :] = accs[q] + bias
    gate = g_ref[0, pl.ds(pl.program_id(0), 1), :]
    x = x_ref[0].reshape(n_rows * wt, d)
    out = _conv_tail(cv_s[...], cg_ref, cb_ref, w2_ref, b2_ref, x, gate, lg_ref, lb_ref)
    o_ref[0] = out.reshape(n_rows, wt, d)


def _conv_ctx_kernel(h_ref, wdw_ref, bdw_ref, cg_ref, cb_ref, w2_ref, b2_ref, x_ref, g_ref, lg_ref, lb_ref,
                     o_ref, hp_s, cv_s, *, n_tok, nb):
    kw = wdw_ref.shape[0]
    pad = kw // 2
    d = h_ref.shape[-1]
    lead = 2 * SUBLANES
    rblk = 128
    cblk = 2 * LANES
    hp_s[0:lead] = jnp.zeros((lead, d), F32)
    hp_s[lead + n_tok:lead + n_tok + lead] = jnp.zeros((lead, d), F32)
    hp_s[lead:lead + n_tok] = h_ref[0]
    span = ((kw - 1 + lead - pad) // SUBLANES) * SUBLANES

    def col_block(lc, carry):
        cols = pl.ds(pl.multiple_of(lc * cblk, cblk), cblk)
        for rc in range(n_tok // rblk):
            acc = jnp.zeros((rblk, cblk), F32)
            for q in range(SUBLANES):
                taps = [k for k in range(kw) if (k + lead - pad) % SUBLANES == q]
                if not taps:
                    continue
                shifted = hp_s[pl.ds(rc * rblk + q, rblk + span), cols]
                for k in taps:
                    o = k + lead - pad - q
                    acc = acc + wdw_ref[k:k + 1, cols] * shifted[o:o + rblk]
            cv_s[rc * rblk:(rc + 1) * rblk, cols] = acc + bdw_ref[:, cols]
        return carry

    lax.fori_loop(0, d // cblk, col_block, 0)
    gate = g_ref[0, nb:nb + 1, :]
    o_ref[0] = _conv_tail(cv_s[...], cg_ref, cb_ref, w2_ref, b2_ref, x_ref[0], gate, lg_ref, lb_ref)


def _conv_post(h, w_dw, b_dw, cv_g, cv_b, w2_bf, j, b2, x, mods_all, layer, ln_g, ln_b, n_rows, is_ctx, nb):
    n, t, d = x.shape
    kw = w_dw.shape[0]
    weights = [pl.BlockSpec((kw, d), lambda b, i: (0, 0)), _vec_spec(d), _vec_spec(d), _vec_spec(d),
               pl.BlockSpec((1, d, d), lambda b, i: (j, 0, 0), pipeline_mode=pl.Buffered(1)), _vec_spec(d)]
    tail = [_mod_spec(layer, d, 2), _vec_spec(d), _vec_spec(d)]
    args_w = (w_dw, b_dw.reshape(1, d), cv_g.reshape(1, d), cv_b.reshape(1, d), w2_bf, b2.reshape(1, d))
    args_t = (mods_all, ln_g.reshape(1, d), ln_b.reshape(1, d))
    if is_ctx:
        blk = pl.BlockSpec((1, t, d), lambda b, i: (b, 0, 0))
        return pl.pallas_call(
            functools.partial(_conv_ctx_kernel, n_tok=t, nb=nb),
            grid=(n, 1),
            in_specs=[blk] + weights + [blk] + tail,
            out_specs=blk,
            out_shape=jax.ShapeDtypeStruct((n, t, d), F32),
            scratch_shapes=[pltpu.VMEM((t + 4 * SUBLANES, d), F32), pltpu.VMEM((t, d), F32)],
            compiler_params=_params(("arbitrary", "arbitrary")),
            name="conv_post_ctx",
        )(h, *args_w, x, *args_t)
    wt = CONV_WT
    width = t // n_rows
    blk = pl.BlockSpec((1, n_rows, wt, d), lambda b, i: (b, 0, i, 0))
    out = pl.pallas_call(
        functools.partial(_conv_lat_kernel, n_rows=n_rows, wt=wt),
        grid=(n, width // wt),
        in_specs=[blk] + weights + [blk] + tail,
        out_specs=blk,
        out_shape=jax.ShapeDtypeStruct((n, n_rows, width, d), F32),
        scratch_shapes=[pltpu.VMEM((kw, SUBLANES, d), F32),
                        pltpu.VMEM((n_rows * wt, d), F32)],
        compiler_params=_params(("arbitrary", "arbitrary")),
        name="conv_post",
    )(h.reshape(n, n_rows, width, d), *args_w, x.reshape(n, n_rows, width, d), *args_t)
    return out.reshape(n, t, d)


def _router_kernel(x_ref, sh_ref, sc_ref, wr_ref, a_ref, *, is_ctx, nb):
    row = _mod_row(is_ctx, nb)
    u = x_ref[0] * (1.0 + sc_ref[0, pl.ds(row, 1), :]) + sh_ref[0, pl.ds(row, 1), :]
    uh, ul = _split_bf16(u)
    wh, wl = _split_bf16(wr_ref[0])
    nt = (((1,), (1,)), ((), ()))
    logits = (lax.dot_general(wh, uh, nt, preferred_element_type=F32)
              + lax.dot_general(wh, ul, nt, preferred_element_type=F32)
              + lax.dot_general(wl, uh, nt, preferred_element_type=F32))
    m = jnp.max(logits, axis=0, keepdims=True)
    ex = jnp.exp(logits - m)
    a_ref[0] = ex / jnp.sum(ex, axis=0, keepdims=True)


def _router(x, mods_all, layer, w_router_t, is_ctx, nb):
    n, t, d = x.shape
    e = w_router_t.shape[1]
    tm = min(t, ROW_TILE)
    kern = functools.partial(_router_kernel, is_ctx=is_ctx, nb=nb)
    tok = pl.BlockSpec((1, tm, d), lambda b, i: (b, i, 0))
    return pl.pallas_call(
        kern,
        grid=(n, t // tm),
        in_specs=[tok, _mod_spec(layer, d, 3), _mod_spec(layer, d, 4),
                  pl.BlockSpec((1, e, d), lambda b, i: (layer, 0, 0))],
        out_specs=pl.BlockSpec((1, e, tm), lambda b, i: (b, 0, i)),
        out_shape=jax.ShapeDtypeStruct((n, e, t), F32),
        compiler_params=_params(("arbitrary", "arbitrary")),
        name="router",
    )(x, mods_all, mods_all, w_router_t)


def _topk_kernel(a_ref, pt_ref, idx_ref, gate_ref, *, cap):
    a = a_ref[0]
    e, t = a.shape
    capf = jnp.float32(cap)

    def count(mask):
        return jnp.sum(jnp.where(mask, 1.0, 0.0), axis=1, keepdims=True)

    def as_row_values(bits):
        return jnp.concatenate([pltpu.bitcast(bits, F32)] * (t // LANES), axis=1)

    thr_bits = jnp.zeros((e, LANES), jnp.int32)
    for bit in range(30, -1, -1):
        cand = thr_bits | jnp.int32(1 << bit)
        keep = count(a >= as_row_values(cand)) >= capf
        thr_bits = jnp.where(keep, cand, thr_bits)
    thr = as_row_values(thr_bits)
    gt = a > thr
    eq = a == thr
    need = capf - count(gt)
    tri = jnp.where(_iota((t, t), 0) <= _iota((t, t), 1), 1.0, 0.0).astype(BF16)
    eq_f = jnp.where(eq, 1.0, 0.0)
    eq_rank = jnp.dot(eq_f.astype(BF16), tri, preferred_element_type=F32) - eq_f
    sel = jnp.where(gt, 1.0, jnp.where(eq & (eq_rank < need), 1.0, 0.0))
    slot = jnp.dot(sel.astype(BF16), tri, preferred_element_type=F32) - 1.0
    pos = jnp.where(sel > 0.0, slot, -1.0)
    padded = jnp.concatenate([pos, jnp.full((LANES - e, t), -1.0, F32)], axis=0)
    pos_tm = padded.T
    pt_ref[0] = pos_tm.astype(jnp.int32)
    tok = _iota((1, t), 1)
    tok_hi = (tok >> TOK_SPLIT_BITS).astype(F32)
    tok_lo = (tok & ((1 << TOK_SPLIT_BITS) - 1)).astype(F32)
    slots = _iota((1, cap), 1).astype(F32)
    for ei in range(e):
        onehot_t = jnp.where(pos_tm[:, ei:ei + 1] == slots, 1.0, 0.0).astype(BF16)
        g = a[ei:ei + 1, :]
        g_hi = g.astype(BF16).astype(F32)
        g_mid = (g - g_hi).astype(BF16).astype(F32)
        g_lo = (g - g_hi) - g_mid
        lhs = jnp.concatenate([tok_hi, tok_lo, g_hi, g_mid, g_lo, jnp.zeros((SUBLANES - 5, t), F32)], axis=0)
        res = jnp.dot(lhs.astype(BF16), onehot_t, preferred_element_type=F32)
        idx_ref[0, ei:ei + 1, :] = (res[0:1] * float(1 << TOK_SPLIT_BITS) + res[1:2]).astype(jnp.int32)
        gate_ref[0, ei:ei + 1, :] = res[2:3] + res[3:4] + res[4:5]


def _topk(aff_t, cap):
    n, e, t = aff_t.shape
    kern = functools.partial(_topk_kernel, cap=cap)
    return pl.pallas_call(
        kern,
        grid=(n,),
        in_specs=[pl.BlockSpec((1, e, t), lambda b: (b, 0, 0))],
        out_specs=[pl.BlockSpec((1, t, LANES), lambda b: (b, 0, 0)),
                   pl.BlockSpec((1, e, cap), lambda b: (b, 0, 0)),
                   pl.BlockSpec((1, e, cap), lambda b: (b, 0, 0))],
        out_shape=[jax.ShapeDtypeStruct((n, t, LANES), jnp.int32), jax.ShapeDtypeStruct((n, e, cap), jnp.int32),
                   jax.ShapeDtypeStruct((n, e, cap), F32)],
        compiler_params=_params(("arbitrary",)),
        name="topk",
    )(aff_t)


def _ffn_kernel(idx_ref, *refs, rows, caps, mod_rows, n_exp):
    n_streams = len(rows)
    src_refs = refs[:n_streams]
    sh_ref, sc_ref, gate_ref, wg_ref, wu_ref, wo_ref = refs[n_streams:n_streams + 6]
    y_refs = refs[n_streams + 6:2 * n_streams + 6]
    act_s, wo_s, x_s, land_s, sem = refs[2 * n_streams + 6:]
    e = pl.program_id(0)
    fc = pl.program_id(1)
    nfc = act_s.shape[0]
    r_tot = sum(rows)
    bounds = [0]
    for r in rows:
        bounds.append(bounds[-1] + r)

    def row_copy(expert, row, k):
        tok = idx_ref[expert * r_tot + row]
        return pltpu.make_async_copy(src_refs[k].at[pl.ds(tok, 1), :], land_s.at[pl.ds(row, 1), :], sem)

    def wait_all_rows():
        pltpu.make_async_copy(src_refs[0].at[pl.ds(0, r_tot), :], land_s, sem).wait()

    @pl.when((e == 0) & (fc == 0))
    def _():
        for k in range(n_streams):
            def issue(row, carry, k=k):
                row_copy(0, row, k).start()
                return carry
            lax.fori_loop(bounds[k], bounds[k + 1], issue, 0)

    @pl.when(fc == 0)
    def _():
        wait_all_rows()
        for k in range(n_streams):
            for b, m in enumerate(mod_rows[k]):
                r0 = bounds[k] + b * caps[k]
                scale = 1.0 + sc_ref[0, m:m + 1, :]
                x_s[r0:r0 + caps[k], :] = (land_s[r0:r0 + caps[k], :] * scale + sh_ref[0, m:m + 1, :]).astype(BF16)

    nxt = lax.rem(e + 1, n_exp)
    for k in range(n_streams):
        share = rows[k] // nfc
        for i in range(share):
            row_copy(nxt, bounds[k] + fc * share + i, k).start()

    x = x_s[...]
    g = jnp.dot(x, wg_ref[0, 0].astype(BF16), preferred_element_type=F32)
    up = jnp.dot(x, wu_ref[0, 0].astype(BF16), preferred_element_type=F32)
    act_s[fc] = ((g * jax.nn.sigmoid(g)) * up).astype(BF16)
    wo_s[fc] = wo_ref[0, 0].astype(BF16)

    @pl.when(fc == nfc - 1)
    def _():
        w_out = wo_s[...].reshape(nfc * wo_s.shape[1], wo_s.shape[2])
        gate_col = jnp.broadcast_to(gate_ref[0], (LANES, r_tot)).T[:, 0:1]
        half = r_tot // 2
        for r0, r1 in ((0, half), (half, r_tot)):
            act = jnp.concatenate([act_s[c, r0:r1, :] for c in range(nfc)], axis=1)
            y = (jnp.dot(act, w_out, preferred_element_type=F32) * gate_col[r0:r1]).astype(BF16)
            for k in range(n_streams):
                lo, hi = max(r0, bounds[k]), min(r1, bounds[k + 1])
                if lo < hi:
                    y_refs[k][0, lo - bounds[k]:hi - bounds[k], :] = y[lo - r0:hi - r0]

    @pl.when((e == n_exp - 1) & (fc == nfc - 1))
    def _():
        wait_all_rows()


def _ffn(streams, mod_rows, mods_all, idx, gates, w_in, w_out, layer):
    d = streams[0].shape[2]
    e = gates.shape[0]
    ff = w_out.shape[2]
    nfc = ff // FF_TILE
    caps = tuple(EC_CAPACITY_FACTOR * s.shape[1] // N_EXPERTS for s in streams)
    rows = tuple(s.shape[0] * c for s, c in zip(streams, caps))
    r_tot = sum(rows)
    assert all(r % nfc == 0 for r in rows) and r_tot % LANES == 0
    return pl.pallas_call(
        functools.partial(_ffn_kernel, rows=rows, caps=caps, mod_rows=mod_rows, n_exp=e),
        grid_spec=pltpu.PrefetchScalarGridSpec(
            num_scalar_prefetch=1,
            grid=(e, nfc),
            in_specs=[pl.BlockSpec(memory_space=pl.ANY) for _ in streams] + [
                pl.BlockSpec((1, MOD_ROWS, d), lambda ei, fc, ix: (layer, 0, 3)),
                pl.BlockSpec((1, MOD_ROWS, d), lambda ei, fc, ix: (layer, 0, 4)),
                pl.BlockSpec((1, 1, r_tot), lambda ei, fc, ix: (ei, 0, 0)),
                pl.BlockSpec((1, 1, d, FF_TILE), lambda ei, fc, ix: (layer, ei, 0, fc)),
                pl.BlockSpec((1, 1, d, FF_TILE), lambda ei, fc, ix: (layer, ei, 0, nfc + fc)),
                pl.BlockSpec((1, 1, FF_TILE, d), lambda ei, fc, ix: (layer, ei, fc, 0))],
            out_specs=[pl.BlockSpec((1, r, d), lambda ei, fc, ix: (ei, 0, 0)) for r in rows],
            scratch_shapes=[pltpu.VMEM((nfc, r_tot, FF_TILE), BF16), pltpu.VMEM((nfc, FF_TILE, d), BF16),
                            pltpu.VMEM((r_tot, d), BF16), pltpu.VMEM((r_tot, d), F32),
                            pltpu.SemaphoreType.DMA(())],
        ),
        out_shape=[jax.ShapeDtypeStruct((e, r, d), BF16) for r in rows],
        compiler_params=_params(("arbitrary", "arbitrary")),
        name="expert_ffn",
    )(idx, *[s.reshape(s.shape[0] * s.shape[1], d) for s in streams], mods_all, mods_all, gates, w_in, w_in, w_out)


def _combine_kernel(y_ref, pt_ref, x_ref, g_ref, lg_ref, lb_ref, o_ref, oh_s, *, is_ctx, nb, cap):
    n_exp, _, d = y_ref.shape
    ec = n_exp * cap
    cw = min(ec, COMBINE_COLS)
    pos = pt_ref[0].astype(F32).astype(BF16)
    for c0 in range(0, ec, cw):
        src = _iota((LANES, cw), 0)
        dst = _iota((LANES, cw), 1) + c0
        expand = jnp.where((dst >= src * cap) & (dst < (src + 1) * cap), 1.0, 0.0).astype(BF16)
        pe = jnp.dot(pos, expand, preferred_element_type=F32)
        slot = (_iota((1, cw), 1) + c0) % cap
        oh_s[:, c0:c0 + cw] = jnp.where(pe == slot.astype(F32), 1.0, 0.0).astype(BF16)
    f = jnp.dot(oh_s[...], y_ref[...].reshape(ec, d), preferred_element_type=F32)
    gate = g_ref[0, pl.ds(_mod_row(is_ctx, nb), 1), :]
    o_ref[0] = _layer_norm(DEEPNORM_ALPHA * x_ref[0] + gate * f, lg_ref[...], lb_ref[...])


def _combine(y, pos_tm, x, mods_all, layer, ln_g, ln_b, cap, is_ctx, nb):
    n, t, d = x.shape
    e = y.shape[0]
    tm = min(t, ROW_TILE)
    kern = functools.partial(_combine_kernel, is_ctx=is_ctx, nb=nb, cap=cap)
    tok = pl.BlockSpec((1, tm, d), lambda b, i: (b, i, 0))
    return pl.pallas_call(
        kern,
        grid=(n, t // tm),
        in_specs=[
            pl.BlockSpec((e, cap, d), lambda b, i: (0, b, 0), pipeline_mode=pl.Buffered(1)),
            pl.BlockSpec((1, tm, LANES), lambda b, i: (b, i, 0)),
            tok,
            _mod_spec(layer, d, 5),
            _vec_spec(d), _vec_spec(d),
        ],
        out_specs=tok,
        out_shape=jax.ShapeDtypeStruct((n, t, d), F32),
        scratch_shapes=[pltpu.VMEM((tm, e * cap), BF16)],
        compiler_params=_params(("arbitrary", "arbitrary")),
        name="combine",
    )(y, pos_tm, x, mods_all, ln_g.reshape(1, d), ln_b.reshape(1, d))


def _moe_post(x1_lat, x1_ctx, mods_all, layer, w_router_t, w_in, w_out, ln_g, ln_b, nb):
    streams = [(x1_lat, False)] + ([(x1_ctx, True)] if x1_ctx is not None else [])
    caps = [EC_CAPACITY_FACTOR * s.shape[1] // N_EXPERTS for s, _ in streams]
    routed, tokens, mod_rows, idx_parts, gate_parts = [], [], [], [], []
    for (s, is_ctx), cap in zip(streams, caps):
        n, t = s.shape[:2]
        aff_t = _router(s, mods_all, layer, w_router_t, is_ctx, nb)
        pos_tm, idx, gate = _topk(aff_t, cap)
        routed.append(pos_tm)
        tokens.append(s)
        mod_rows.append(tuple(nb if is_ctx else b for b in range(n)))
        flat = idx + (jnp.arange(n, dtype=jnp.int32) * t)[:, None, None]
        idx_parts.append(jnp.swapaxes(flat, 0, 1).reshape(N_EXPERTS, n * cap))
        gate_parts.append(jnp.swapaxes(gate, 0, 1).reshape(N_EXPERTS, n * cap))
    idx_all = jnp.concatenate(idx_parts, axis=1).reshape(-1)
    gates = jnp.concatenate(gate_parts, axis=1)[:, None, :]
    ys = _ffn(tokens, tuple(mod_rows), mods_all, idx_all, gates, w_in, w_out, layer)
    outs = [_combine(y, pos_tm, s, mods_all, layer, ln_g, ln_b, cap, is_ctx, nb)
            for (s, is_ctx), pos_tm, cap, y in zip(streams, routed, caps, ys)]
    return outs[0], (outs[1] if len(outs) > 1 else None)


def kernel(x, c, ctx, c_ctx, ada_w, ada_b, ln_g, ln_b, s5_a_re, s5_a_im, s5_log_dt, s5_b_re, s5_b_im,
           s5_c_re, s5_c_im, s5_d, s5_w_glu, s5_b_glu, cv_w_pw1, cv_b_pw1, cv_w_dw, cv_b_dw, cv_ln_g,
           cv_ln_b, cv_w_pw2, cv_b_pw2, moe_w_router, moe_w_in, moe_w_out):
    nb, seq, d = x.shape
    assert nb + 1 <= MOD_ROWS and nb % 2 == 0 and d % LANES == 0
    rows = seq // GRID_W

    c8 = jnp.concatenate([c.astype(F32), c_ctx.astype(F32)[None], jnp.zeros((MOD_ROWS - nb - 1, d), F32)], axis=0)
    mods_all = _ada_all(c8, ada_w, ada_b)
    w_glu_bf = s5_w_glu.astype(BF16)
    w_pw1_bf = cv_w_pw1.astype(BF16)
    w_pw2_bf = cv_w_pw2.astype(BF16)
    w_router_t = jnp.swapaxes(moe_w_router, 1, 2)

    x_lat, x_ctx = x, ctx
    for i in range(DEPTH):
        is_s5 = (i % N_MIXERS) == 0
        j = i // N_MIXERS
        ctx_out = any((k % N_MIXERS) == 0 for k in range(i + 1, DEPTH))
        x1_ctx = None

        if is_s5:
            ops = _s5_operators(s5_a_re[j], s5_a_im[j], s5_log_dt[j], s5_b_re[j], s5_b_im[j],
                                s5_c_re[j], s5_c_im[j])
            z_lat, z_ctx = _s5_core(x_lat, x_ctx, mods_all, i, s5_d[j], ops, ctx_out)
            x1_lat = _s5_out(z_lat, w_glu_bf, j, s5_b_glu[j], x_lat, mods_all, i, ln_g[i, 0], ln_b[i, 0], False, nb)
            if ctx_out:
                x1_ctx = _s5_out(z_ctx, w_glu_bf, j, s5_b_glu[j], x_ctx, mods_all, i, ln_g[i, 0], ln_b[i, 0],
                                 True, nb)
        else:
            conv_args = (cv_w_dw[j], cv_b_dw[j], cv_ln_g[j], cv_ln_b[j], w_pw2_bf, j, cv_b_pw2[j])
            h_lat = _pw1(x_lat, mods_all, i, w_pw1_bf, j, cv_b_pw1[j], False, nb)
            x1_lat = _conv_post(h_lat, *conv_args, x_lat, mods_all, i, ln_g[i, 0], ln_b[i, 0], rows, False, nb)
            if ctx_out:
                h_ctx = _pw1(x_ctx, mods_all, i, w_pw1_bf, j, cv_b_pw1[j], True, nb)
                x1_ctx = _conv_post(h_ctx, *conv_args, x_ctx, mods_all, i, ln_g[i, 0], ln_b[i, 0], None, True, nb)

        x_lat, x_ctx_new = _moe_post(x1_lat, x1_ctx, mods_all, i, w_router_t, moe_w_in, moe_w_out,
                                     ln_g[i, 1], ln_b[i, 1], nb)
        if ctx_out:
            x_ctx = x_ctx_new
    return x_lat
```

```python
import functools

import jax
import jax.numpy as jnp
from jax import lax
from jax.experimental import pallas as pl
from jax.experimental.pallas import tpu as pltpu

F32 = jnp.float32
BF16 = jnp.bfloat16

DEPTH = 4
N_MIXERS = 2
GRID_W = 64
S5_H = 16
S5_P = 64
N_EXPERTS = 16
EC_CAPACITY_FACTOR = 2
DEEPNORM_ALPHA = (2.0 * DEPTH) ** 0.25
LN_EPS = 1e-5

LANES = 128
SUBLANES = 8
S5_CHUNK = 8
S5_GPT = LANES // S5_H
S5_STATE_COLS = S5_GPT * S5_P
MOD_ROWS = 8
VMEM_LIMIT = 56 * 1024 * 1024
ROW_TILE = 512
COMBINE_COLS = 1024
TOK_SPLIT_BITS = 6
FF_TILE = 256
CONV_WT = 2 * SUBLANES
CTX_CONV_ROWS = 128
CTX_CONV_COLS = 2 * LANES
ADA_TILE = 2048


def _params(sem, vmem=VMEM_LIMIT):
    return pltpu.CompilerParams(dimension_semantics=sem, vmem_limit_bytes=vmem)


def _layer_norm(v, g, b):
    mu = jnp.mean(v, axis=-1, keepdims=True)
    c = v - mu
    var = jnp.mean(c * c, axis=-1, keepdims=True)
    return c * lax.rsqrt(var + LN_EPS) * g + b


def _split_bf16(v):
    hi = v.astype(BF16)
    lo = (v - hi.astype(F32)).astype(BF16)
    return hi, lo


def _mod_spec(layer, d, k):
    return pl.BlockSpec((1, MOD_ROWS, d), lambda *_: (layer, 0, k))


def _vec_spec(d):
    return pl.BlockSpec((1, d), lambda *_: (0, 0))


def _mod_row(is_ctx, nb):
    return nb if is_ctx else pl.program_id(0)


def _ada_kernel(c_ref, w_ref, b_ref, o_ref):
    c = c_ref[...]
    cond = c * jax.nn.sigmoid(c)
    hi, lo = _split_bf16(cond)
    lhs = jnp.concatenate([hi, lo], axis=0)
    r = jnp.dot(lhs, w_ref[0].astype(BF16), preferred_element_type=F32)
    o_ref[0] = r[:MOD_ROWS] + r[MOD_ROWS:] + b_ref[0]


def _ada_all(c8, ada_w, ada_b):
    depth, d, n = ada_w.shape
    tn = ADA_TILE
    return pl.pallas_call(
        _ada_kernel,
        grid=(depth, n // tn),
        in_specs=[
            pl.BlockSpec((MOD_ROWS, d), lambda i, k: (0, 0)),
            pl.BlockSpec((1, d, tn), lambda i, k: (i, 0, k)),
            pl.BlockSpec((1, 1, tn), lambda i, k: (i, 0, k)),
        ],
        out_specs=pl.BlockSpec((1, MOD_ROWS, tn), lambda i, k: (i, 0, k)),
        out_shape=jax.ShapeDtypeStruct((depth, MOD_ROWS, n), F32),
        compiler_params=_params(("arbitrary", "arbitrary")),
        name="adaln",
    )(c8, ada_w, ada_b.reshape(depth, 1, n))


def _cmul(ar, ai, br, bi):
    return ar * br - ai * bi, ar * bi + ai * br


def _s5_operators(a_re, a_im, log_dt, b_re, b_im, c_re, c_im):
    hp = lax.Precision.HIGHEST
    t = S5_CHUNK
    g = a_re.shape[1]
    nj = g // S5_GPT
    lam_r, lam_i = a_re.astype(F32), a_im.astype(F32)
    dt = jnp.exp(log_dt.astype(F32))[..., None]
    mag = jnp.exp(lam_r * dt)
    abar_r, abar_i = mag * jnp.cos(lam_i * dt), mag * jnp.sin(lam_i * dt)
    den = lam_r * lam_r + lam_i * lam_i
    xr, xi = abar_r - 1.0, abar_i
    coef_r, coef_i = (xr * lam_r + xi * lam_i) / den, (xi * lam_r - xr * lam_i) / den
    bb_r, bb_i = _cmul(coef_r[..., None], coef_i[..., None], b_re.astype(F32), b_im.astype(F32))
    cm_r, cm_i = c_re.astype(F32), c_im.astype(F32)
    pr, pi = [jnp.ones_like(abar_r)], [jnp.zeros_like(abar_r)]
    for _ in range(t):
        nr, ni = _cmul(pr[-1], pi[-1], abar_r, abar_i)
        pr.append(nr)
        pi.append(ni)
    pr, pi = jnp.stack(pr), jnp.stack(pi)

    def lag_kernels(d):
        qr, qi = _cmul(pr[:t, d, :, :, None], pi[:t, d, :, :, None], bb_r[d][None], bb_i[d][None])
        return (jnp.einsum('ghp,kgpj->kghj', cm_r[d], qr, precision=hp)
                - jnp.einsum('ghp,kgpj->kghj', cm_i[d], qi, precision=hp))

    kf, kb = lag_kernels(0), lag_kernels(1)
    kall = jnp.concatenate([kb[:0:-1], (kf[0] + kb[0])[None], kf[1:]], axis=0)
    nlag = 2 * t - 1
    kl = kall.reshape(nlag, nj, S5_GPT, S5_H, S5_H).transpose(1, 0, 2, 4, 3).reshape(nj, nlag * LANES, S5_H)

    def per_tile(re, im, perm, rows, cols):
        both = jnp.stack([re, im], axis=1)
        both = both.reshape(2, 2, nj, S5_GPT, both.shape[-2], both.shape[-1])
        return both.transpose(perm).reshape(nj, rows, cols)

    bt = per_tile(bb_r, bb_i, (2, 0, 1, 3, 5, 4), 4 * LANES, S5_P)
    ct = per_tile(cm_r, cm_i, (2, 0, 1, 3, 5, 4), 4 * S5_STATE_COLS, S5_H)

    exps_b = (jnp.arange(t - 1, -1, -1), jnp.arange(t))
    exps_c = (jnp.arange(1, t + 1), jnp.arange(t, 0, -1))

    def powers(exps):
        both = jnp.stack([jnp.stack([pr[exps[d], d], pi[exps[d], d]]) for d in range(2)])
        return both.reshape(2, 2, t, nj, S5_STATE_COLS)

    pw = powers(exps_b).transpose(3, 0, 1, 2, 4).reshape(nj, 4 * t, S5_STATE_COLS)
    pc = powers(exps_c).transpose(3, 0, 1, 4, 2).reshape(nj, 4 * S5_STATE_COLS, t)

    def decay_tiles(d):
        ar = pr[t, d].reshape(nj, S5_STATE_COLS // LANES, LANES)
        ai = pi[t, d].reshape(nj, S5_STATE_COLS // LANES, LANES)
        return jnp.concatenate([ar, ar, -ai, ai], axis=1)

    at = jnp.concatenate([decay_tiles(0), decay_tiles(1)], axis=1)
    return kl, bt, ct, pw, pc, at


def _iota(shape, axis):
    return lax.broadcasted_iota(jnp.int32, shape, axis)


def _group_blockdiag(comp, width, rows_per_group):
    k = comp.shape[1]
    n = S5_GPT * k
    rep = jnp.where((_iota((k, n), 1) & (k - 1)) == _iota((k, n), 0), 1.0, 0.0).astype(BF16)
    hi = comp.astype(BF16)
    rest = comp - hi.astype(F32)
    mid = rest.astype(BF16)
    lo = (rest - mid.astype(F32)).astype(BF16)
    tiled = (jnp.dot(hi, rep, preferred_element_type=F32) + jnp.dot(mid, rep, preferred_element_type=F32)
             + jnp.dot(lo, rep, preferred_element_type=F32))
    row_grp = (_iota((comp.shape[0], 1), 0) >> (rows_per_group.bit_length() - 1)) & (S5_GPT - 1)
    col_grp = _iota((1, n), 1) >> (width.bit_length() - 1)
    return jnp.where(row_grp == col_grp, tiled, 0.0)


def _build_s5_operators(kl_ref, bt_ref, ct_ref, pw_ref, pc_ref, wm_s, wb_s, wc_s):
    t = S5_CHUNK
    sc = S5_STATE_COLS
    lagk = _group_blockdiag(kl_ref[0], S5_H, S5_H).astype(BF16)
    for s in range(t):
        for u in range(t):
            lag = u - s + t - 1
            wm_s[s * LANES:(s + 1) * LANES, u * LANES:(u + 1) * LANES] = lagk[lag * LANES:(lag + 1) * LANES, :]
    bexp = _group_blockdiag(bt_ref[0], S5_P, S5_H)
    for d in range(2):
        b_re = bexp[(2 * d) * LANES:(2 * d + 1) * LANES]
        b_im = bexp[(2 * d + 1) * LANES:(2 * d + 2) * LANES]
        for s in range(t):
            p_re = pw_ref[0, (2 * d) * t + s:(2 * d) * t + s + 1, :]
            p_im = pw_ref[0, (2 * d + 1) * t + s:(2 * d + 1) * t + s + 1, :]
            rows = slice(s * LANES, (s + 1) * LANES)
            wb_s[rows, d * 2 * sc:d * 2 * sc + sc] = (b_re * p_re - b_im * p_im).astype(BF16)
            wb_s[rows, d * 2 * sc + sc:(d + 1) * 2 * sc] = (b_re * p_im + b_im * p_re).astype(BF16)
    cexp = _group_blockdiag(ct_ref[0], S5_H, S5_P)
    for d in range(2):
        c_re = cexp[(2 * d) * sc:(2 * d + 1) * sc]
        c_im = cexp[(2 * d + 1) * sc:(2 * d + 2) * sc]
        for u in range(t):
            p_re = pc_ref[0, (2 * d) * sc:(2 * d + 1) * sc, u:u + 1]
            p_im = pc_ref[0, (2 * d + 1) * sc:(2 * d + 2) * sc, u:u + 1]
            cols = slice(u * LANES, (u + 1) * LANES)
            wc_s[d * 2 * sc:d * 2 * sc + sc, cols] = (c_re * p_re - c_im * p_im).astype(BF16)
            wc_s[d * 2 * sc + sc:(d + 1) * 2 * sc, cols] = (-(c_re * p_im + c_im * p_re)).astype(BF16)


def _s5_kernel(x_ref, xc_ref, sh_ref, sc_ref, d_ref, kl_ref, bt_ref, ct_ref, pw_ref, pc_ref, at_ref, *rest,
               nb_half, n_lat, n_ctx, ctx_out):
    if ctx_out:
        z_ref, zc_ref, wm_s, wb_s, wc_s, xf_s, st_s, en_s, y_s, zb_s, zcb_s = rest
    else:
        z_ref, wm_s, wb_s, wc_s, xf_s, st_s, en_s, y_s, zb_s = rest
        zc_ref = zcb_s = None
    t = S5_CHUNK
    ncl = n_lat // t
    ncc = n_ctx // t
    lat_rows = nb_half * ncl
    half = pl.program_id(1)
    sc_cols = S5_STATE_COLS
    nst = sc_cols // LANES
    n_rows = nb_half * (ncl + ncc)

    @pl.when(half == 0)
    def _():
        _build_s5_operators(kl_ref, bt_ref, ct_ref, pw_ref, pc_ref, wm_s, wb_s, wc_s)

    for k in range(nb_half):
        b = half * nb_half + k
        scale = 1.0 + sc_ref[0, pl.ds(b, 1), :]
        shift = sh_ref[0, pl.ds(b, 1), :]
        for s in range(t):
            xf_s[k * ncl:(k + 1) * ncl, s * LANES:(s + 1) * LANES] = (
                x_ref[pl.ds(k * n_lat + s, ncl, stride=t), :] * scale + shift)
    scale_c = 1.0 + sc_ref[0, nb_half * 2:nb_half * 2 + 1, :]
    shift_c = sh_ref[0, nb_half * 2:nb_half * 2 + 1, :]
    for k in range(nb_half):
        for s in range(t):
            r0 = lat_rows + k * ncc
            xf_s[r0:r0 + ncc, s * LANES:(s + 1) * LANES] = (
                xc_ref[pl.ds(k * n_ctx + s, ncc, stride=t), :] * scale_c + shift_c)

    xb = xf_s[...].astype(BF16)
    dvec = jnp.concatenate([d_ref[...]] * t, axis=1)
    y_s[...] = jnp.dot(xb, wm_s[...], preferred_element_type=F32) + xf_s[...] * dvec

    for d in range(2):
        local = jnp.dot(xb, wb_s[:, d * 2 * sc_cols:(d + 1) * 2 * sc_cols], preferred_element_type=F32)
        for q in range(2 * nst):
            st_s[d, pl.ds(q, n_rows, stride=2 * nst), :] = local[:, q * LANES:(q + 1) * LANES]
    coef = [(at_ref[0, (2 * d) * 2 * nst:(2 * d + 1) * 2 * nst, :],
             at_ref[0, (2 * d + 1) * 2 * nst:(2 * d + 2) * 2 * nst, :]) for d in range(2)]

    def step(chunk_rows, h):
        new = []
        for d in range(2):
            for k in range(nb_half):
                tile = pl.ds(pl.multiple_of((chunk_rows[d] + k * chunk_rows[2]) * 2 * nst, 2 * nst), 2 * nst)
                prev, prev_sw = h[2 * (d * nb_half + k)], h[2 * (d * nb_half + k) + 1]
                local_state = st_s[d, tile, :]
                en_s[d, tile, :] = prev
                new.append(coef[d][0] * prev + coef[d][1] * prev_sw + local_state)
                new.append(coef[d][0] * prev_sw - coef[d][1] * prev + pltpu.roll(local_state, nst, axis=0))
        return tuple(new)

    zero = tuple(jnp.zeros((2 * nst, LANES), F32) for _ in range(4 * nb_half))
    h = lax.fori_loop(0, ncc, lambda i, h: step((lat_rows + i, lat_rows + ncc - 1 - i, ncc), h), zero, unroll=4)
    lax.fori_loop(0, ncl, lambda i, h: step((i, ncl - 1 - i, ncl), h), h, unroll=4)

    entering = jnp.concatenate([en_s[d, pl.ds(q, n_rows, stride=2 * nst), :]
                                for d in range(2) for q in range(2 * nst)], axis=1)
    y_s[...] += jnp.dot(entering.astype(BF16), wc_s[...], preferred_element_type=F32)

    z = jax.nn.gelu(y_s[...])
    for k in range(nb_half):
        for s in range(t):
            zb_s[pl.ds(k * n_lat + s, ncl, stride=t), :] = z[k * ncl:(k + 1) * ncl, s * LANES:(s + 1) * LANES]
    z_ref[...] = zb_s[...].astype(BF16)
    if ctx_out:
        for k in range(nb_half):
            for s in range(t):
                r0 = lat_rows + k * ncc
                zcb_s[pl.ds(k * n_ctx + s, ncc, stride=t), :] = z[r0:r0 + ncc, s * LANES:(s + 1) * LANES]
        zc_ref[...] = zcb_s[...].astype(BF16)


def _s5_core(x, xc, mods_all, layer, d_skip, ops, ctx_out):
    nb, n_lat, d = x.shape
    n_ctx = xc.shape[1]
    nj = d // LANES
    nb_half = nb // 2
    t = S5_CHUNK
    tl = t * LANES
    rows = nb_half * (n_lat + n_ctx) // t
    kern = functools.partial(_s5_kernel, nb_half=nb_half, n_lat=n_lat, n_ctx=n_ctx, ctx_out=ctx_out)
    out_shape = [jax.ShapeDtypeStruct((nb * n_lat, d), BF16)]
    out_specs = [pl.BlockSpec((nb_half * n_lat, LANES), lambda j, h: (h, j))]
    scratch = [pltpu.VMEM((tl, tl), BF16), pltpu.VMEM((tl, 4 * S5_STATE_COLS), BF16),
               pltpu.VMEM((4 * S5_STATE_COLS, tl), BF16),
               pltpu.VMEM((rows, tl), F32), pltpu.VMEM((2, rows * 2 * S5_STATE_COLS // LANES, LANES), F32),
               pltpu.VMEM((2, rows * 2 * S5_STATE_COLS // LANES, LANES), F32),
               pltpu.VMEM((rows, tl), F32), pltpu.VMEM((nb_half * n_lat, LANES), F32)]
    if ctx_out:
        out_shape.append(jax.ShapeDtypeStruct((nb * n_ctx, d), BF16))
        out_specs.append(pl.BlockSpec((nb_half * n_ctx, LANES), lambda j, h: (h, j)))
        scratch.append(pltpu.VMEM((nb_half * n_ctx, LANES), F32))
    res = pl.pallas_call(
        kern,
        grid=(nj, 2),
        in_specs=[
            pl.BlockSpec((nb_half * n_lat, LANES), lambda j, h: (h, j)),
            pl.BlockSpec((nb_half * n_ctx, LANES), lambda j, h: (h, j)),
            pl.BlockSpec((1, MOD_ROWS, LANES), lambda j, h: (layer, 0, j)),
            pl.BlockSpec((1, MOD_ROWS, LANES), lambda j, h: (layer, 0, nj + j)),
            pl.BlockSpec((1, LANES), lambda j, h: (0, j)),
        ] + [pl.BlockSpec((1,) + op.shape[1:], lambda j, h: (j, 0, 0)) for op in ops],
        out_specs=out_specs,
        out_shape=out_shape,
        scratch_shapes=scratch,
        compiler_params=_params(("arbitrary", "arbitrary")),
        name="s5_core",
    )(x.reshape(nb * n_lat, d), xc.reshape(nb * n_ctx, d), mods_all, mods_all, d_skip.reshape(1, d), *ops)
    z = res[0].reshape(nb, n_lat, d)
    zc = res[1].reshape(nb, n_ctx, d) if ctx_out else None
    return z, zc


def _s5_out_kernel(z_ref, w_ref, b_ref, x_ref, g_ref, lg_ref, lb_ref, o_ref, *, is_ctx, nb):
    d = x_ref.shape[-1]
    acc = jnp.dot(z_ref[0], w_ref[0], preferred_element_type=F32) + b_ref[...]
    y = acc[:, :d] * jax.nn.sigmoid(acc[:, d:])
    gate = g_ref[0, pl.ds(_mod_row(is_ctx, nb), 1), :]
    o_ref[0] = _layer_norm(DEEPNORM_ALPHA * x_ref[0] + gate * y, lg_ref[...], lb_ref[...])


def _s5_out(z, w_bf, j, b_glu, x, mods_all, layer, ln_g, ln_b, is_ctx, nb):
    n, t, d = x.shape
    tm = min(t, ROW_TILE)
    kern = functools.partial(_s5_out_kernel, is_ctx=is_ctx, nb=nb)
    tok = pl.BlockSpec((1, tm, d), lambda b, i: (b, i, 0))
    return pl.pallas_call(
        kern,
        grid=(n, t // tm),
        in_specs=[
            tok,
            pl.BlockSpec((1, d, 2 * d), lambda b, i: (j, 0, 0), pipeline_mode=pl.Buffered(1)),
            _vec_spec(2 * d),
            tok,
            _mod_spec(layer, d, 2),
            _vec_spec(d), _vec_spec(d),
        ],
        out_specs=tok,
        out_shape=jax.ShapeDtypeStruct((n, t, d), F32),
        compiler_params=_params(("arbitrary", "arbitrary")),
        name="s5_out",
    )(z, w_bf, b_glu.reshape(1, 2 * d), x, mods_all, ln_g.reshape(1, d), ln_b.reshape(1, d))


def _pw1_kernel(x_ref, sh_ref, sc_ref, w_ref, b_ref, o_ref, *, is_ctx, nb):
    d = x_ref.shape[-1]
    row = _mod_row(is_ctx, nb)
    u = x_ref[0] * (1.0 + sc_ref[0, pl.ds(row, 1), :]) + sh_ref[0, pl.ds(row, 1), :]
    acc = jnp.dot(u.astype(BF16), w_ref[0], preferred_element_type=F32) + b_ref[...]
    o_ref[0] = acc[:, :d] * jax.nn.sigmoid(acc[:, d:])


def _pw1(x, mods_all, layer, w_bf, j, b_pw1, is_ctx, nb):
    n, t, d = x.shape
    tm = min(t, ROW_TILE)
    kern = functools.partial(_pw1_kernel, is_ctx=is_ctx, nb=nb)
    tok = pl.BlockSpec((1, tm, d), lambda b, i: (b, i, 0))
    return pl.pallas_call(
        kern,
        grid=(n, t // tm),
        in_specs=[
            tok,
            _mod_spec(layer, d, 0), _mod_spec(layer, d, 1),
            pl.BlockSpec((1, d, 2 * d), lambda b, i: (j, 0, 0), pipeline_mode=pl.Buffered(1)),
            _vec_spec(2 * d),
        ],
        out_specs=tok,
        out_shape=jax.ShapeDtypeStruct((n, t, d), F32),
        compiler_params=_params(("arbitrary", "arbitrary")),
        name="conv_pw1",
    )(x, mods_all, mods_all, w_bf, b_pw1.reshape(1, 2 * d))


def _conv_tail(cv, cg_ref, cb_ref, w2_ref, b2_ref, x, gate, lg_ref, lb_ref):
    hn = _layer_norm(cv, cg_ref[...], cb_ref[...])
    hn = hn * jax.nn.sigmoid(hn)
    y = jnp.dot(hn.astype(BF16), w2_ref[0], preferred_element_type=F32) + b2_ref[...]
    return _layer_norm(DEEPNORM_ALPHA * x + gate * y, lg_ref[...], lb_ref[...])


def _conv_lat_kernel(h_ref, wdw_ref, bdw_ref, cg_ref, cb_ref, w2_ref, b2_ref, x_ref, g_ref, lg_ref, lb_ref,
                     o_ref, wb_s, cv_s, *, n_rows, wt):
    kw = wdw_ref.shape[0]
    pad = kw // 2
    d = h_ref.shape[-1]
    nsub = wt // SUBLANES
    for k in range(kw):
        wb_s[k] = jnp.broadcast_to(wdw_ref[k:k + 1, :], (SUBLANES, d))
    bias = bdw_ref[...]
    for r in range(n_rows):
        accs = [None] * nsub
        for k in range(max(0, pad - r), min(kw, n_rows + pad - r)):
            w8 = wb_s[k]
            for q in range(nsub):
                term = w8 * h_ref[0, r + k - pad, q * SUBLANES:(q + 1) * SUBLANES, :]
                accs[q] = term if accs[q] is None else accs[q] + term
        for q in range(nsub):
            cv_s[r * wt + q * SUBLANES:r * wt + (q + 1) * SUBLANES, :] = accs[q] + bias
    gate = g_ref[0, pl.ds(pl.program_id(0), 1), :]
    x = x_ref[0].reshape(n_rows * wt, d)
    out = _conv_tail(cv_s[...], cg_ref, cb_ref, w2_ref, b2_ref, x, gate, lg_ref, lb_ref)
    o_ref[0] = out.reshape(n_rows, wt, d)


def _conv_ctx_kernel(h_ref, wdw_ref, bdw_ref, cg_ref, cb_ref, w2_ref, b2_ref, x_ref, g_ref, lg_ref, lb_ref,
                     o_ref, hp_s, cv_s, *, n_tok, nb):
    kw = wdw_ref.shape[0]
    pad = kw // 2
    d = h_ref.shape[-1]
    lead = 2 * SUBLANES
    rblk = CTX_CONV_ROWS
    cblk = CTX_CONV_COLS
    hp_s[0:lead] = jnp.zeros((lead, d), F32)
    hp_s[lead + n_tok:lead + n_tok + lead] = jnp.zeros((lead, d), F32)
    hp_s[lead:lead + n_tok] = h_ref[0]
    span = ((kw - 1 + lead - pad) // SUBLANES) * SUBLANES

    def col_block(lc, carry):
        cols = pl.ds(pl.multiple_of(lc * cblk, cblk), cblk)
        for rc in range(n_tok // rblk):
            acc = jnp.zeros((rblk, cblk), F32)
            for q in range(SUBLANES):
                taps = [k for k in range(kw) if (k + lead - pad) % SUBLANES == q]
                if not taps:
                    continue
                shifted = hp_s[pl.ds(rc * rblk + q, rblk + span), cols]
                for k in taps:
                    o = k + lead - pad - q
                    acc = acc + wdw_ref[k:k + 1, cols] * shifted[o:o + rblk]
            cv_s[rc * rblk:(rc + 1) * rblk, cols] = acc + bdw_ref[:, cols]
        return carry

    lax.fori_loop(0, d // cblk, col_block, 0)
    gate = g_ref[0, nb:nb + 1, :]
    o_ref[0] = _conv_tail(cv_s[...], cg_ref, cb_ref, w2_ref, b2_ref, x_ref[0], gate, lg_ref, lb_ref)


def _conv_post(h, w_dw, b_dw, cv_g, cv_b, w2_bf, j, b2, x, mods_all, layer, ln_g, ln_b, n_rows, is_ctx, nb):
    n, t, d = x.shape
    kw = w_dw.shape[0]
    weights = [pl.BlockSpec((kw, d), lambda b, i: (0, 0)), _vec_spec(d), _vec_spec(d), _vec_spec(d),
               pl.BlockSpec((1, d, d), lambda b, i: (j, 0, 0), pipeline_mode=pl.Buffered(1)), _vec_spec(d)]
    tail = [_mod_spec(layer, d, 2), _vec_spec(d), _vec_spec(d)]
    args_w = (w_dw, b_dw.reshape(1, d), cv_g.reshape(1, d), cv_b.reshape(1, d), w2_bf, b2.reshape(1, d))
    args_t = (mods_all, ln_g.reshape(1, d), ln_b.reshape(1, d))
    if is_ctx:
        blk = pl.BlockSpec((1, t, d), lambda b, i: (b, 0, 0))
        return pl.pallas_call(
            functools.partial(_conv_ctx_kernel, n_tok=t, nb=nb),
            grid=(n, 1),
            in_specs=[blk] + weights + [blk] + tail,
            out_specs=blk,
            out_shape=jax.ShapeDtypeStruct((n, t, d), F32),
            scratch_shapes=[pltpu.VMEM((t + 4 * SUBLANES, d), F32), pltpu.VMEM((t, d), F32)],
            compiler_params=_params(("arbitrary", "arbitrary")),
            name="conv_post_ctx",
        )(h, *args_w, x, *args_t)
    wt = CONV_WT
    width = t // n_rows
    blk = pl.BlockSpec((1, n_rows, wt, d), lambda b, i: (b, 0, i, 0))
    out = pl.pallas_call(
        functools.partial(_conv_lat_kernel, n_rows=n_rows, wt=wt),
        grid=(n, width // wt),
        in_specs=[blk] + weights + [blk] + tail,
        out_specs=blk,
        out_shape=jax.ShapeDtypeStruct((n, n_rows, width, d), F32),
        scratch_shapes=[pltpu.VMEM((kw, SUBLANES, d), F32),
                        pltpu.VMEM((n_rows * wt, d), F32)],
        compiler_params=_params(("arbitrary", "arbitrary")),
        name="conv_post",
    )(h.reshape(n, n_rows, width, d), *args_w, x.reshape(n, n_rows, width, d), *args_t)
    return out.reshape(n, t, d)


def _router_kernel(x_ref, sh_ref, sc_ref, wr_ref, a_ref, *, is_ctx, nb):
    row = _mod_row(is_ctx, nb)
    u = x_ref[0] * (1.0 + sc_ref[0, pl.ds(row, 1), :]) + sh_ref[0, pl.ds(row, 1), :]
    uh, ul = _split_bf16(u)
    wh, wl = _split_bf16(wr_ref[0])
    nt = (((1,), (1,)), ((), ()))
    logits = (lax.dot_general(wh, uh, nt, preferred_element_type=F32)
              + lax.dot_general(wh, ul, nt, preferred_element_type=F32)
              + lax.dot_general(wl, uh, nt, preferred_element_type=F32))
    m = jnp.max(logits, axis=0, keepdims=True)
    ex = jnp.exp(logits - m)
    a_ref[0] = ex / jnp.sum(ex, axis=0, keepdims=True)


def _router(x, mods_all, layer, w_router_t, is_ctx, nb):
    n, t, d = x.shape
    e = w_router_t.shape[1]
    tm = min(t, ROW_TILE)
    kern = functools.partial(_router_kernel, is_ctx=is_ctx, nb=nb)
    tok = pl.BlockSpec((1, tm, d), lambda b, i: (b, i, 0))
    return pl.pallas_call(
        kern,
        grid=(n, t // tm),
        in_specs=[tok, _mod_spec(layer, d, 3), _mod_spec(layer, d, 4),
                  pl.BlockSpec((1, e, d), lambda b, i: (layer, 0, 0))],
        out_specs=pl.BlockSpec((1, e, tm), lambda b, i: (b, 0, i)),
        out_shape=jax.ShapeDtypeStruct((n, e, t), F32),
        compiler_params=_params(("arbitrary", "arbitrary")),
        name="router",
    )(x, mods_all, mods_all, w_router_t)


def _topk_kernel(a_ref, pt_ref, idx_ref, gate_ref, *, cap):
    a = a_ref[0]
    e, t = a.shape
    capf = jnp.float32(cap)

    def count(mask):
        return jnp.sum(jnp.where(mask, 1.0, 0.0), axis=1, keepdims=True)

    def as_row_values(bits):
        return jnp.concatenate([pltpu.bitcast(bits, F32)] * (t // LANES), axis=1)

    thr_bits = jnp.zeros((e, LANES), jnp.int32)
    for bit in range(30, -1, -1):
        cand = thr_bits | jnp.int32(1 << bit)
        keep = count(a >= as_row_values(cand)) >= capf
        thr_bits = jnp.where(keep, cand, thr_bits)
    thr = as_row_values(thr_bits)
    gt = a > thr
    eq = a == thr
    need = capf - count(gt)
    tri = jnp.where(_iota((t, t), 0) <= _iota((t, t), 1), 1.0, 0.0).astype(BF16)
    eq_f = jnp.where(eq, 1.0, 0.0)
    eq_rank = jnp.dot(eq_f.astype(BF16), tri, preferred_element_type=F32) - eq_f
    sel = jnp.where(gt, 1.0, jnp.where(eq & (eq_rank < need), 1.0, 0.0))
    slot = jnp.dot(sel.astype(BF16), tri, preferred_element_type=F32) - 1.0
    pos = jnp.where(sel > 0.0, slot, -1.0)
    padded = jnp.concatenate([pos, jnp.full((LANES - e, t), -1.0, F32)], axis=0)
    pos_tm = padded.T
    pt_ref[0] = pos_tm.astype(jnp.int32)
    tok = _iota((1, t), 1)
    tok_hi = (tok >> TOK_SPLIT_BITS).astype(F32)
    tok_lo = (tok & ((1 << TOK_SPLIT_BITS) - 1)).astype(F32)
    slots = _iota((cap, 1), 0).astype(F32)
    contract_tokens = (((1,), (1,)), ((), ()))
    for ei in range(e):
        onehot = jnp.where(pos[ei:ei + 1, :] == slots, 1.0, 0.0).astype(BF16)
        g = a[ei:ei + 1, :]
        g_hi = g.astype(BF16).astype(F32)
        g_mid = (g - g_hi).astype(BF16).astype(F32)
        g_lo = (g - g_hi) - g_mid
        lhs = jnp.concatenate([tok_hi, tok_lo, g_hi, g_mid, g_lo, jnp.zeros((SUBLANES - 5, t), F32)], axis=0)
        res = lax.dot_general(lhs.astype(BF16), onehot, contract_tokens, preferred_element_type=F32)
        idx_ref[0, ei:ei + 1, :] = (res[0:1] * float(1 << TOK_SPLIT_BITS) + res[1:2]).astype(jnp.int32)
        gate_ref[0, ei:ei + 1, :] = res[2:3] + res[3:4] + res[4:5]


def _topk(aff_t, cap):
    n, e, t = aff_t.shape
    kern = functools.partial(_topk_kernel, cap=cap)
    return pl.pallas_call(
        kern,
        grid=(n,),
        in_specs=[pl.BlockSpec((1, e, t), lambda b: (b, 0, 0))],
        out_specs=[pl.BlockSpec((1, t, LANES), lambda b: (b, 0, 0)),
                   pl.BlockSpec((1, e, cap), lambda b: (b, 0, 0)),
                   pl.BlockSpec((1, e, cap), lambda b: (b, 0, 0))],
        out_shape=[jax.ShapeDtypeStruct((n, t, LANES), jnp.int32), jax.ShapeDtypeStruct((n, e, cap), jnp.int32),
                   jax.ShapeDtypeStruct((n, e, cap), F32)],
        compiler_params=_params(("arbitrary",)),
        name="topk",
    )(aff_t)


def _ffn_kernel(idx_ref, *refs, rows, caps, mod_rows, n_exp):
    n_streams = len(rows)
    src_refs = refs[:n_streams]
    sh_ref, sc_ref, gate_ref, wg_ref, wu_ref, wo_ref = refs[n_streams:n_streams + 6]
    y_refs = refs[n_streams + 6:2 * n_streams + 6]
    act_s, wo_s, x_s, land_s, sem = refs[2 * n_streams + 6:]
    e = pl.program_id(0)
    fc = pl.program_id(1)
    nfc = act_s.shape[0]
    r_tot = sum(rows)
    bounds = [0]
    for r in rows:
        bounds.append(bounds[-1] + r)

    def row_copy(expert, row, k):
        tok = idx_ref[expert * r_tot + row]
        return pltpu.make_async_copy(src_refs[k].at[pl.ds(tok, 1), :], land_s.at[pl.ds(row, 1), :], sem)

    def wait_all_rows():
        pltpu.make_async_copy(src_refs[0].at[pl.ds(0, r_tot), :], land_s, sem).wait()

    @pl.when((e == 0) & (fc == 0))
    def _():
        for k in range(n_streams):
            def issue(row, carry, k=k):
                row_copy(0, row, k).start()
                return carry
            lax.fori_loop(bounds[k], bounds[k + 1], issue, 0)

    @pl.when(fc == 0)
    def _():
        wait_all_rows()
        for k in range(n_streams):
            for b, m in enumerate(mod_rows[k]):
                r0 = bounds[k] + b * caps[k]
                scale = 1.0 + sc_ref[0, m:m + 1, :]
                x_s[r0:r0 + caps[k], :] = (land_s[r0:r0 + caps[k], :] * scale + sh_ref[0, m:m + 1, :]).astype(BF16)

    nxt = lax.rem(e + 1, n_exp)
    for k in range(n_streams):
        share = rows[k] // nfc
        for i in range(share):
            row_copy(nxt, bounds[k] + fc * share + i, k).start()

    x = x_s[...]
    g = jnp.dot(x, wg_ref[0, 0].astype(BF16), preferred_element_type=F32)
    up = jnp.dot(x, wu_ref[0, 0].astype(BF16), preferred_element_type=F32)
    act_s[fc] = ((g * jax.nn.sigmoid(g)) * up).astype(BF16)
    wo_s[fc] = wo_ref[0, 0].astype(BF16)

    @pl.when(fc == nfc - 1)
    def _():
        w_out = wo_s[...].reshape(nfc * wo_s.shape[1], wo_s.shape[2])
        gate_col = jnp.broadcast_to(gate_ref[0], (LANES, r_tot)).T[:, 0:1]
        half = r_tot // 2
        for r0, r1 in ((0, half), (half, r_tot)):
            act = jnp.concatenate([act_s[c, r0:r1, :] for c in range(nfc)], axis=1)
            y = (jnp.dot(act, w_out, preferred_element_type=F32) * gate_col[r0:r1]).astype(BF16)
            for k in range(n_streams):
                lo, hi = max(r0, bounds[k]), min(r1, bounds[k + 1])
                if lo < hi:
                    y_refs[k][0, lo - bounds[k]:hi - bounds[k], :] = y[lo - r0:hi - r0]

    @pl.when((e == n_exp - 1) & (fc == nfc - 1))
    def _():
        wait_all_rows()


def _ffn(streams, mod_rows, mods_all, idx, gates, w_in, w_out, layer):
    d = streams[0].shape[2]
    e = gates.shape[0]
    ff = w_out.shape[2]
    nfc = ff // FF_TILE
    caps = tuple(EC_CAPACITY_FACTOR * s.shape[1] // N_EXPERTS for s in streams)
    rows = tuple(s.shape[0] * c for s, c in zip(streams, caps))
    r_tot = sum(rows)
    assert all(r % nfc == 0 for r in rows) and r_tot % LANES == 0
    return pl.pallas_call(
        functools.partial(_ffn_kernel, rows=rows, caps=caps, mod_rows=mod_rows, n_exp=e),
        grid_spec=pltpu.PrefetchScalarGridSpec(
            num_scalar_prefetch=1,
            grid=(e, nfc),
            in_specs=[pl.BlockSpec(memory_space=pl.ANY) for _ in streams] + [
                pl.BlockSpec((1, MOD_ROWS, d), lambda ei, fc, ix: (layer, 0, 3)),
                pl.BlockSpec((1, MOD_ROWS, d), lambda ei, fc, ix: (layer, 0, 4)),
                pl.BlockSpec((1, 1, r_tot), lambda ei, fc, ix: (ei, 0, 0)),
                pl.BlockSpec((1, 1, d, FF_TILE), lambda ei, fc, ix: (layer, ei, 0, fc)),
                pl.BlockSpec((1, 1, d, FF_TILE), lambda ei, fc, ix: (layer, ei, 0, nfc + fc)),
                pl.BlockSpec((1, 1, FF_TILE, d), lambda ei, fc, ix: (layer, ei, fc, 0))],
            out_specs=[pl.BlockSpec((1, r, d), lambda ei, fc, ix: (ei, 0, 0)) for r in rows],
            scratch_shapes=[pltpu.VMEM((nfc, r_tot, FF_TILE), BF16), pltpu.VMEM((nfc, FF_TILE, d), BF16),
                            pltpu.VMEM((r_tot, d), BF16), pltpu.VMEM((r_tot, d), F32),
                            pltpu.SemaphoreType.DMA(())],
        ),
        out_shape=[jax.ShapeDtypeStruct((e, r, d), BF16) for r in rows],
        compiler_params=_params(("arbitrary", "arbitrary")),
        name="expert_ffn",
    )(idx, *[s.reshape(s.shape[0] * s.shape[1], d) for s in streams], mods_all, mods_all, gates, w_in, w_in, w_out)


def _combine_kernel(y_ref, pt_ref, x_ref, g_ref, lg_ref, lb_ref, o_ref, oh_s, *, is_ctx, nb, cap):
    n_exp, _, d = y_ref.shape
    ec = n_exp * cap
    cw = min(ec, COMBINE_COLS)
    pos = pt_ref[0].astype(F32).astype(BF16)
    for c0 in range(0, ec, cw):
        src = _iota((LANES, cw), 0)
        dst = _iota((LANES, cw), 1) + c0
        expand = jnp.where((dst >= src * cap) & (dst < (src + 1) * cap), 1.0, 0.0).astype(BF16)
        pe = jnp.dot(pos, expand, preferred_element_type=F32)
        slot = (_iota((1, cw), 1) + c0) % cap
        oh_s[:, c0:c0 + cw] = jnp.where(pe == slot.astype(F32), 1.0, 0.0).astype(BF16)
    f = jnp.dot(oh_s[...], y_ref[...].reshape(ec, d), preferred_element_type=F32)
    gate = g_ref[0, pl.ds(_mod_row(is_ctx, nb), 1), :]
    o_ref[0] = _layer_norm(DEEPNORM_ALPHA * x_ref[0] + gate * f, lg_ref[...], lb_ref[...])


def _combine(y, pos_tm, x, mods_all, layer, ln_g, ln_b, cap, is_ctx, nb):
    n, t, d = x.shape
    e = y.shape[0]
    tm = min(t, ROW_TILE)
    kern = functools.partial(_combine_kernel, is_ctx=is_ctx, nb=nb, cap=cap)
    tok = pl.BlockSpec((1, tm, d), lambda b, i: (b, i, 0))
    return pl.pallas_call(
        kern,
        grid=(n, t // tm),
        in_specs=[
            pl.BlockSpec((e, cap, d), lambda b, i: (0, b, 0), pipeline_mode=pl.Buffered(1)),
            pl.BlockSpec((1, tm, LANES), lambda b, i: (b, i, 0)),
            tok,
            _mod_spec(layer, d, 5),
            _vec_spec(d), _vec_spec(d),
        ],
        out_specs=tok,
        out_shape=jax.ShapeDtypeStruct((n, t, d), F32),
        scratch_shapes=[pltpu.VMEM((tm, e * cap), BF16)],
        compiler_params=_params(("arbitrary", "arbitrary")),
        name="combine",
    )(y, pos_tm, x, mods_all, ln_g.reshape(1, d), ln_b.reshape(1, d))


def _moe_post(x1_lat, x1_ctx, mods_all, layer, w_router_t, w_in, w_out, ln_g, ln_b, nb):
    streams = [(x1_lat, False)] + ([(x1_ctx, True)] if x1_ctx is not None else [])
    caps = [EC_CAPACITY_FACTOR * s.shape[1] // N_EXPERTS for s, _ in streams]
    routed, tokens, mod_rows, idx_parts, gate_parts = [], [], [], [], []
    for (s, is_ctx), cap in zip(streams, caps):
        n, t = s.shape[:2]
        aff_t = _router(s, mods_all, layer, w_router_t, is_ctx, nb)
        pos_tm, idx, gate = _topk(aff_t, cap)
        routed.append(pos_tm)
        tokens.append(s)
        mod_rows.append(tuple(nb if is_ctx else b for b in range(n)))
        flat = idx + (jnp.arange(n, dtype=jnp.int32) * t)[:, None, None]
        idx_parts.append(jnp.swapaxes(flat, 0, 1).reshape(N_EXPERTS, n * cap))
        gate_parts.append(jnp.swapaxes(gate, 0, 1).reshape(N_EXPERTS, n * cap))
    idx_all = jnp.concatenate(idx_parts, axis=1).reshape(-1)
    gates = jnp.concatenate(gate_parts, axis=1)[:, None, :]
    ys = _ffn(tokens, tuple(mod_rows), mods_all, idx_all, gates, w_in, w_out, layer)
    outs = [_combine(y, pos_tm, s, mods_all, layer, ln_g, ln_b, cap, is_ctx, nb)
            for (s, is_ctx), pos_tm, cap, y in zip(streams, routed, caps, ys)]
    return outs[0], (outs[1] if len(outs) > 1 else None)


def kernel(x, c, ctx, c_ctx, ada_w, ada_b, ln_g, ln_b, s5_a_re, s5_a_im, s5_log_dt, s5_b_re, s5_b_im,
           s5_c_re, s5_c_im, s5_d, s5_w_glu, s5_b_glu, cv_w_pw1, cv_b_pw1, cv_w_dw, cv_b_dw, cv_ln_g,
           cv_ln_b, cv_w_pw2, cv_b_pw2, moe_w_router, moe_w_in, moe_w_out):
    nb, seq, d = x.shape
    assert nb + 1 <= MOD_ROWS and nb % 2 == 0 and d % LANES == 0
    rows = seq // GRID_W

    c8 = jnp.concatenate([c.astype(F32), c_ctx.astype(F32)[None], jnp.zeros((MOD_ROWS - nb - 1, d), F32)], axis=0)
    mods_all = _ada_all(c8, ada_w, ada_b)
    w_glu_bf = s5_w_glu.astype(BF16)
    w_pw1_bf = cv_w_pw1.astype(BF16)
    w_pw2_bf = cv_w_pw2.astype(BF16)
    w_router_t = jnp.swapaxes(moe_w_router, 1, 2)

    x_lat, x_ctx = x, ctx
    for i in range(DEPTH):
        is_s5 = (i % N_MIXERS) == 0
        j = i // N_MIXERS
        ctx_out = any((k % N_MIXERS) == 0 for k in range(i + 1, DEPTH))
        x1_ctx = None

        if is_s5:
            ops = _s5_operators(s5_a_re[j], s5_a_im[j], s5_log_dt[j], s5_b_re[j], s5_b_im[j],
                                s5_c_re[j], s5_c_im[j])
            z_lat, z_ctx = _s5_core(x_lat, x_ctx, mods_all, i, s5_d[j], ops, ctx_out)
            x1_lat = _s5_out(z_lat, w_glu_bf, j, s5_b_glu[j], x_lat, mods_all, i, ln_g[i, 0], ln_b[i, 0], False, nb)
            if ctx_out:
                x1_ctx = _s5_out(z_ctx, w_glu_bf, j, s5_b_glu[j], x_ctx, mods_all, i, ln_g[i, 0], ln_b[i, 0],
                                 True, nb)
        else:
            conv_args = (cv_w_dw[j], cv_b_dw[j], cv_ln_g[j], cv_ln_b[j], w_pw2_bf, j, cv_b_pw2[j])
            h_lat = _pw1(x_lat, mods_all, i, w_pw1_bf, j, cv_b_pw1[j], False, nb)
            x1_lat = _conv_post(h_lat, *conv_args, x_lat, mods_all, i, ln_g[i, 0], ln_b[i, 0], rows, False, nb)
            if ctx_out:
                h_ctx = _pw1(x_ctx, mods_all, i, w_pw1_bf, j, cv_b_pw1[j], True, nb)
                x1_ctx = _conv_post(h_ctx, *conv_args, x_ctx, mods_all, i, ln_g[i, 0], ln_b[i, 0], None, True, nb)

        x_lat, x_ctx_new = _moe_post(x1_lat, x1_ctx, mods_all, i, w_router_t, moe_w_in, moe_w_out,
                                     ln_g[i, 1], ln_b[i, 1], nb)
        if ctx_out:
            x_ctx = x_ctx_new
    return x_lat
```

```python
import functools

import jax
import jax.numpy as jnp
from jax import lax
from jax.experimental import pallas as pl
from jax.experimental.pallas import tpu as pltpu

F32 = jnp.float32
BF16 = jnp.bfloat16

DEPTH = 4
N_MIXERS = 2
GRID_W = 64
S5_H = 16
S5_P = 64
N_EXPERTS = 16
EC_CAPACITY_FACTOR = 2
DEEPNORM_ALPHA = (2.0 * DEPTH) ** 0.25
LN_EPS = 1e-5

LANES = 128
SUBLANES = 8
S5_CHUNK = 8
S5_GPT = LANES // S5_H
S5_STATE_COLS = S5_GPT * S5_P
MOD_ROWS = 8
VMEM_LIMIT = 56 * 1024 * 1024
ROW_TILE = 512
COMBINE_COLS = 1024
TOK_SPLIT_BITS = 6
FF_TILE = 256
CONV_WT = 2 * SUBLANES
CTX_CONV_ROWS = 128
CTX_CONV_COLS = 2 * LANES
ADA_TILE = 2048


def _params(sem, vmem=VMEM_LIMIT):
    return pltpu.CompilerParams(dimension_semantics=sem, vmem_limit_bytes=vmem)


def _layer_norm(v, g, b):
    mu = jnp.mean(v, axis=-1, keepdims=True)
    c = v - mu
    var = jnp.mean(c * c, axis=-1, keepdims=True)
    return c * lax.rsqrt(var + LN_EPS) * g + b


def _split_bf16(v):
    hi = v.astype(BF16)
    lo = (v - hi.astype(F32)).astype(BF16)
    return hi, lo


def _mod_spec(layer, d, k):
    return pl.BlockSpec((1, MOD_ROWS, d), lambda *_: (layer, 0, k))


def _vec_spec(d):
    return pl.BlockSpec((1, d), lambda *_: (0, 0))


def _mod_row(is_ctx, nb):
    return nb if is_ctx else pl.program_id(0)


def _ada_kernel(c_ref, w_ref, b_ref, o_ref):
    c = c_ref[...]
    cond = c * jax.nn.sigmoid(c)
    hi, lo = _split_bf16(cond)
    lhs = jnp.concatenate([hi, lo], axis=0)
    r = jnp.dot(lhs, w_ref[0].astype(BF16), preferred_element_type=F32)
    o_ref[0] = r[:MOD_ROWS] + r[MOD_ROWS:] + b_ref[0]


def _ada_all(c8, ada_w, ada_b):
    depth, d, n = ada_w.shape
    tn = ADA_TILE
    return pl.pallas_call(
        _ada_kernel,
        grid=(depth, n // tn),
        in_specs=[
            pl.BlockSpec((MOD_ROWS, d), lambda i, k: (0, 0)),
            pl.BlockSpec((1, d, tn), lambda i, k: (i, 0, k)),
            pl.BlockSpec((1, 1, tn), lambda i, k: (i, 0, k)),
        ],
        out_specs=pl.BlockSpec((1, MOD_ROWS, tn), lambda i, k: (i, 0, k)),
        out_shape=jax.ShapeDtypeStruct((depth, MOD_ROWS, n), F32),
        compiler_params=_params(("arbitrary", "arbitrary")),
        name="adaln",
    )(c8, ada_w, ada_b.reshape(depth, 1, n))


def _cmul(ar, ai, br, bi):
    return ar * br - ai * bi, ar * bi + ai * br


def _s5_operators(a_re, a_im, log_dt, b_re, b_im, c_re, c_im):
    hp = lax.Precision.HIGHEST
    t = S5_CHUNK
    g = a_re.shape[1]
    nj = g // S5_GPT
    lam_r, lam_i = a_re.astype(F32), a_im.astype(F32)
    dt = jnp.exp(log_dt.astype(F32))[..., None]
    mag = jnp.exp(lam_r * dt)
    abar_r, abar_i = mag * jnp.cos(lam_i * dt), mag * jnp.sin(lam_i * dt)
    den = lam_r * lam_r + lam_i * lam_i
    xr, xi = abar_r - 1.0, abar_i
    coef_r, coef_i = (xr * lam_r + xi * lam_i) / den, (xi * lam_r - xr * lam_i) / den
    bb_r, bb_i = _cmul(coef_r[..., None], coef_i[..., None], b_re.astype(F32), b_im.astype(F32))
    cm_r, cm_i = c_re.astype(F32), c_im.astype(F32)
    pr, pi = [jnp.ones_like(abar_r)], [jnp.zeros_like(abar_r)]
    for _ in range(t):
        nr, ni = _cmul(pr[-1], pi[-1], abar_r, abar_i)
        pr.append(nr)
        pi.append(ni)
    pr, pi = jnp.stack(pr), jnp.stack(pi)

    def lag_kernels(d):
        qr, qi = _cmul(pr[:t, d, :, :, None], pi[:t, d, :, :, None], bb_r[d][None], bb_i[d][None])
        return (jnp.einsum('ghp,kgpj->kghj', cm_r[d], qr, precision=hp)
                - jnp.einsum('ghp,kgpj->kghj', cm_i[d], qi, precision=hp))

    kf, kb = lag_kernels(0), lag_kernels(1)
    kall = jnp.concatenate([kb[:0:-1], (kf[0] + kb[0])[None], kf[1:]], axis=0)
    nlag = 2 * t - 1
    kl = kall.reshape(nlag, nj, S5_GPT, S5_H, S5_H).transpose(1, 0, 2, 4, 3).reshape(nj, nlag * LANES, S5_H)

    def per_tile(re, im, perm, rows, cols):
        both = jnp.stack([re, im], axis=1)
        both = both.reshape(2, 2, nj, S5_GPT, both.shape[-2], both.shape[-1])
        return both.transpose(perm).reshape(nj, rows, cols)

    bt = per_tile(bb_r, bb_i, (2, 0, 1, 3, 5, 4), 4 * LANES, S5_P)
    ct = per_tile(cm_r, cm_i, (2, 0, 1, 3, 5, 4), 4 * S5_STATE_COLS, S5_H)

    exps_b = (jnp.arange(t - 1, -1, -1), jnp.arange(t))
    exps_c = (jnp.arange(1, t + 1), jnp.arange(t, 0, -1))

    def powers(exps):
        both = jnp.stack([jnp.stack([pr[exps[d], d], pi[exps[d], d]]) for d in range(2)])
        return both.reshape(2, 2, t, nj, S5_STATE_COLS)

    pw = powers(exps_b).transpose(3, 0, 1, 2, 4).reshape(nj, 4 * t, S5_STATE_COLS)
    pc = powers(exps_c).transpose(3, 0, 1, 4, 2).reshape(nj, 4 * S5_STATE_COLS, t)

    def decay_tiles(d):
        ar = pr[t, d].reshape(nj, S5_STATE_COLS // LANES, LANES)
        ai = pi[t, d].reshape(nj, S5_STATE_COLS // LANES, LANES)
        return jnp.concatenate([ar, ar, -ai, ai], axis=1)

    at = jnp.concatenate([decay_tiles(0), decay_tiles(1)], axis=1)
    return kl, bt, ct, pw, pc, at


def _iota(shape, axis):
    return lax.broadcasted_iota(jnp.int32, shape, axis)


def _group_blockdiag(comp, width, rows_per_group):
    k = comp.shape[1]
    n = S5_GPT * k
    rep = jnp.where((_iota((k, n), 1) & (k - 1)) == _iota((k, n), 0), 1.0, 0.0).astype(BF16)
    hi = comp.astype(BF16)
    rest = comp - hi.astype(F32)
    mid = rest.astype(BF16)
    lo = (rest - mid.astype(F32)).astype(BF16)
    tiled = (jnp.dot(hi, rep, preferred_element_type=F32) + jnp.dot(mid, rep, preferred_element_type=F32)
             + jnp.dot(lo, rep, preferred_element_type=F32))
    row_grp = (_iota((comp.shape[0], 1), 0) >> (rows_per_group.bit_length() - 1)) & (S5_GPT - 1)
    col_grp = _iota((1, n), 1) >> (width.bit_length() - 1)
    return jnp.where(row_grp == col_grp, tiled, 0.0)


def _build_s5_operators(kl_ref, bt_ref, ct_ref, pw_ref, pc_ref, wm_s, wb_s, wc_s):
    t = S5_CHUNK
    sc = S5_STATE_COLS
    lagk = _group_blockdiag(kl_ref[0, 0], S5_H, S5_H).astype(BF16)
    for s in range(t):
        for u in range(t):
            lag = u - s + t - 1
            wm_s[s * LANES:(s + 1) * LANES, u * LANES:(u + 1) * LANES] = lagk[lag * LANES:(lag + 1) * LANES, :]
    bexp = _group_blockdiag(bt_ref[0, 0], S5_P, S5_H)
    for d in range(2):
        b_re = bexp[(2 * d) * LANES:(2 * d + 1) * LANES]
        b_im = bexp[(2 * d + 1) * LANES:(2 * d + 2) * LANES]
        for s in range(t):
            p_re = pw_ref[0, 0, (2 * d) * t + s:(2 * d) * t + s + 1, :]
            p_im = pw_ref[0, 0, (2 * d + 1) * t + s:(2 * d + 1) * t + s + 1, :]
            rows = slice(s * LANES, (s + 1) * LANES)
            wb_s[rows, d * 2 * sc:d * 2 * sc + sc] = (b_re * p_re - b_im * p_im).astype(BF16)
            wb_s[rows, d * 2 * sc + sc:(d + 1) * 2 * sc] = (b_re * p_im + b_im * p_re).astype(BF16)
    cexp = _group_blockdiag(ct_ref[0, 0], S5_H, S5_P)
    for d in range(2):
        c_re = cexp[(2 * d) * sc:(2 * d + 1) * sc]
        c_im = cexp[(2 * d + 1) * sc:(2 * d + 2) * sc]
        for u in range(t):
            p_re = pc_ref[0, 0, (2 * d) * sc:(2 * d + 1) * sc, u:u + 1]
            p_im = pc_ref[0, 0, (2 * d + 1) * sc:(2 * d + 2) * sc, u:u + 1]
            cols = slice(u * LANES, (u + 1) * LANES)
            wc_s[d * 2 * sc:d * 2 * sc + sc, cols] = (c_re * p_re - c_im * p_im).astype(BF16)
            wc_s[d * 2 * sc + sc:(d + 1) * 2 * sc, cols] = (-(c_re * p_im + c_im * p_re)).astype(BF16)


def _s5_kernel(x_ref, xc_ref, sh_ref, sc_ref, d_ref, kl_ref, bt_ref, ct_ref, pw_ref, pc_ref, at_ref, *rest,
               nb_half, n_lat, n_ctx, ctx_out):
    if ctx_out:
        z_ref, zc_ref, wm_s, wb_s, wc_s, xf_s, st_s, en_s, y_s, zb_s, zcb_s = rest
    else:
        z_ref, wm_s, wb_s, wc_s, xf_s, st_s, en_s, y_s, zb_s = rest
        zc_ref = zcb_s = None
    t = S5_CHUNK
    ncl = n_lat // t
    ncc = n_ctx // t
    lat_rows = nb_half * ncl
    half = pl.program_id(1)
    sc_cols = S5_STATE_COLS
    nst = sc_cols // LANES
    n_rows = nb_half * (ncl + ncc)

    @pl.when(half == 0)
    def _():
        _build_s5_operators(kl_ref, bt_ref, ct_ref, pw_ref, pc_ref, wm_s, wb_s, wc_s)

    for k in range(nb_half):
        b = half * nb_half + k
        scale = 1.0 + sc_ref[0, pl.ds(b, 1), :]
        shift = sh_ref[0, pl.ds(b, 1), :]
        for s in range(t):
            xf_s[k * ncl:(k + 1) * ncl, s * LANES:(s + 1) * LANES] = (
                x_ref[pl.ds(k * n_lat + s, ncl, stride=t), :] * scale + shift)
    scale_c = 1.0 + sc_ref[0, nb_half * 2:nb_half * 2 + 1, :]
    shift_c = sh_ref[0, nb_half * 2:nb_half * 2 + 1, :]
    for k in range(nb_half):
        for s in range(t):
            r0 = lat_rows + k * ncc
            xf_s[r0:r0 + ncc, s * LANES:(s + 1) * LANES] = (
                xc_ref[pl.ds(k * n_ctx + s, ncc, stride=t), :] * scale_c + shift_c)

    xb = xf_s[...].astype(BF16)
    dvec = jnp.concatenate([d_ref[...]] * t, axis=1)
    y_s[...] = jnp.dot(xb, wm_s[...], preferred_element_type=F32) + xf_s[...] * dvec

    for d in range(2):
        local = jnp.dot(xb, wb_s[:, d * 2 * sc_cols:(d + 1) * 2 * sc_cols], preferred_element_type=F32)
        for q in range(2 * nst):
            st_s[d, pl.ds(q, n_rows, stride=2 * nst), :] = local[:, q * LANES:(q + 1) * LANES]
    coef = [(at_ref[0, 0, (2 * d) * 2 * nst:(2 * d + 1) * 2 * nst, :],
             at_ref[0, 0, (2 * d + 1) * 2 * nst:(2 * d + 2) * 2 * nst, :]) for d in range(2)]

    def step(chunk_rows, h):
        new = []
        for d in range(2):
            for k in range(nb_half):
                tile = pl.ds(pl.multiple_of((chunk_rows[d] + k * chunk_rows[2]) * 2 * nst, 2 * nst), 2 * nst)
                prev, prev_sw = h[2 * (d * nb_half + k)], h[2 * (d * nb_half + k) + 1]
                local_state = st_s[d, tile, :]
                en_s[d, tile, :] = prev
                new.append(coef[d][0] * prev + coef[d][1] * prev_sw + local_state)
                new.append(coef[d][0] * prev_sw - coef[d][1] * prev + pltpu.roll(local_state, nst, axis=0))
        return tuple(new)

    zero = tuple(jnp.zeros((2 * nst, LANES), F32) for _ in range(4 * nb_half))
    h = lax.fori_loop(0, ncc, lambda i, h: step((lat_rows + i, lat_rows + ncc - 1 - i, ncc), h), zero, unroll=4)
    lax.fori_loop(0, ncl, lambda i, h: step((i, ncl - 1 - i, ncl), h), h, unroll=4)

    entering = jnp.concatenate([en_s[d, pl.ds(q, n_rows, stride=2 * nst), :]
                                for d in range(2) for q in range(2 * nst)], axis=1)
    y_s[...] += jnp.dot(entering.astype(BF16), wc_s[...], preferred_element_type=F32)

    z = jax.nn.gelu(y_s[...])
    for k in range(nb_half):
        for s in range(t):
            zb_s[pl.ds(k * n_lat + s, ncl, stride=t), :] = z[k * ncl:(k + 1) * ncl, s * LANES:(s + 1) * LANES]
    z_ref[...] = zb_s[...].astype(BF16)
    if ctx_out:
        for k in range(nb_half):
            for s in range(t):
                r0 = lat_rows + k * ncc
                zcb_s[pl.ds(k * n_ctx + s, ncc, stride=t), :] = z[r0:r0 + ncc, s * LANES:(s + 1) * LANES]
        zc_ref[...] = zcb_s[...].astype(BF16)


def _s5_core(x, xc, mods_all, layer, d_skip, ops, s5_layer, ctx_out):
    nb, n_lat, d = x.shape
    n_ctx = xc.shape[1]
    nj = d // LANES
    nb_half = nb // 2
    t = S5_CHUNK
    tl = t * LANES
    rows = nb_half * (n_lat + n_ctx) // t
    kern = functools.partial(_s5_kernel, nb_half=nb_half, n_lat=n_lat, n_ctx=n_ctx, ctx_out=ctx_out)
    out_shape = [jax.ShapeDtypeStruct((nb * n_lat, d), BF16)]
    out_specs = [pl.BlockSpec((nb_half * n_lat, LANES), lambda j, h: (h, j))]
    scratch = [pltpu.VMEM((tl, tl), BF16), pltpu.VMEM((tl, 4 * S5_STATE_COLS), BF16),
               pltpu.VMEM((4 * S5_STATE_COLS, tl), BF16),
               pltpu.VMEM((rows, tl), F32), pltpu.VMEM((2, rows * 2 * S5_STATE_COLS // LANES, LANES), F32),
               pltpu.VMEM((2, rows * 2 * S5_STATE_COLS // LANES, LANES), F32),
               pltpu.VMEM((rows, tl), F32), pltpu.VMEM((nb_half * n_lat, LANES), F32)]
    if ctx_out:
        out_shape.append(jax.ShapeDtypeStruct((nb * n_ctx, d), BF16))
        out_specs.append(pl.BlockSpec((nb_half * n_ctx, LANES), lambda j, h: (h, j)))
        scratch.append(pltpu.VMEM((nb_half * n_ctx, LANES), F32))
    res = pl.pallas_call(
        kern,
        grid=(nj, 2),
        in_specs=[
            pl.BlockSpec((nb_half * n_lat, LANES), lambda j, h: (h, j)),
            pl.BlockSpec((nb_half * n_ctx, LANES), lambda j, h: (h, j)),
            pl.BlockSpec((1, MOD_ROWS, LANES), lambda j, h: (layer, 0, j)),
            pl.BlockSpec((1, MOD_ROWS, LANES), lambda j, h: (layer, 0, nj + j)),
            pl.BlockSpec((1, LANES), lambda j, h: (0, j)),
        ] + [pl.BlockSpec((1, 1) + op.shape[2:], lambda j, h: (s5_layer, j, 0, 0)) for op in ops],
        out_specs=out_specs,
        out_shape=out_shape,
        scratch_shapes=scratch,
        compiler_params=_params(("arbitrary", "arbitrary")),
        name="s5_core",
    )(x.reshape(nb * n_lat, d), xc.reshape(nb * n_ctx, d), mods_all, mods_all, d_skip.reshape(1, d), *ops)
    z = res[0].reshape(nb, n_lat, d)
    zc = res[1].reshape(nb, n_ctx, d) if ctx_out else None
    return z, zc


def _s5_out_kernel(z_ref, w_ref, b_ref, x_ref, g_ref, lg_ref, lb_ref, o_ref, *, is_ctx, nb):
    d = x_ref.shape[-1]
    acc = jnp.dot(z_ref[0], w_ref[0], preferred_element_type=F32) + b_ref[...]
    y = acc[:, :d] * jax.nn.sigmoid(acc[:, d:])
    gate = g_ref[0, pl.ds(_mod_row(is_ctx, nb), 1), :]
    o_ref[0] = _layer_norm(DEEPNORM_ALPHA * x_ref[0] + gate * y, lg_ref[...], lb_ref[...])


def _s5_out(z, w_bf, j, b_glu, x, mods_all, layer, ln_g, ln_b, is_ctx, nb):
    n, t, d = x.shape
    tm = min(t, ROW_TILE)
    kern = functools.partial(_s5_out_kernel, is_ctx=is_ctx, nb=nb)
    tok = pl.BlockSpec((1, tm, d), lambda b, i: (b, i, 0))
    return pl.pallas_call(
        kern,
        grid=(n, t // tm),
        in_specs=[
            tok,
            pl.BlockSpec((1, d, 2 * d), lambda b, i: (j, 0, 0), pipeline_mode=pl.Buffered(1)),
            _vec_spec(2 * d),
            tok,
            _mod_spec(layer, d, 2),
            _vec_spec(d), _vec_spec(d),
        ],
        out_specs=tok,
        out_shape=jax.ShapeDtypeStruct((n, t, d), F32),
        compiler_params=_params(("arbitrary", "arbitrary")),
        name="s5_out",
    )(z, w_bf, b_glu.reshape(1, 2 * d), x, mods_all, ln_g.reshape(1, d), ln_b.reshape(1, d))


def _pw1_kernel(x_ref, sh_ref, sc_ref, w_ref, b_ref, o_ref, *, is_ctx, nb):
    d = x_ref.shape[-1]
    row = _mod_row(is_ctx, nb)
    u = x_ref[0] * (1.0 + sc_ref[0, pl.ds(row, 1), :]) + sh_ref[0, pl.ds(row, 1), :]
    acc = jnp.dot(u.astype(BF16), w_ref[0], preferred_element_type=F32) + b_ref[...]
    o_ref[0] = acc[:, :d] * jax.nn.sigmoid(acc[:, d:])


def _pw1(x, mods_all, layer, w_bf, j, b_pw1, is_ctx, nb):
    n, t, d = x.shape
    tm = min(t, ROW_TILE)
    kern = functools.partial(_pw1_kernel, is_ctx=is_ctx, nb=nb)
    tok = pl.BlockSpec((1, tm, d), lambda b, i: (b, i, 0))
    return pl.pallas_call(
        kern,
        grid=(n, t // tm),
        in_specs=[
            tok,
            _mod_spec(layer, d, 0), _mod_spec(layer, d, 1),
            pl.BlockSpec((1, d, 2 * d), lambda b, i: (j, 0, 0), pipeline_mode=pl.Buffered(1)),
            _vec_spec(2 * d),
        ],
        out_specs=tok,
        out_shape=jax.ShapeDtypeStruct((n, t, d), F32),
        compiler_params=_params(("arbitrary", "arbitrary")),
        name="conv_pw1",
    )(x, mods_all, mods_all, w_bf, b_pw1.reshape(1, 2 * d))


def _conv_tail(cv, cg_ref, cb_ref, w2_ref, b2_ref, x, gate, lg_ref, lb_ref):
    hn = _layer_norm(cv, cg_ref[...], cb_ref[...])
    hn = hn * jax.nn.sigmoid(hn)
    y = jnp.dot(hn.astype(BF16), w2_ref[0], preferred_element_type=F32) + b2_ref[...]
    return _layer_norm(DEEPNORM_ALPHA * x + gate * y, lg_ref[...], lb_ref[...])


def _conv_lat_kernel(h_ref, wdw_ref, bdw_ref, cg_ref, cb_ref, w2_ref, b2_ref, x_ref, g_ref, lg_ref, lb_ref,
                     o_ref, wb_s, cv_s, *, n_rows, wt):
    kw = wdw_ref.shape[0]
    pad = kw // 2
    d = h_ref.shape[-1]
    nsub = wt // SUBLANES
    for k in range(kw):
        wb_s[k] = jnp.broadcast_to(wdw_ref[k:k + 1, :], (SUBLANES, d))
    bias = bdw_ref[...]
    for r in range(n_rows):
        accs = [None] * nsub
        for k in range(max(0, pad - r), min(kw, n_rows + pad - r)):
            w8 = wb_s[k]
            for q in range(nsub):
                term = w8 * h_ref[0, r + k - pad, q * SUBLANES:(q + 1) * SUBLANES, :]
                accs[q] = term if accs[q] is None else accs[q] + term
        for q in range(nsub):
            cv_s[r * wt + q * SUBLANES:r * wt + (q + 1) * SUBLANES, :] = accs[q] + bias
    gate = g_ref[0, pl.ds(pl.program_id(0), 1), :]
    x = x_ref[0].reshape(n_rows * wt, d)
    out = _conv_tail(cv_s[...], cg_ref, cb_ref, w2_ref, b2_ref, x, gate, lg_ref, lb_ref)
    o_ref[0] = out.reshape(n_rows, wt, d)


def _conv_ctx_kernel(h_ref, wdw_ref, bdw_ref, cg_ref, cb_ref, w2_ref, b2_ref, x_ref, g_ref, lg_ref, lb_ref,
                     o_ref, hp_s, cv_s, *, n_tok, nb):
    kw = wdw_ref.shape[0]
    pad = kw // 2
    d = h_ref.shape[-1]
    lead = 2 * SUBLANES
    rblk = CTX_CONV_ROWS
    cblk = CTX_CONV_COLS
    hp_s[0:lead] = jnp.zeros((lead, d), F32)
    hp_s[lead + n_tok:lead + n_tok + lead] = jnp.zeros((lead, d), F32)
    hp_s[lead:lead + n_tok] = h_ref[0]
    span = ((kw - 1 + lead - pad) // SUBLANES) * SUBLANES

    def col_block(lc, carry):
        cols = pl.ds(pl.multiple_of(lc * cblk, cblk), cblk)
        for rc in range(n_tok // rblk):
            acc = jnp.zeros((rblk, cblk), F32)
            for q in range(SUBLANES):
                taps = [k for k in range(kw) if (k + lead - pad) % SUBLANES == q]
                if not taps:
                    continue
                shifted = hp_s[pl.ds(rc * rblk + q, rblk + span), cols]
                for k in taps:
                    o = k + lead - pad - q
                    acc = acc + wdw_ref[k:k + 1, cols] * shifted[o:o + rblk]
            cv_s[rc * rblk:(rc + 1) * rblk, cols] = acc + bdw_ref[:, cols]
        return carry

    lax.fori_loop(0, d // cblk, col_block, 0)
    gate = g_ref[0, nb:nb + 1, :]
    o_ref[0] = _conv_tail(cv_s[...], cg_ref, cb_ref, w2_ref, b2_ref, x_ref[0], gate, lg_ref, lb_ref)


def _conv_post(h, w_dw, b_dw, cv_g, cv_b, w2_bf, j, b2, x, mods_all, layer, ln_g, ln_b, n_rows, is_ctx, nb):
    n, t, d = x.shape
    kw = w_dw.shape[0]
    weights = [pl.BlockSpec((kw, d), lambda b, i: (0, 0)), _vec_spec(d), _vec_spec(d), _vec_spec(d),
               pl.BlockSpec((1, d, d), lambda b, i: (j, 0, 0), pipeline_mode=pl.Buffered(1)), _vec_spec(d)]
    tail = [_mod_spec(layer, d, 2), _vec_spec(d), _vec_spec(d)]
    args_w = (w_dw, b_dw.reshape(1, d), cv_g.reshape(1, d), cv_b.reshape(1, d), w2_bf, b2.reshape(1, d))
    args_t = (mods_all, ln_g.reshape(1, d), ln_b.reshape(1, d))
    if is_ctx:
        blk = pl.BlockSpec((1, t, d), lambda b, i: (b, 0, 0))
        return pl.pallas_call(
            functools.partial(_conv_ctx_kernel, n_tok=t, nb=nb),
            grid=(n, 1),
            in_specs=[blk] + weights + [blk] + tail,
            out_specs=blk,
            out_shape=jax.ShapeDtypeStruct((n, t, d), F32),
            scratch_shapes=[pltpu.VMEM((t + 4 * SUBLANES, d), F32), pltpu.VMEM((t, d), F32)],
            compiler_params=_params(("arbitrary", "arbitrary")),
            name="conv_post_ctx",
        )(h, *args_w, x, *args_t)
    wt = CONV_WT
    width = t // n_rows
    blk = pl.BlockSpec((1, n_rows, wt, d), lambda b, i: (b, 0, i, 0))
    out = pl.pallas_call(
        functools.partial(_conv_lat_kernel, n_rows=n_rows, wt=wt),
        grid=(n, width // wt),
        in_specs=[blk] + weights + [blk] + tail,
        out_specs=blk,
        out_shape=jax.ShapeDtypeStruct((n, n_rows, width, d), F32),
        scratch_shapes=[pltpu.VMEM((kw, SUBLANES, d), F32),
                        pltpu.VMEM((n_rows * wt, d), F32)],
        compiler_params=_params(("arbitrary", "arbitrary")),
        name="conv_post",
    )(h.reshape(n, n_rows, width, d), *args_w, x.reshape(n, n_rows, width, d), *args_t)
    return out.reshape(n, t, d)


def _router_kernel(x_ref, sh_ref, sc_ref, wr_ref, a_ref, *, is_ctx, nb):
    row = _mod_row(is_ctx, nb)
    u = x_ref[0] * (1.0 + sc_ref[0, pl.ds(row, 1), :]) + sh_ref[0, pl.ds(row, 1), :]
    uh, ul = _split_bf16(u)
    wh, wl = _split_bf16(wr_ref[0])
    nt = (((1,), (1,)), ((), ()))
    logits = (lax.dot_general(wh, uh, nt, preferred_element_type=F32)
              + lax.dot_general(wh, ul, nt, preferred_element_type=F32)
              + lax.dot_general(wl, uh, nt, preferred_element_type=F32))
    m = jnp.max(logits, axis=0, keepdims=True)
    ex = jnp.exp(logits - m)
    a_ref[0] = ex / jnp.sum(ex, axis=0, keepdims=True)


def _router(x, mods_all, layer, w_router_t, is_ctx, nb):
    n, t, d = x.shape
    e = w_router_t.shape[1]
    tm = min(t, ROW_TILE)
    kern = functools.partial(_router_kernel, is_ctx=is_ctx, nb=nb)
    tok = pl.BlockSpec((1, tm, d), lambda b, i: (b, i, 0))
    return pl.pallas_call(
        kern,
        grid=(n, t // tm),
        in_specs=[tok, _mod_spec(layer, d, 3), _mod_spec(layer, d, 4),
                  pl.BlockSpec((1, e, d), lambda b, i: (layer, 0, 0))],
        out_specs=pl.BlockSpec((1, e, tm), lambda b, i: (b, 0, i)),
        out_shape=jax.ShapeDtypeStruct((n, e, t), F32),
        compiler_params=_params(("arbitrary", "arbitrary")),
        name="router",
    )(x, mods_all, mods_all, w_router_t)


def _topk_kernel(a_ref, pt_ref, idx_ref, gate_ref, *, cap):
    a = a_ref[0]
    e, t = a.shape
    capf = jnp.float32(cap)

    def count(mask):
        return jnp.sum(jnp.where(mask, 1.0, 0.0), axis=1, keepdims=True)

    def as_row_values(bits):
        return jnp.concatenate([pltpu.bitcast(bits, F32)] * (t // LANES), axis=1)

    thr_bits = jnp.zeros((e, LANES), jnp.int32)
    for bit in range(30, -1, -1):
        cand = thr_bits | jnp.int32(1 << bit)
        keep = count(a >= as_row_values(cand)) >= capf
        thr_bits = jnp.where(keep, cand, thr_bits)
    thr = as_row_values(thr_bits)
    gt = a > thr
    eq = a == thr
    need = capf - count(gt)
    tri = jnp.where(_iota((t, t), 0) <= _iota((t, t), 1), 1.0, 0.0).astype(BF16)
    eq_f = jnp.where(eq, 1.0, 0.0)
    eq_rank = jnp.dot(eq_f.astype(BF16), tri, preferred_element_type=F32) - eq_f
    sel = jnp.where(gt, 1.0, jnp.where(eq & (eq_rank < need), 1.0, 0.0))
    slot = jnp.dot(sel.astype(BF16), tri, preferred_element_type=F32) - 1.0
    pos = jnp.where(sel > 0.0, slot, -1.0)
    padded = jnp.concatenate([pos, jnp.full((LANES - e, t), -1.0, F32)], axis=0)
    pos_tm = padded.T
    pt_ref[0] = pos_tm.astype(jnp.int32)
    tok = _iota((1, t), 1)
    tok_hi = (tok >> TOK_SPLIT_BITS).astype(F32)
    tok_lo = (tok & ((1 << TOK_SPLIT_BITS) - 1)).astype(F32)
    slots = _iota((cap, 1), 0).astype(F32)
    contract_tokens = (((1,), (1,)), ((), ()))
    for ei in range(e):
        onehot = jnp.where(pos[ei:ei + 1, :] == slots, 1.0, 0.0).astype(BF16)
        g = a[ei:ei + 1, :]
        g_hi = g.astype(BF16).astype(F32)
        g_mid = (g - g_hi).astype(BF16).astype(F32)
        g_lo = (g - g_hi) - g_mid
        lhs = jnp.concatenate([tok_hi, tok_lo, g_hi, g_mid, g_lo, jnp.zeros((SUBLANES - 5, t), F32)], axis=0)
        res = lax.dot_general(lhs.astype(BF16), onehot, contract_tokens, preferred_element_type=F32)
        idx_ref[0, ei:ei + 1, :] = (res[0:1] * float(1 << TOK_SPLIT_BITS) + res[1:2]).astype(jnp.int32)
        gate_ref[0, ei:ei + 1, :] = res[2:3] + res[3:4] + res[4:5]


def _topk(aff_t, cap):
    n, e, t = aff_t.shape
    kern = functools.partial(_topk_kernel, cap=cap)
    return pl.pallas_call(
        kern,
        grid=(n,),
        in_specs=[pl.BlockSpec((1, e, t), lambda b: (b, 0, 0))],
        out_specs=[pl.BlockSpec((1, t, LANES), lambda b: (b, 0, 0)),
                   pl.BlockSpec((1, e, cap), lambda b: (b, 0, 0)),
                   pl.BlockSpec((1, e, cap), lambda b: (b, 0, 0))],
        out_shape=[jax.ShapeDtypeStruct((n, t, LANES), jnp.int32), jax.ShapeDtypeStruct((n, e, cap), jnp.int32),
                   jax.ShapeDtypeStruct((n, e, cap), F32)],
        compiler_params=_params(("arbitrary",)),
        name="topk",
    )(aff_t)


def _ffn_kernel(idx_ref, *refs, rows, caps, mod_rows, n_exp):
    n_streams = len(rows)
    src_refs = refs[:n_streams]
    sh_ref, sc_ref, gate_ref, wg_ref, wu_ref, wo_ref = refs[n_streams:n_streams + 6]
    y_refs = refs[n_streams + 6:2 * n_streams + 6]
    act_s, wo_s, x_s, land_s, sem = refs[2 * n_streams + 6:]
    e = pl.program_id(0)
    fc = pl.program_id(1)
    nfc = act_s.shape[0]
    r_tot = sum(rows)
    bounds = [0]
    for r in rows:
        bounds.append(bounds[-1] + r)

    def row_copy(expert, row, k):
        tok = idx_ref[expert * r_tot + row]
        return pltpu.make_async_copy(src_refs[k].at[pl.ds(tok, 1), :], land_s.at[pl.ds(row, 1), :], sem)

    def wait_all_rows():
        pltpu.make_async_copy(src_refs[0].at[pl.ds(0, r_tot), :], land_s, sem).wait()

    @pl.when((e == 0) & (fc == 0))
    def _():
        for k in range(n_streams):
            def issue(row, carry, k=k):
                row_copy(0, row, k).start()
                return carry
            lax.fori_loop(bounds[k], bounds[k + 1], issue, 0)

    @pl.when(fc == 0)
    def _():
        wait_all_rows()
        for k in range(n_streams):
            for b, m in enumerate(mod_rows[k]):
                r0 = bounds[k] + b * caps[k]
                scale = 1.0 + sc_ref[0, m:m + 1, :]
                x_s[r0:r0 + caps[k], :] = (land_s[r0:r0 + caps[k], :] * scale + sh_ref[0, m:m + 1, :]).astype(BF16)

    nxt = lax.rem(e + 1, n_exp)
    for k in range(n_streams):
        share = rows[k] // nfc
        for i in range(share):
            row_copy(nxt, bounds[k] + fc * share + i, k).start()

    x = x_s[...]
    g = jnp.dot(x, wg_ref[0, 0].astype(BF16), preferred_element_type=F32)
    up = jnp.dot(x, wu_ref[0, 0].astype(BF16), preferred_element_type=F32)
    act_s[fc] = ((g * jax.nn.sigmoid(g)) * up).astype(BF16)
    wo_s[fc] = wo_ref[0, 0].astype(BF16)

    @pl.when(fc == nfc - 1)
    def _():
        w_out = wo_s[...].reshape(nfc * wo_s.shape[1], wo_s.shape[2])
        gate_col = jnp.broadcast_to(gate_ref[0], (LANES, r_tot)).T[:, 0:1]
        half = r_tot // 2
        for r0, r1 in ((0, half), (half, r_tot)):
            act = jnp.concatenate([act_s[c, r0:r1, :] for c in range(nfc)], axis=1)
            y = (jnp.dot(act, w_out, preferred_element_type=F32) * gate_col[r0:r1]).astype(BF16)
            for k in range(n_streams):
                lo, hi = max(r0, bounds[k]), min(r1, bounds[k + 1])
                if lo < hi:
                    y_refs[k][0, lo - bounds[k]:hi - bounds[k], :] = y[lo - r0:hi - r0]

    @pl.when((e == n_exp - 1) & (fc == nfc - 1))
    def _():
        wait_all_rows()


def _ffn(streams, mod_rows, mods_all, idx, gates, w_in, w_out, layer):
    d = streams[0].shape[2]
    e = gates.shape[0]
    ff = w_out.shape[2]
    nfc = ff // FF_TILE
    caps = tuple(EC_CAPACITY_FACTOR * s.shape[1] // N_EXPERTS for s in streams)
    rows = tuple(s.shape[0] * c for s, c in zip(streams, caps))
    r_tot = sum(rows)
    assert all(r % nfc == 0 for r in rows) and r_tot % LANES == 0
    return pl.pallas_call(
        functools.partial(_ffn_kernel, rows=rows, caps=caps, mod_rows=mod_rows, n_exp=e),
        grid_spec=pltpu.PrefetchScalarGridSpec(
            num_scalar_prefetch=1,
            grid=(e, nfc),
            in_specs=[pl.BlockSpec(memory_space=pl.ANY) for _ in streams] + [
                pl.BlockSpec((1, MOD_ROWS, d), lambda ei, fc, ix: (layer, 0, 3)),
                pl.BlockSpec((1, MOD_ROWS, d), lambda ei, fc, ix: (layer, 0, 4)),
                pl.BlockSpec((1, 1, r_tot), lambda ei, fc, ix: (ei, 0, 0)),
                pl.BlockSpec((1, 1, d, FF_TILE), lambda ei, fc, ix: (layer, ei, 0, fc)),
                pl.BlockSpec((1, 1, d, FF_TILE), lambda ei, fc, ix: (layer, ei, 0, nfc + fc)),
                pl.BlockSpec((1, 1, FF_TILE, d), lambda ei, fc, ix: (layer, ei, fc, 0))],
            out_specs=[pl.BlockSpec((1, r, d), lambda ei, fc, ix: (ei, 0, 0)) for r in rows],
            scratch_shapes=[pltpu.VMEM((nfc, r_tot, FF_TILE), BF16), pltpu.VMEM((nfc, FF_TILE, d), BF16),
                            pltpu.VMEM((r_tot, d), BF16), pltpu.VMEM((r_tot, d), F32),
                            pltpu.SemaphoreType.DMA(())],
        ),
        out_shape=[jax.ShapeDtypeStruct((e, r, d), BF16) for r in rows],
        compiler_params=_params(("arbitrary", "arbitrary")),
        name="expert_ffn",
    )(idx, *[s.reshape(s.shape[0] * s.shape[1], d) for s in streams], mods_all, mods_all, gates, w_in, w_in, w_out)


def _combine_kernel(y_ref, pt_ref, x_ref, g_ref, lg_ref, lb_ref, o_ref, oh_s, *, is_ctx, nb, cap):
    n_exp, _, d = y_ref.shape
    ec = n_exp * cap
    cw = min(ec, COMBINE_COLS)
    pos = pt_ref[0].astype(F32).astype(BF16)
    for c0 in range(0, ec, cw):
        src = _iota((LANES, cw), 0)
        dst = _iota((LANES, cw), 1) + c0
        expand = jnp.where((dst >= src * cap) & (dst < (src + 1) * cap), 1.0, 0.0).astype(BF16)
        pe = jnp.dot(pos, expand, preferred_element_type=F32)
        slot = (_iota((1, cw), 1) + c0) % cap
        oh_s[:, c0:c0 + cw] = jnp.where(pe == slot.astype(F32), 1.0, 0.0).astype(BF16)
    f = jnp.dot(oh_s[...], y_ref[...].reshape(ec, d), preferred_element_type=F32)
    gate = g_ref[0, pl.ds(_mod_row(is_ctx, nb), 1), :]
    o_ref[0] = _layer_norm(DEEPNORM_ALPHA * x_ref[0] + gate * f, lg_ref[...], lb_ref[...])


def _combine(y, pos_tm, x, mods_all, layer, ln_g, ln_b, cap, is_ctx, nb):
    n, t, d = x.shape
    e = y.shape[0]
    tm = min(t, ROW_TILE)
    kern = functools.partial(_combine_kernel, is_ctx=is_ctx, nb=nb, cap=cap)
    tok = pl.BlockSpec((1, tm, d), lambda b, i: (b, i, 0))
    return pl.pallas_call(
        kern,
        grid=(n, t // tm),
        in_specs=[
            pl.BlockSpec((e, cap, d), lambda b, i: (0, b, 0), pipeline_mode=pl.Buffered(1)),
            pl.BlockSpec((1, tm, LANES), lambda b, i: (b, i, 0)),
            tok,
            _mod_spec(layer, d, 5),
            _vec_spec(d), _vec_spec(d),
        ],
        out_specs=tok,
        out_shape=jax.ShapeDtypeStruct((n, t, d), F32),
        scratch_shapes=[pltpu.VMEM((tm, e * cap), BF16)],
        compiler_params=_params(("arbitrary", "arbitrary")),
        name="combine",
    )(y, pos_tm, x, mods_all, ln_g.reshape(1, d), ln_b.reshape(1, d))


def _moe_post(x1_lat, x1_ctx, mods_all, layer, w_router_t, w_in, w_out, ln_g, ln_b, nb):
    streams = [(x1_lat, False)] + ([(x1_ctx, True)] if x1_ctx is not None else [])
    caps = [EC_CAPACITY_FACTOR * s.shape[1] // N_EXPERTS for s, _ in streams]
    routed, tokens, mod_rows, idx_parts, gate_parts = [], [], [], [], []
    for (s, is_ctx), cap in zip(streams, caps):
        n, t = s.shape[:2]
        aff_t = _router(s, mods_all, layer, w_router_t, is_ctx, nb)
        pos_tm, idx, gate = _topk(aff_t, cap)
        routed.append(pos_tm)
        tokens.append(s)
        mod_rows.append(tuple(nb if is_ctx else b for b in range(n)))
        flat = idx + (jnp.arange(n, dtype=jnp.int32) * t)[:, None, None]
        idx_parts.append(jnp.swapaxes(flat, 0, 1).reshape(N_EXPERTS, n * cap))
        gate_parts.append(jnp.swapaxes(gate, 0, 1).reshape(N_EXPERTS, n * cap))
    idx_all = jnp.concatenate(idx_parts, axis=1).reshape(-1)
    gates = jnp.concatenate(gate_parts, axis=1)[:, None, :]
    ys = _ffn(tokens, tuple(mod_rows), mods_all, idx_all, gates, w_in, w_out, layer)
    outs = [_combine(y, pos_tm, s, mods_all, layer, ln_g, ln_b, cap, is_ctx, nb)
            for (s, is_ctx), pos_tm, cap, y in zip(streams, routed, caps, ys)]
    return outs[0], (outs[1] if len(outs) > 1 else None)


def kernel(x, c, ctx, c_ctx, ada_w, ada_b, ln_g, ln_b, s5_a_re, s5_a_im, s5_log_dt, s5_b_re, s5_b_im,
           s5_c_re, s5_c_im, s5_d, s5_w_glu, s5_b_glu, cv_w_pw1, cv_b_pw1, cv_w_dw, cv_b_dw, cv_ln_g,
           cv_ln_b, cv_w_pw2, cv_b_pw2, moe_w_router, moe_w_in, moe_w_out):
    nb, seq, d = x.shape
    assert nb + 1 <= MOD_ROWS and nb % 2 == 0 and d % LANES == 0
    rows = seq // GRID_W

    c8 = jnp.concatenate([c.astype(F32), c_ctx.astype(F32)[None], jnp.zeros((MOD_ROWS - nb - 1, d), F32)], axis=0)
    mods_all = _ada_all(c8, ada_w, ada_b)
    w_glu_bf = s5_w_glu.astype(BF16)
    w_pw1_bf = cv_w_pw1.astype(BF16)
    w_pw2_bf = cv_w_pw2.astype(BF16)
    w_router_t = jnp.swapaxes(moe_w_router, 1, 2)
    s5_ops = jax.vmap(_s5_operators)(s5_a_re, s5_a_im, s5_log_dt, s5_b_re, s5_b_im, s5_c_re, s5_c_im)

    x_lat, x_ctx = x, ctx
    for i in range(DEPTH):
        is_s5 = (i % N_MIXERS) == 0
        j = i // N_MIXERS
        ctx_out = any((k % N_MIXERS) == 0 for k in range(i + 1, DEPTH))
        x1_ctx = None

        if is_s5:
            z_lat, z_ctx = _s5_core(x_lat, x_ctx, mods_all, i, s5_d[j], s5_ops, j, ctx_out)
            x1_lat = _s5_out(z_lat, w_glu_bf, j, s5_b_glu[j], x_lat, mods_all, i, ln_g[i, 0], ln_b[i, 0], False, nb)
            if ctx_out:
                x1_ctx = _s5_out(z_ctx, w_glu_bf, j, s5_b_glu[j], x_ctx, mods_all, i, ln_g[i, 0], ln_b[i, 0],
                                 True, nb)
        else:
            conv_args = (cv_w_dw[j], cv_b_dw[j], cv_ln_g[j], cv_ln_b[j], w_pw2_bf, j, cv_b_pw2[j])
            h_lat = _pw1(x_lat, mods_all, i, w_pw1_bf, j, cv_b_pw1[j], False, nb)
            x1_lat = _conv_post(h_lat, *conv_args, x_lat, mods_all, i, ln_g[i, 0], ln_b[i, 0], rows, False, nb)
            if ctx_out:
                h_ctx = _pw1(x_ctx, mods_all, i, w_pw1_bf, j, cv_b_pw1[j], True, nb)
                x1_ctx = _conv_post(h_ctx, *conv_args, x_ctx, mods_all, i, ln_g[i, 0], ln_b[i, 0], None, True, nb)

        x_lat, x_ctx_new = _moe_post(x1_lat, x1_ctx, mods_all, i, w_router_t, moe_w_in, moe_w_out,
                                     ln_g[i, 1], ln_b[i, 1], nb)
        if ctx_out:
            x_ctx = x_ctx_new
    return x_lat
```

```python
import functools

import jax
import jax.numpy as jnp
from jax import lax
from jax.experimental import pallas as pl
from jax.experimental.pallas import tpu as pltpu

F32 = jnp.float32
BF16 = jnp.bfloat16

DEPTH = 4
N_MIXERS = 2
GRID_W = 64
S5_H = 16
S5_P = 64
N_EXPERTS = 16
EC_CAPACITY_FACTOR = 2
DEEPNORM_ALPHA = (2.0 * DEPTH) ** 0.25
LN_EPS = 1e-5

LANES = 128
SUBLANES = 8
S5_CHUNK = 8
S5_GPT = LANES // S5_H
S5_STATE_COLS = S5_GPT * S5_P
MOD_ROWS = 8
VMEM_LIMIT = 56 * 1024 * 1024
ROW_TILE = 512
COMBINE_COLS = 1024
TOK_SPLIT_BITS = 6
FF_TILE = 256
CONV_WT = 2 * SUBLANES
CTX_CONV_ROWS = 128
CTX_CONV_COLS = 2 * LANES
ADA_TILE = 2048


def _params(sem, vmem=VMEM_LIMIT):
    return pltpu.CompilerParams(dimension_semantics=sem, vmem_limit_bytes=vmem)


def _layer_norm(v, g, b):
    mu = jnp.mean(v, axis=-1, keepdims=True)
    c = v - mu
    var = jnp.mean(c * c, axis=-1, keepdims=True)
    return c * lax.rsqrt(var + LN_EPS) * g + b


def _split_bf16(v):
    hi = v.astype(BF16)
    lo = (v - hi.astype(F32)).astype(BF16)
    return hi, lo


def _mod_spec(layer, d, k):
    return pl.BlockSpec((1, MOD_ROWS, d), lambda *_: (layer, 0, k))


def _vec_spec(d):
    return pl.BlockSpec((1, d), lambda *_: (0, 0))


def _mod_row(is_ctx, nb):
    return nb if is_ctx else pl.program_id(0)


def _ada_kernel(c_ref, w_ref, b_ref, o_ref):
    c = c_ref[...]
    cond = c * jax.nn.sigmoid(c)
    hi, lo = _split_bf16(cond)
    lhs = jnp.concatenate([hi, lo], axis=0)
    r = jnp.dot(lhs, w_ref[0].astype(BF16), preferred_element_type=F32)
    o_ref[0] = r[:MOD_ROWS] + r[MOD_ROWS:] + b_ref[0]


def _ada_all(c8, ada_w, ada_b):
    depth, d, n = ada_w.shape
    tn = ADA_TILE
    return pl.pallas_call(
        _ada_kernel,
        grid=(depth, n // tn),
        in_specs=[
            pl.BlockSpec((MOD_ROWS, d), lambda i, k: (0, 0)),
            pl.BlockSpec((1, d, tn), lambda i, k: (i, 0, k)),
            pl.BlockSpec((1, 1, tn), lambda i, k: (i, 0, k)),
        ],
        out_specs=pl.BlockSpec((1, MOD_ROWS, tn), lambda i, k: (i, 0, k)),
        out_shape=jax.ShapeDtypeStruct((depth, MOD_ROWS, n), F32),
        compiler_params=_params(("arbitrary", "arbitrary")),
        name="adaln",
    )(c8, ada_w, ada_b.reshape(depth, 1, n))


def _cmul(ar, ai, br, bi):
    return ar * br - ai * bi, ar * bi + ai * br


def _s5_operators(a_re, a_im, log_dt, b_re, b_im, c_re, c_im):
    hp = lax.Precision.HIGHEST
    t = S5_CHUNK
    g = a_re.shape[1]
    nj = g // S5_GPT
    lam_r, lam_i = a_re.astype(F32), a_im.astype(F32)
    dt = jnp.exp(log_dt.astype(F32))[..., None]
    mag = jnp.exp(lam_r * dt)
    abar_r, abar_i = mag * jnp.cos(lam_i * dt), mag * jnp.sin(lam_i * dt)
    den = lam_r * lam_r + lam_i * lam_i
    xr, xi = abar_r - 1.0, abar_i
    coef_r, coef_i = (xr * lam_r + xi * lam_i) / den, (xi * lam_r - xr * lam_i) / den
    bb_r, bb_i = _cmul(coef_r[..., None], coef_i[..., None], b_re.astype(F32), b_im.astype(F32))
    cm_r, cm_i = c_re.astype(F32), c_im.astype(F32)
    pr, pi = [jnp.ones_like(abar_r)], [jnp.zeros_like(abar_r)]
    for _ in range(t):
        nr, ni = _cmul(pr[-1], pi[-1], abar_r, abar_i)
        pr.append(nr)
        pi.append(ni)
    pr, pi = jnp.stack(pr), jnp.stack(pi)

    def lag_kernels(d):
        qr, qi = _cmul(pr[:t, d, :, :, None], pi[:t, d, :, :, None], bb_r[d][None], bb_i[d][None])
        return (jnp.einsum('ghp,kgpj->kghj', cm_r[d], qr, precision=hp)
                - jnp.einsum('ghp,kgpj->kghj', cm_i[d], qi, precision=hp))

    kf, kb = lag_kernels(0), lag_kernels(1)
    kall = jnp.concatenate([kb[:0:-1], (kf[0] + kb[0])[None], kf[1:]], axis=0)
    nlag = 2 * t - 1
    kl = kall.reshape(nlag, nj, S5_GPT, S5_H, S5_H).transpose(1, 0, 2, 4, 3).reshape(nj, nlag * LANES, S5_H)

    def per_tile(re, im, perm, rows, cols):
        both = jnp.stack([re, im], axis=1)
        both = both.reshape(2, 2, nj, S5_GPT, both.shape[-2], both.shape[-1])
        return both.transpose(perm).reshape(nj, rows, cols)

    bt = per_tile(bb_r, bb_i, (2, 0, 1, 3, 5, 4), 4 * LANES, S5_P)
    ct = per_tile(cm_r, cm_i, (2, 0, 1, 3, 5, 4), 4 * S5_STATE_COLS, S5_H)

    exps_b = (jnp.arange(t - 1, -1, -1), jnp.arange(t))
    exps_c = (jnp.arange(1, t + 1), jnp.arange(t, 0, -1))

    def powers(exps):
        both = jnp.stack([jnp.stack([pr[exps[d], d], pi[exps[d], d]]) for d in range(2)])
        return both.reshape(2, 2, t, nj, S5_STATE_COLS)

    pw = powers(exps_b).transpose(3, 0, 1, 2, 4).reshape(nj, 4 * t, S5_STATE_COLS)
    pc = powers(exps_c).transpose(3, 0, 1, 4, 2).reshape(nj, 4 * S5_STATE_COLS, t)

    def decay_tiles(d):
        ar = pr[t, d].reshape(nj, S5_STATE_COLS // LANES, LANES)
        ai = pi[t, d].reshape(nj, S5_STATE_COLS // LANES, LANES)
        return jnp.concatenate([ar, ar, -ai, ai], axis=1)

    at = jnp.concatenate([decay_tiles(0), decay_tiles(1)], axis=1)
    return kl, bt, ct, pw, pc, at


def _iota(shape, axis):
    return lax.broadcasted_iota(jnp.int32, shape, axis)


def _group_blockdiag(comp, width, rows_per_group):
    k = comp.shape[1]
    n = S5_GPT * k
    rep = jnp.where((_iota((k, n), 1) & (k - 1)) == _iota((k, n), 0), 1.0, 0.0).astype(BF16)
    hi = comp.astype(BF16)
    rest = comp - hi.astype(F32)
    mid = rest.astype(BF16)
    lo = (rest - mid.astype(F32)).astype(BF16)
    tiled = (jnp.dot(hi, rep, preferred_element_type=F32) + jnp.dot(mid, rep, preferred_element_type=F32)
             + jnp.dot(lo, rep, preferred_element_type=F32))
    row_grp = (_iota((comp.shape[0], 1), 0) >> (rows_per_group.bit_length() - 1)) & (S5_GPT - 1)
    col_grp = _iota((1, n), 1) >> (width.bit_length() - 1)
    return jnp.where(row_grp == col_grp, tiled, 0.0)


def _build_s5_operators(kl_ref, bt_ref, ct_ref, pw_ref, pc_ref, wm_s, wb_s, wc_s):
    t = S5_CHUNK
    sc = S5_STATE_COLS
    lagk = _group_blockdiag(kl_ref[0, 0], S5_H, S5_H).astype(BF16)
    for s in range(t):
        for u in range(t):
            lag = u - s + t - 1
            wm_s[s * LANES:(s + 1) * LANES, u * LANES:(u + 1) * LANES] = lagk[lag * LANES:(lag + 1) * LANES, :]
    bexp = _group_blockdiag(bt_ref[0, 0], S5_P, S5_H)
    for d in range(2):
        b_re = bexp[(2 * d) * LANES:(2 * d + 1) * LANES]
        b_im = bexp[(2 * d + 1) * LANES:(2 * d + 2) * LANES]
        for s in range(t):
            p_re = pw_ref[0, 0, (2 * d) * t + s:(2 * d) * t + s + 1, :]
            p_im = pw_ref[0, 0, (2 * d + 1) * t + s:(2 * d + 1) * t + s + 1, :]
            rows = slice(s * LANES, (s + 1) * LANES)
            wb_s[rows, d * 2 * sc:d * 2 * sc + sc] = (b_re * p_re - b_im * p_im).astype(BF16)
            wb_s[rows, d * 2 * sc + sc:(d + 1) * 2 * sc] = (b_re * p_im + b_im * p_re).astype(BF16)
    cexp = _group_blockdiag(ct_ref[0, 0], S5_H, S5_P)
    for d in range(2):
        c_re = cexp[(2 * d) * sc:(2 * d + 1) * sc]
        c_im = cexp[(2 * d + 1) * sc:(2 * d + 2) * sc]
        for u in range(t):
            p_re = pc_ref[0, 0, (2 * d) * sc:(2 * d + 1) * sc, u:u + 1]
            p_im = pc_ref[0, 0, (2 * d + 1) * sc:(2 * d + 2) * sc, u:u + 1]
            cols = slice(u * LANES, (u + 1) * LANES)
            wc_s[d * 2 * sc:d * 2 * sc + sc, cols] = (c_re * p_re - c_im * p_im).astype(BF16)
            wc_s[d * 2 * sc + sc:(d + 1) * 2 * sc, cols] = (-(c_re * p_im + c_im * p_re)).astype(BF16)


def _s5_kernel(x_ref, xc_ref, sh_ref, sc_ref, d_ref, kl_ref, bt_ref, ct_ref, pw_ref, pc_ref, at_ref, *rest,
               nb_half, n_lat, n_ctx, ctx_out):
    if ctx_out:
        z_ref, zc_ref, wm_s, wb_s, wc_s, xf_s, st_s, en_s, y_s, zb_s, zcb_s = rest
    else:
        z_ref, wm_s, wb_s, wc_s, xf_s, st_s, en_s, y_s, zb_s = rest
        zc_ref = zcb_s = None
    t = S5_CHUNK
    ncl = n_lat // t
    ncc = n_ctx // t
    lat_rows = nb_half * ncl
    half = pl.program_id(1)
    sc_cols = S5_STATE_COLS
    nst = sc_cols // LANES
    n_rows = nb_half * (ncl + ncc)

    @pl.when(half == 0)
    def _():
        _build_s5_operators(kl_ref, bt_ref, ct_ref, pw_ref, pc_ref, wm_s, wb_s, wc_s)

    for k in range(nb_half):
        b = half * nb_half + k
        scale = 1.0 + sc_ref[0, pl.ds(b, 1), :]
        shift = sh_ref[0, pl.ds(b, 1), :]
        for s in range(t):
            xf_s[k * ncl:(k + 1) * ncl, s * LANES:(s + 1) * LANES] = (
                x_ref[pl.ds(k * n_lat + s, ncl, stride=t), :] * scale + shift)
    scale_c = 1.0 + sc_ref[0, nb_half * 2:nb_half * 2 + 1, :]
    shift_c = sh_ref[0, nb_half * 2:nb_half * 2 + 1, :]
    for k in range(nb_half):
        for s in range(t):
            r0 = lat_rows + k * ncc
            xf_s[r0:r0 + ncc, s * LANES:(s + 1) * LANES] = (
                xc_ref[pl.ds(k * n_ctx + s, ncc, stride=t), :] * scale_c + shift_c)

    xb = xf_s[...].astype(BF16)
    dvec = jnp.concatenate([d_ref[...]] * t, axis=1)
    y_s[...] = jnp.dot(xb, wm_s[...], preferred_element_type=F32) + xf_s[...] * dvec

    for d in range(2):
        local = jnp.dot(xb, wb_s[:, d * 2 * sc_cols:(d + 1) * 2 * sc_cols], preferred_element_type=F32)
        for q in range(2 * nst):
            st_s[d, pl.ds(q, n_rows, stride=2 * nst), :] = local[:, q * LANES:(q + 1) * LANES]
    coef = [(at_ref[0, 0, (2 * d) * 2 * nst:(2 * d + 1) * 2 * nst, :],
             at_ref[0, 0, (2 * d + 1) * 2 * nst:(2 * d + 2) * 2 * nst, :]) for d in range(2)]

    def step(chunk_rows, h):
        new = []
        for d in range(2):
            for k in range(nb_half):
                tile = pl.ds(pl.multiple_of((chunk_rows[d] + k * chunk_rows[2]) * 2 * nst, 2 * nst), 2 * nst)
                prev, prev_sw = h[2 * (d * nb_half + k)], h[2 * (d * nb_half + k) + 1]
                local_state = st_s[d, tile, :]
                en_s[d, tile, :] = prev
                new.append(coef[d][0] * prev + coef[d][1] * prev_sw + local_state)
                new.append(coef[d][0] * prev_sw - coef[d][1] * prev + pltpu.roll(local_state, nst, axis=0))
        return tuple(new)

    zero = tuple(jnp.zeros((2 * nst, LANES), F32) for _ in range(4 * nb_half))
    h = lax.fori_loop(0, ncc, lambda i, h: step((lat_rows + i, lat_rows + ncc - 1 - i, ncc), h), zero, unroll=4)
    lax.fori_loop(0, ncl, lambda i, h: step((i, ncl - 1 - i, ncl), h), h, unroll=4)

    entering = jnp.concatenate([en_s[d, pl.ds(q, n_rows, stride=2 * nst), :]
                                for d in range(2) for q in range(2 * nst)], axis=1)
    y_s[...] += jnp.dot(entering.astype(BF16), wc_s[...], preferred_element_type=F32)

    z = jax.nn.gelu(y_s[...])
    for k in range(nb_half):
        for s in range(t):
            zb_s[pl.ds(k * n_lat + s, ncl, stride=t), :] = z[k * ncl:(k + 1) * ncl, s * LANES:(s + 1) * LANES]
    z_ref[...] = zb_s[...].astype(BF16)
    if ctx_out:
        for k in range(nb_half):
            for s in range(t):
                r0 = lat_rows + k * ncc
                zcb_s[pl.ds(k * n_ctx + s, ncc, stride=t), :] = z[r0:r0 + ncc, s * LANES:(s + 1) * LANES]
        zc_ref[...] = zcb_s[...].astype(BF16)


def _s5_core(x, xc, mods_all, layer, d_skip, ops, s5_layer, ctx_out):
    nb, n_lat, d = x.shape
    n_ctx = xc.shape[1]
    nj = d // LANES
    nb_half = nb // 2
    t = S5_CHUNK
    tl = t * LANES
    rows = nb_half * (n_lat + n_ctx) // t
    kern = functools.partial(_s5_kernel, nb_half=nb_half, n_lat=n_lat, n_ctx=n_ctx, ctx_out=ctx_out)
    out_shape = [jax.ShapeDtypeStruct((nb * n_lat, d), BF16)]
    out_specs = [pl.BlockSpec((nb_half * n_lat, LANES), lambda j, h: (h, j))]
    scratch = [pltpu.VMEM((tl, tl), BF16), pltpu.VMEM((tl, 4 * S5_STATE_COLS), BF16),
               pltpu.VMEM((4 * S5_STATE_COLS, tl), BF16),
               pltpu.VMEM((rows, tl), F32), pltpu.VMEM((2, rows * 2 * S5_STATE_COLS // LANES, LANES), F32),
               pltpu.VMEM((2, rows * 2 * S5_STATE_COLS // LANES, LANES), F32),
               pltpu.VMEM((rows, tl), F32), pltpu.VMEM((nb_half * n_lat, LANES), F32)]
    if ctx_out:
        out_shape.append(jax.ShapeDtypeStruct((nb * n_ctx, d), BF16))
        out_specs.append(pl.BlockSpec((nb_half * n_ctx, LANES), lambda j, h: (h, j)))
        scratch.append(pltpu.VMEM((nb_half * n_ctx, LANES), F32))
    res = pl.pallas_call(
        kern,
        grid=(nj, 2),
        in_specs=[
            pl.BlockSpec((nb_half * n_lat, LANES), lambda j, h: (h, j)),
            pl.BlockSpec((nb_half * n_ctx, LANES), lambda j, h: (h, j)),
            pl.BlockSpec((1, MOD_ROWS, LANES), lambda j, h: (layer, 0, j)),
            pl.BlockSpec((1, MOD_ROWS, LANES), lambda j, h: (layer, 0, nj + j)),
            pl.BlockSpec((1, LANES), lambda j, h: (0, j)),
        ] + [pl.BlockSpec((1, 1) + op.shape[2:], lambda j, h: (s5_layer, j, 0, 0)) for op in ops],
        out_specs=out_specs,
        out_shape=out_shape,
        scratch_shapes=scratch,
        compiler_params=_params(("arbitrary", "arbitrary")),
        name="s5_core",
    )(x.reshape(nb * n_lat, d), xc.reshape(nb * n_ctx, d), mods_all, mods_all, d_skip.reshape(1, d), *ops)
    z = res[0].reshape(nb, n_lat, d)
    zc = res[1].reshape(nb, n_ctx, d) if ctx_out else None
    return z, zc


def _s5_out_kernel(z_ref, w_ref, b_ref, x_ref, g_ref, lg_ref, lb_ref, o_ref, *, is_ctx, nb):
    d = x_ref.shape[-1]
    acc = jnp.dot(z_ref[0], w_ref[0], preferred_element_type=F32) + b_ref[...]
    y = acc[:, :d] * jax.nn.sigmoid(acc[:, d:])
    gate = g_ref[0, pl.ds(_mod_row(is_ctx, nb), 1), :]
    o_ref[0] = _layer_norm(DEEPNORM_ALPHA * x_ref[0] + gate * y, lg_ref[...], lb_ref[...])


def _s5_out(z, w_bf, j, b_glu, x, mods_all, layer, ln_g, ln_b, is_ctx, nb):
    n, t, d = x.shape
    tm = min(t, ROW_TILE)
    kern = functools.partial(_s5_out_kernel, is_ctx=is_ctx, nb=nb)
    tok = pl.BlockSpec((1, tm, d), lambda b, i: (b, i, 0))
    return pl.pallas_call(
        kern,
        grid=(n, t // tm),
        in_specs=[
            tok,
            pl.BlockSpec((1, d, 2 * d), lambda b, i: (j, 0, 0), pipeline_mode=pl.Buffered(1)),
            _vec_spec(2 * d),
            tok,
            _mod_spec(layer, d, 2),
            _vec_spec(d), _vec_spec(d),
        ],
        out_specs=tok,
        out_shape=jax.ShapeDtypeStruct((n, t, d), F32),
        compiler_params=_params(("arbitrary", "arbitrary")),
        name="s5_out",
    )(z, w_bf, b_glu.reshape(1, 2 * d), x, mods_all, ln_g.reshape(1, d), ln_b.reshape(1, d))


def _pw1_kernel(x_ref, sh_ref, sc_ref, w_ref, b_ref, o_ref, *, is_ctx, nb):
    d = x_ref.shape[-1]
    row = _mod_row(is_ctx, nb)
    u = x_ref[0] * (1.0 + sc_ref[0, pl.ds(row, 1), :]) + sh_ref[0, pl.ds(row, 1), :]
    acc = jnp.dot(u.astype(BF16), w_ref[0], preferred_element_type=F32) + b_ref[...]
    o_ref[0] = acc[:, :d] * jax.nn.sigmoid(acc[:, d:])


def _pw1(x, mods_all, layer, w_bf, j, b_pw1, is_ctx, nb):
    n, t, d = x.shape
    tm = min(t, ROW_TILE)
    kern = functools.partial(_pw1_kernel, is_ctx=is_ctx, nb=nb)
    tok = pl.BlockSpec((1, tm, d), lambda b, i: (b, i, 0))
    return pl.pallas_call(
        kern,
        grid=(n, t // tm),
        in_specs=[
            tok,
            _mod_spec(layer, d, 0), _mod_spec(layer, d, 1),
            pl.BlockSpec((1, d, 2 * d), lambda b, i: (j, 0, 0), pipeline_mode=pl.Buffered(1)),
            _vec_spec(2 * d),
        ],
        out_specs=tok,
        out_shape=jax.ShapeDtypeStruct((n, t, d), F32),
        compiler_params=_params(("arbitrary", "arbitrary")),
        name="conv_pw1",
    )(x, mods_all, mods_all, w_bf, b_pw1.reshape(1, 2 * d))


def _conv_tail(cv, cg_ref, cb_ref, w2_ref, b2_ref, x, gate, lg_ref, lb_ref):
    hn = _layer_norm(cv, cg_ref[...], cb_ref[...])
    hn = hn * jax.nn.sigmoid(hn)
    y = jnp.dot(hn.astype(BF16), w2_ref[0], preferred_element_type=F32) + b2_ref[...]
    return _layer_norm(DEEPNORM_ALPHA * x + gate * y, lg_ref[...], lb_ref[...])


def _conv_lat_kernel(h_ref, wdw_ref, bdw_ref, cg_ref, cb_ref, w2_ref, b2_ref, x_ref, g_ref, lg_ref, lb_ref,
                     o_ref, wb_s, cv_s, *, n_rows, wt):
    kw = wdw_ref.shape[0]
    pad = kw // 2
    d = h_ref.shape[-1]
    nsub = wt // SUBLANES
    for k in range(kw):
        wb_s[k] = jnp.broadcast_to(wdw_ref[k:k + 1, :], (SUBLANES, d))
    bias = bdw_ref[...]
    for r in range(n_rows):
        accs = [None] * nsub
        for k in range(max(0, pad - r), min(kw, n_rows + pad - r)):
            w8 = wb_s[k]
            for q in range(nsub):
                term = w8 * h_ref[0, r + k - pad, q * SUBLANES:(q + 1) * SUBLANES, :]
                accs[q] = term if accs[q] is None else accs[q] + term
        for q in range(nsub):
            cv_s[r * wt + q * SUBLANES:r * wt + (q + 1) * SUBLANES, :] = accs[q] + bias
    gate = g_ref[0, pl.ds(pl.program_id(0), 1), :]
    x = x_ref[0].reshape(n_rows * wt, d)
    out = _conv_tail(cv_s[...], cg_ref, cb_ref, w2_ref, b2_ref, x, gate, lg_ref, lb_ref)
    o_ref[0] = out.reshape(n_rows, wt, d)


def _conv_ctx_kernel(h_ref, wdw_ref, bdw_ref, cg_ref, cb_ref, w2_ref, b2_ref, x_ref, g_ref, lg_ref, lb_ref,
                     o_ref, hp_s, cv_s, *, n_tok, nb):
    kw = wdw_ref.shape[0]
    pad = kw // 2
    d = h_ref.shape[-1]
    lead = 2 * SUBLANES
    rblk = CTX_CONV_ROWS
    cblk = CTX_CONV_COLS
    hp_s[0:lead] = jnp.zeros((lead, d), F32)
    hp_s[lead + n_tok:lead + n_tok + lead] = jnp.zeros((lead, d), F32)
    hp_s[lead:lead + n_tok] = h_ref[0]
    span = ((kw - 1 + lead - pad) // SUBLANES) * SUBLANES

    def col_block(lc, carry):
        cols = pl.ds(pl.multiple_of(lc * cblk, cblk), cblk)
        for rc in range(n_tok // rblk):
            acc = jnp.zeros((rblk, cblk), F32)
            for q in range(SUBLANES):
                taps = [k for k in range(kw) if (k + lead - pad) % SUBLANES == q]
                if not taps:
                    continue
                shifted = hp_s[pl.ds(rc * rblk + q, rblk + span), cols]
                for k in taps:
                    o = k + lead - pad - q
                    acc = acc + wdw_ref[k:k + 1, cols] * shifted[o:o + rblk]
            cv_s[rc * rblk:(rc + 1) * rblk, cols] = acc + bdw_ref[:, cols]
        return carry

    lax.fori_loop(0, d // cblk, col_block, 0)
    gate = g_ref[0, nb:nb + 1, :]
    o_ref[0] = _conv_tail(cv_s[...], cg_ref, cb_ref, w2_ref, b2_ref, x_ref[0], gate, lg_ref, lb_ref)


def _conv_post(h, w_dw, b_dw, cv_g, cv_b, w2_bf, j, b2, x, mods_all, layer, ln_g, ln_b, n_rows, is_ctx, nb):
    n, t, d = x.shape
    kw = w_dw.shape[0]
    weights = [pl.BlockSpec((kw, d), lambda b, i: (0, 0)), _vec_spec(d), _vec_spec(d), _vec_spec(d),
               pl.BlockSpec((1, d, d), lambda b, i: (j, 0, 0), pipeline_mode=pl.Buffered(1)), _vec_spec(d)]
    tail = [_mod_spec(layer, d, 2), _vec_spec(d), _vec_spec(d)]
    args_w = (w_dw, b_dw.reshape(1, d), cv_g.reshape(1, d), cv_b.reshape(1, d), w2_bf, b2.reshape(1, d))
    args_t = (mods_all, ln_g.reshape(1, d), ln_b.reshape(1, d))
    if is_ctx:
        blk = pl.BlockSpec((1, t, d), lambda b, i: (b, 0, 0))
        return pl.pallas_call(
            functools.partial(_conv_ctx_kernel, n_tok=t, nb=nb),
            grid=(n, 1),
            in_specs=[blk] + weights + [blk] + tail,
            out_specs=blk,
            out_shape=jax.ShapeDtypeStruct((n, t, d), F32),
            scratch_shapes=[pltpu.VMEM((t + 4 * SUBLANES, d), F32), pltpu.VMEM((t, d), F32)],
            compiler_params=_params(("arbitrary", "arbitrary")),
            name="conv_post_ctx",
        )(h, *args_w, x, *args_t)
    wt = CONV_WT
    width = t // n_rows
    blk = pl.BlockSpec((1, n_rows, wt, d), lambda b, i: (b, 0, i, 0))
    out = pl.pallas_call(
        functools.partial(_conv_lat_kernel, n_rows=n_rows, wt=wt),
        grid=(n, width // wt),
        in_specs=[blk] + weights + [blk] + tail,
        out_specs=blk,
        out_shape=jax.ShapeDtypeStruct((n, n_rows, width, d), F32),
        scratch_shapes=[pltpu.VMEM((kw, SUBLANES, d), F32),
                        pltpu.VMEM((n_rows * wt, d), F32)],
        compiler_params=_params(("arbitrary", "arbitrary")),
        name="conv_post",
    )(h.reshape(n, n_rows, width, d), *args_w, x.reshape(n, n_rows, width, d), *args_t)
    return out.reshape(n, t, d)


def _router_kernel(x_ref, sh_ref, sc_ref, wr_ref, a_ref, *, is_ctx, nb):
    row = _mod_row(is_ctx, nb)
    u = x_ref[0] * (1.0 + sc_ref[0, pl.ds(row, 1), :]) + sh_ref[0, pl.ds(row, 1), :]
    uh, ul = _split_bf16(u)
    wh, wl = _split_bf16(wr_ref[0])
    nt = (((1,), (1,)), ((), ()))
    logits = (lax.dot_general(wh, uh, nt, preferred_element_type=F32)
              + lax.dot_general(wh, ul, nt, preferred_element_type=F32)
              + lax.dot_general(wl, uh, nt, preferred_element_type=F32))
    m = jnp.max(logits, axis=0, keepdims=True)
    ex = jnp.exp(logits - m)
    a_ref[0] = ex / jnp.sum(ex, axis=0, keepdims=True)


def _router(x, mods_all, layer, w_router_t, is_ctx, nb):
    n, t, d = x.shape
    e = w_router_t.shape[1]
    tm = min(t, ROW_TILE)
    kern = functools.partial(_router_kernel, is_ctx=is_ctx, nb=nb)
    tok = pl.BlockSpec((1, tm, d), lambda b, i: (b, i, 0))
    return pl.pallas_call(
        kern,
        grid=(n, t // tm),
        in_specs=[tok, _mod_spec(layer, d, 3), _mod_spec(layer, d, 4),
                  pl.BlockSpec((1, e, d), lambda b, i: (layer, 0, 0))],
        out_specs=pl.BlockSpec((1, e, tm), lambda b, i: (b, 0, i)),
        out_shape=jax.ShapeDtypeStruct((n, e, t), F32),
        compiler_params=_params(("arbitrary", "arbitrary")),
        name="router",
    )(x, mods_all, mods_all, w_router_t)


def _topk_kernel(a_ref, pt_ref, idx_ref, gate_ref, *, cap):
    a = a_ref[0]
    e, t = a.shape
    capf = jnp.float32(cap)

    def count(mask):
        return jnp.sum(jnp.where(mask, 1.0, 0.0), axis=1, keepdims=True)

    def as_row_values(bits):
        return jnp.concatenate([pltpu.bitcast(bits, F32)] * (t // LANES), axis=1)

    thr_bits = jnp.zeros((e, LANES), jnp.int32)
    for bit in range(30, -1, -1):
        cand = thr_bits | jnp.int32(1 << bit)
        keep = count(a >= as_row_values(cand)) >= capf
        thr_bits = jnp.where(keep, cand, thr_bits)
    thr = as_row_values(thr_bits)
    gt = a > thr
    eq = a == thr
    need = capf - count(gt)
    tri = jnp.where(_iota((t, t), 0) <= _iota((t, t), 1), 1.0, 0.0).astype(BF16)
    eq_f = jnp.where(eq, 1.0, 0.0)
    eq_rank = jnp.dot(eq_f.astype(BF16), tri, preferred_element_type=F32) - eq_f
    sel = jnp.where(gt, 1.0, jnp.where(eq & (eq_rank < need), 1.0, 0.0))
    slot = jnp.dot(sel.astype(BF16), tri, preferred_element_type=F32) - 1.0
    pos = jnp.where(sel > 0.0, slot, -1.0)
    padded = jnp.concatenate([pos, jnp.full((LANES - e, t), -1.0, F32)], axis=0)
    pos_tm = padded.T
    pt_ref[0] = pos_tm.astype(jnp.int32)
    tok = _iota((1, t), 1)
    tok_hi = (tok >> TOK_SPLIT_BITS).astype(F32)
    tok_lo = (tok & ((1 << TOK_SPLIT_BITS) - 1)).astype(F32)
    slots = _iota((cap, 1), 0).astype(F32)
    contract_tokens = (((1,), (1,)), ((), ()))
    for ei in range(e):
        onehot = jnp.where(pos[ei:ei + 1, :] == slots, 1.0, 0.0).astype(BF16)
        g = a[ei:ei + 1, :]
        g_hi = g.astype(BF16).astype(F32)
        g_mid = (g - g_hi).astype(BF16).astype(F32)
        g_lo = (g - g_hi) - g_mid
        lhs = jnp.concatenate([tok_hi, tok_lo, g_hi, g_mid, g_lo, jnp.zeros((SUBLANES - 5, t), F32)], axis=0)
        res = lax.dot_general(lhs.astype(BF16), onehot, contract_tokens, preferred_element_type=F32)
        idx_ref[0, ei:ei + 1, :] = (res[0:1] * float(1 << TOK_SPLIT_BITS) + res[1:2]).astype(jnp.int32)
        gate_ref[0, ei:ei + 1, :] = res[2:3] + res[3:4] + res[4:5]


def _topk(aff_t, cap):
    n, e, t = aff_t.shape
    kern = functools.partial(_topk_kernel, cap=cap)
    return pl.pallas_call(
        kern,
        grid=(n,),
        in_specs=[pl.BlockSpec((1, e, t), lambda b: (b, 0, 0))],
        out_specs=[pl.BlockSpec((1, t, LANES), lambda b: (b, 0, 0)),
                   pl.BlockSpec((1, e, cap), lambda b: (b, 0, 0)),
                   pl.BlockSpec((1, e, cap), lambda b: (b, 0, 0))],
        out_shape=[jax.ShapeDtypeStruct((n, t, LANES), jnp.int32), jax.ShapeDtypeStruct((n, e, cap), jnp.int32),
                   jax.ShapeDtypeStruct((n, e, cap), F32)],
        compiler_params=_params(("arbitrary",)),
        name="topk",
    )(aff_t)


def _ffn_kernel(idx_ref, *refs, rows, caps, mod_rows, n_exp):
    n_streams = len(rows)
    src_refs = refs[:n_streams]
    sh_ref, sc_ref, gate_ref, wg_ref, wu_ref, wo_ref = refs[n_streams:n_streams + 6]
    y_refs = refs[n_streams + 6:2 * n_streams + 6]
    act_s, wo_s, x_s, land_s, sem = refs[2 * n_streams + 6:]
    e = pl.program_id(0)
    fc = pl.program_id(1)
    nfc = act_s.shape[0]
    r_tot = sum(rows)
    bounds = [0]
    for r in rows:
        bounds.append(bounds[-1] + r)

    def row_copy(expert, row, k):
        tok = idx_ref[expert * r_tot + row]
        return pltpu.make_async_copy(src_refs[k].at[pl.ds(tok, 1), :], land_s.at[pl.ds(row, 1), :], sem)

    def wait_all_rows():
        pltpu.make_async_copy(src_refs[0].at[pl.ds(0, r_tot), :], land_s, sem).wait()

    @pl.when((e == 0) & (fc == 0))
    def _():
        for k in range(n_streams):
            def issue(row, carry, k=k):
                row_copy(0, row, k).start()
                return carry
            lax.fori_loop(bounds[k], bounds[k + 1], issue, 0)

    @pl.when(fc == 0)
    def _():
        wait_all_rows()
        for k in range(n_streams):
            for b, m in enumerate(mod_rows[k]):
                r0 = bounds[k] + b * caps[k]
                scale = 1.0 + sc_ref[0, m:m + 1, :]
                x_s[r0:r0 + caps[k], :] = (land_s[r0:r0 + caps[k], :] * scale + sh_ref[0, m:m + 1, :]).astype(BF16)

    nxt = lax.rem(e + 1, n_exp)
    for k in range(n_streams):
        share = rows[k] // nfc
        for i in range(share):
            row_copy(nxt, bounds[k] + fc * share + i, k).start(priority=i % 2)

    x = x_s[...]
    g = jnp.dot(x, wg_ref[0, 0].astype(BF16), preferred_element_type=F32)
    up = jnp.dot(x, wu_ref[0, 0].astype(BF16), preferred_element_type=F32)
    act_s[fc] = ((g * jax.nn.sigmoid(g)) * up).astype(BF16)
    wo_s[fc] = wo_ref[0, 0].astype(BF16)

    @pl.when(fc == nfc - 1)
    def _():
        w_out = wo_s[...].reshape(nfc * wo_s.shape[1], wo_s.shape[2])
        gate_col = jnp.broadcast_to(gate_ref[0], (LANES, r_tot)).T[:, 0:1]
        half = r_tot // 2
        for r0, r1 in ((0, half), (half, r_tot)):
            act = jnp.concatenate([act_s[c, r0:r1, :] for c in range(nfc)], axis=1)
            y = (jnp.dot(act, w_out, preferred_element_type=F32) * gate_col[r0:r1]).astype(BF16)
            for k in range(n_streams):
                lo, hi = max(r0, bounds[k]), min(r1, bounds[k + 1])
                if lo < hi:
                    y_refs[k][0, lo - bounds[k]:hi - bounds[k], :] = y[lo - r0:hi - r0]

    @pl.when((e == n_exp - 1) & (fc == nfc - 1))
    def _():
        wait_all_rows()


def _ffn(streams, mod_rows, mods_all, idx, gates, w_in, w_out, layer):
    d = streams[0].shape[2]
    e = gates.shape[0]
    ff = w_out.shape[2]
    nfc = ff // FF_TILE
    caps = tuple(EC_CAPACITY_FACTOR * s.shape[1] // N_EXPERTS for s in streams)
    rows = tuple(s.shape[0] * c for s, c in zip(streams, caps))
    r_tot = sum(rows)
    assert all(r % nfc == 0 for r in rows) and r_tot % LANES == 0
    return pl.pallas_call(
        functools.partial(_ffn_kernel, rows=rows, caps=caps, mod_rows=mod_rows, n_exp=e),
        grid_spec=pltpu.PrefetchScalarGridSpec(
            num_scalar_prefetch=1,
            grid=(e, nfc),
            in_specs=[pl.BlockSpec(memory_space=pl.ANY) for _ in streams] + [
                pl.BlockSpec((1, MOD_ROWS, d), lambda ei, fc, ix: (layer, 0, 3)),
                pl.BlockSpec((1, MOD_ROWS, d), lambda ei, fc, ix: (layer, 0, 4)),
                pl.BlockSpec((1, 1, r_tot), lambda ei, fc, ix: (ei, 0, 0)),
                pl.BlockSpec((1, 1, d, FF_TILE), lambda ei, fc, ix: (layer, ei, 0, fc)),
                pl.BlockSpec((1, 1, d, FF_TILE), lambda ei, fc, ix: (layer, ei, 0, nfc + fc)),
                pl.BlockSpec((1, 1, FF_TILE, d), lambda ei, fc, ix: (layer, ei, fc, 0))],
            out_specs=[pl.BlockSpec((1, r, d), lambda ei, fc, ix: (ei, 0, 0)) for r in rows],
            scratch_shapes=[pltpu.VMEM((nfc, r_tot, FF_TILE), BF16), pltpu.VMEM((nfc, FF_TILE, d), BF16),
                            pltpu.VMEM((r_tot, d), BF16), pltpu.VMEM((r_tot, d), F32),
                            pltpu.SemaphoreType.DMA(())],
        ),
        out_shape=[jax.ShapeDtypeStruct((e, r, d), BF16) for r in rows],
        compiler_params=_params(("arbitrary", "arbitrary")),
        name="expert_ffn",
    )(idx, *[s.reshape(s.shape[0] * s.shape[1], d) for s in streams], mods_all, mods_all, gates, w_in, w_in, w_out)


def _combine_kernel(y_ref, pt_ref, x_ref, g_ref, lg_ref, lb_ref, o_ref, oh_s, *, is_ctx, nb, cap):
    n_exp, _, d = y_ref.shape
    ec = n_exp * cap
    cw = min(ec, COMBINE_COLS)
    pos = pt_ref[0].astype(F32).astype(BF16)
    for c0 in range(0, ec, cw):
        src = _iota((LANES, cw), 0)
        dst = _iota((LANES, cw), 1) + c0
        expand = jnp.where((dst >= src * cap) & (dst < (src + 1) * cap), 1.0, 0.0).astype(BF16)
        pe = jnp.dot(pos, expand, preferred_element_type=F32)
        slot = (_iota((1, cw), 1) + c0) % cap
        oh_s[:, c0:c0 + cw] = jnp.where(pe == slot.astype(F32), 1.0, 0.0).astype(BF16)
    f = jnp.dot(oh_s[...], y_ref[...].reshape(ec, d), preferred_element_type=F32)
    gate = g_ref[0, pl.ds(_mod_row(is_ctx, nb), 1), :]
    o_ref[0] = _layer_norm(DEEPNORM_ALPHA * x_ref[0] + gate * f, lg_ref[...], lb_ref[...])


def _combine(y, pos_tm, x, mods_all, layer, ln_g, ln_b, cap, is_ctx, nb):
    n, t, d = x.shape
    e = y.shape[0]
    tm = min(t, ROW_TILE)
    kern = functools.partial(_combine_kernel, is_ctx=is_ctx, nb=nb, cap=cap)
    tok = pl.BlockSpec((1, tm, d), lambda b, i: (b, i, 0))
    return pl.pallas_call(
        kern,
        grid=(n, t // tm),
        in_specs=[
            pl.BlockSpec((e, cap, d), lambda b, i: (0, b, 0), pipeline_mode=pl.Buffered(1)),
            pl.BlockSpec((1, tm, LANES), lambda b, i: (b, i, 0)),
            tok,
            _mod_spec(layer, d, 5),
            _vec_spec(d), _vec_spec(d),
        ],
        out_specs=tok,
        out_shape=jax.ShapeDtypeStruct((n, t, d), F32),
        scratch_shapes=[pltpu.VMEM((tm, e * cap), BF16)],
        compiler_params=_params(("arbitrary", "arbitrary")),
        name="combine",
    )(y, pos_tm, x, mods_all, ln_g.reshape(1, d), ln_b.reshape(1, d))


def _moe_post(x1_lat, x1_ctx, mods_all, layer, w_router_t, w_in, w_out, ln_g, ln_b, nb):
    streams = [(x1_lat, False)] + ([(x1_ctx, True)] if x1_ctx is not None else [])
    caps = [EC_CAPACITY_FACTOR * s.shape[1] // N_EXPERTS for s, _ in streams]
    routed, tokens, mod_rows, idx_parts, gate_parts = [], [], [], [], []
    for (s, is_ctx), cap in zip(streams, caps):
        n, t = s.shape[:2]
        aff_t = _router(s, mods_all, layer, w_router_t, is_ctx, nb)
        pos_tm, idx, gate = _topk(aff_t, cap)
        routed.append(pos_tm)
        tokens.append(s)
        mod_rows.append(tuple(nb if is_ctx else b for b in range(n)))
        flat = idx + (jnp.arange(n, dtype=jnp.int32) * t)[:, None, None]
        idx_parts.append(jnp.swapaxes(flat, 0, 1).reshape(N_EXPERTS, n * cap))
        gate_parts.append(jnp.swapaxes(gate, 0, 1).reshape(N_EXPERTS, n * cap))
    idx_all = jnp.concatenate(idx_parts, axis=1).reshape(-1)
    gates = jnp.concatenate(gate_parts, axis=1)[:, None, :]
    ys = _ffn(tokens, tuple(mod_rows), mods_all, idx_all, gates, w_in, w_out, layer)
    outs = [_combine(y, pos_tm, s, mods_all, layer, ln_g, ln_b, cap, is_ctx, nb)
            for (s, is_ctx), pos_tm, cap, y in zip(streams, routed, caps, ys)]
    return outs[0], (outs[1] if len(outs) > 1 else None)


def kernel(x, c, ctx, c_ctx, ada_w, ada_b, ln_g, ln_b, s5_a_re, s5_a_im, s5_log_dt, s5_b_re, s5_b_im,
           s5_c_re, s5_c_im, s5_d, s5_w_glu, s5_b_glu, cv_w_pw1, cv_b_pw1, cv_w_dw, cv_b_dw, cv_ln_g,
           cv_ln_b, cv_w_pw2, cv_b_pw2, moe_w_router, moe_w_in, moe_w_out):
    nb, seq, d = x.shape
    assert nb + 1 <= MOD_ROWS and nb % 2 == 0 and d % LANES == 0
    rows = seq // GRID_W

    c8 = jnp.concatenate([c.astype(F32), c_ctx.astype(F32)[None], jnp.zeros((MOD_ROWS - nb - 1, d), F32)], axis=0)
    mods_all = _ada_all(c8, ada_w, ada_b)
    w_glu_bf = s5_w_glu.astype(BF16)
    w_pw1_bf = cv_w_pw1.astype(BF16)
    w_pw2_bf = cv_w_pw2.astype(BF16)
    w_router_t = jnp.swapaxes(moe_w_router, 1, 2)
    s5_ops = jax.vmap(_s5_operators)(s5_a_re, s5_a_im, s5_log_dt, s5_b_re, s5_b_im, s5_c_re, s5_c_im)

    x_lat, x_ctx = x, ctx
    for i in range(DEPTH):
        is_s5 = (i % N_MIXERS) == 0
        j = i // N_MIXERS
        ctx_out = any((k % N_MIXERS) == 0 for k in range(i + 1, DEPTH))
        x1_ctx = None

        if is_s5:
            z_lat, z_ctx = _s5_core(x_lat, x_ctx, mods_all, i, s5_d[j], s5_ops, j, ctx_out)
            x1_lat = _s5_out(z_lat, w_glu_bf, j, s5_b_glu[j], x_lat, mods_all, i, ln_g[i, 0], ln_b[i, 0], False, nb)
            if ctx_out:
                x1_ctx = _s5_out(z_ctx, w_glu_bf, j, s5_b_glu[j], x_ctx, mods_all, i, ln_g[i, 0], ln_b[i, 0],
                                 True, nb)
        else:
            conv_args = (cv_w_dw[j], cv_b_dw[j], cv_ln_g[j], cv_ln_b[j], w_pw2_bf, j, cv_b_pw2[j])
            h_lat = _pw1(x_lat, mods_all, i, w_pw1_bf, j, cv_b_pw1[j], False, nb)
            x1_lat = _conv_post(h_lat, *conv_args, x_lat, mods_all, i, ln_g[i, 0], ln_b[i, 0], rows, False, nb)
            if ctx_out:
                h_ctx = _pw1(x_ctx, mods_all, i, w_pw1_bf, j, cv_b_pw1[j], True, nb)
                x1_ctx = _conv_post(h_ctx, *conv_args, x_ctx, mods_all, i, ln_g[i, 0], ln_b[i, 0], None, True, nb)

        x_lat, x_ctx_new = _moe_post(x1_lat, x1_ctx, mods_all, i, w_router_t, moe_w_in, moe_w_out,
                                     ln_g[i, 1], ln_b[i, 1], nb)
        if ctx_out:
            x_ctx = x_ctx_new
    return x_lat
```
